```python
import functools
import jax
import jax.numpy as jnp
from jax import lax
import numpy as np

D_MODEL = 2048
BATCH = 4
SEQ = 2048
DEPTH = 4
DEC_BATCH = 8
DEC_SEQ = 1
PAST_LEN = 16384
PAGE_SIZE = 128

N_MIXERS = 3
N_HEADS = 16
HEAD_DIM = D_MODEL // N_HEADS
ATTN_DIM = N_HEADS * HEAD_DIM
NSA_KV_HEADS = 4
NSA_GROUP = N_HEADS // NSA_KV_HEADS
NSA_KV_DIM = NSA_KV_HEADS * HEAD_DIM
CMP_BLOCK = 64
CMP_HIDDEN = 256
N_SELECT = 16
WINDOW = 512
NSA_IN_DIM = ATTN_DIM + 6 * NSA_KV_DIM + 3 * N_HEADS
MOBA_KV_HEADS = 4
MOBA_GROUP = N_HEADS // MOBA_KV_HEADS
MOBA_KV_DIM = MOBA_KV_HEADS * HEAD_DIM
MOBA_BLOCK = 256
MOBA_TOPK = 3
D_FF = -((-8 * D_MODEL) // (3 * 256)) * 256
N_NSA_LAYERS = (DEPTH + 2) // 3
N_MOBA_LAYERS = (DEPTH + 1) // 3
N_SB_LAYERS = DEPTH // 3
WIN_QBLOCK = 128
SLC_QBLOCK = 32
MOBA_QBLOCK = 8
SB_QBLOCK = 128
ROPE_THETA = 10000.0
RMS_EPS = 1e-6
NEG_INF = -1e30
SCALE = HEAD_DIM ** -0.5

kernel_name = 'nsa_moba_stickbreak_hybrid_step'


def rms_norm(x, gain):
    xf = x.astype(jnp.float32)
    y = xf * lax.rsqrt(jnp.mean(xf * xf, axis=-1, keepdims=True) + RMS_EPS)
    return (y * gain.astype(jnp.float32)).astype(x.dtype)


def rope(x, pos):
    half = HEAD_DIM // 2
    inv_freq = ROPE_THETA ** (-jnp.arange(half, dtype=jnp.float32) / half)
    ang = pos.astype(jnp.float32)[:, None] * inv_freq[None, :]
    cos = jnp.cos(ang)[:, None, :]
    sin = jnp.sin(ang)[:, None, :]
    xf = x.astype(jnp.float32)
    x1, x2 = xf[..., :half], xf[..., half:]
    return jnp.concatenate([x1 * cos - x2 * sin, x2 * cos + x1 * sin], axis=-1).astype(x.dtype)


def masked_softmax(s, mask):
    s = jnp.where(mask, s.astype(jnp.float32), NEG_INF)
    return jnp.where(mask, jax.nn.softmax(s, axis=-1), 0.0)


def swiglu(x, w_gate, w_up, w_down):
    return (jax.nn.silu(x @ w_gate) * (x @ w_up)) @ w_down


def split_heads(x, n_heads):
    return x.reshape(x.shape[0], x.shape[1], n_heads, HEAD_DIM)


def merge_heads(o, w_out):
    return o.reshape(o.shape[0], o.shape[1], ATTN_DIM) @ w_out


def past_rows(cache, layer, page_table, slot):
    rows = cache[layer, page_table, :, slot]
    return rows.reshape(rows.shape[0], rows.shape[1] * rows.shape[2], rows.shape[3], rows.shape[4])


def to_blocks(x, block):
    B, L = x.shape[:2]
    nb = -(-L // block)
    x = jnp.pad(x, ((0, 0), (0, nb * block - L), (0, 0), (0, 0)))
    return x.reshape(B, nb, block, x.shape[2], HEAD_DIM).transpose(0, 3, 1, 2, 4)


def sweep_queries(fn, block, *xs):
    tq = xs[0].shape[1]
    if tq <= block or tq % block:
        return fn(*xs)
    nb = tq // block
    split = lambda x: jnp.moveaxis(x.reshape(x.shape[0], nb, block, *x.shape[2:]), 1, 0)
    out = lax.map(lambda a: fn(*a), tuple(split(x) for x in xs))
    return jnp.moveaxis(out, 0, 1).reshape(out.shape[1], tq, *out.shape[3:])


def nsa_project(h, pos, w_in, q_gain, ks_gain, kw_gain):
    offs = [ATTN_DIM + i * NSA_KV_DIM for i in range(7)]
    q, kc, vc, ks, vs, kw, vw, g = jnp.split(h @ w_in, offs, axis=-1)
    q = rope(rms_norm(split_heads(q, N_HEADS), q_gain), pos)
    ks = rope(rms_norm(split_heads(ks, NSA_KV_HEADS), ks_gain), pos)
    kw = rope(rms_norm(split_heads(kw, NSA_KV_HEADS), kw_gain), pos)
    gates = jax.nn.sigmoid(g.reshape(h.shape[0], h.shape[1], N_HEADS, 3))
    rows = jnp.stack([split_heads(kc, NSA_KV_HEADS), split_heads(vc, NSA_KV_HEADS), ks,
                      split_heads(vs, NSA_KV_HEADS)], axis=2)
    win_rows = jnp.stack([kw, split_heads(vw, NSA_KV_HEADS)], axis=2)
    return q, rows, win_rows, gates


def nsa_compress(rows, pos_emb, w1, w2):
    B, L = rows.shape[:2]
    nc = L // CMP_BLOCK
    blk = rows[:, :nc * CMP_BLOCK].reshape(B, nc, CMP_BLOCK, NSA_KV_HEADS, HEAD_DIM) + pos_emb[:, None, :]
    blk = jnp.swapaxes(blk, 2, 3).reshape(B, nc, NSA_KV_HEADS, CMP_BLOCK * HEAD_DIM)
    return jax.nn.gelu(blk @ w1) @ w2


def nsa_selected(q, pos, idx, valid, k_blk, v_blk):
    t = pos[0]
    B, T = q.shape[:2]
    n = idx.shape[-1]
    bi = jnp.arange(B)[:, None, None, None]
    gi = jnp.arange(NSA_KV_HEADS)[None, None, :, None]
    kg = k_blk[bi, gi, idx]
    vg = v_blk[bi, gi, idx].reshape(B, T, NSA_KV_HEADS, n * CMP_BLOCK, HEAD_DIM)
    qg = q.reshape(B, T, NSA_KV_HEADS, NSA_GROUP, HEAD_DIM)
    s = jnp.einsum('btgrd,btgnkd->btgrnk', qg, kg).reshape(B, T, NSA_KV_HEADS, NSA_GROUP, n * CMP_BLOCK) * SCALE
    kpos = idx[..., None] * CMP_BLOCK + jnp.arange(CMP_BLOCK, dtype=jnp.int32)
    mask = (valid[..., None] & (kpos <= t[None, :, None, None, None])).reshape(B, T, NSA_KV_HEADS, 1, n * CMP_BLOCK)
    p = masked_softmax(s, mask)
    o = jnp.einsum('btgrm,btgmd->btgrd', p.astype(vg.dtype), vg)
    return o.reshape(B, T, N_HEADS, HEAD_DIM)


def window_attend(q, t, k, v, kpos):
    B, T = q.shape[:2]
    qg = q.reshape(B, T, NSA_KV_HEADS, NSA_GROUP, HEAD_DIM)
    s = jnp.einsum('btgrd,bsgd->btgrs', qg, k) * SCALE
    d = t[:, None] - kpos[None, :]
    mask = ((d >= 0) & (d < WINDOW) & (kpos[None, :] >= 0))[None, :, None, None, :]
    p = masked_softmax(s, mask)
    return jnp.einsum('btgrs,bsgd->btgrd', p.astype(v.dtype), v).reshape(B, T, N_HEADS, HEAD_DIM)


def window_prompt(q, pos, k, v):
    kp = jnp.pad(k, ((0, 0), (WINDOW, 0), (0, 0), (0, 0)))
    vp = jnp.pad(v, ((0, 0), (WINDOW, 0), (0, 0), (0, 0)))

    def band(qb, pb):
        t = pb[0]
        start = t[0]
        n = qb.shape[1] + WINDOW
        kb = lax.dynamic_slice_in_dim(kp, start, n, axis=1)
        vb = lax.dynamic_slice_in_dim(vp, start, n, axis=1)
        kpos = start - WINDOW + jnp.arange(n, dtype=jnp.int32)
        return window_attend(qb, t, kb, vb, kpos)

    return sweep_queries(band, WIN_QBLOCK, q, pos[None])


def nsa_mix(q, pos, gates, kc_raw, vc_raw, ks, vs, o_win, kc_gain, cmp_pos, cmp_w1, cmp_w2):
    B, T = q.shape[:2]
    L = kc_raw.shape[1]
    qg = q.reshape(B, T, NSA_KV_HEADS, NSA_GROUP, HEAD_DIM)
    nc = L // CMP_BLOCK
    cpos = jnp.arange(nc, dtype=jnp.int32) * CMP_BLOCK + (CMP_BLOCK - 1)
    k_cmp = rope(rms_norm(nsa_compress(kc_raw, cmp_pos[0], cmp_w1[0], cmp_w2[0]), kc_gain), cpos)
    v_cmp = nsa_compress(vc_raw, cmp_pos[1], cmp_w1[1], cmp_w2[1])
    s_cmp = jnp.einsum('btgrd,bcgd->btgrc', qg, k_cmp) * SCALE
    p_cmp = masked_softmax(s_cmp, (cpos[None, :] <= pos[:, None])[None, :, None, None, :])
    o_cmp = jnp.einsum('btgrc,bcgd->btgrd', p_cmp.astype(v_cmp.dtype), v_cmp).reshape(B, T, N_HEADS, HEAD_DIM)
    imp = jnp.sum(p_cmp, axis=3)
    cur = pos // CMP_BLOCK
    cand = (jnp.arange(nc, dtype=jnp.int32)[None, :] < cur[:, None])[None, :, None, :]
    _, top = lax.top_k(jnp.where(cand, imp, NEG_INF), min(N_SELECT - 1, nc))
    own = jnp.broadcast_to(cur[None, :, None, None], (B, T, NSA_KV_HEADS, 1))
    idx = jnp.concatenate([top.astype(jnp.int32), own], axis=-1)
    valid = jnp.concatenate([top < cur[None, :, None, None], jnp.ones(own.shape, dtype=bool)], axis=-1)
    k_blk, v_blk = to_blocks(ks, CMP_BLOCK), to_blocks(vs, CMP_BLOCK)
    o_slc = sweep_queries(functools.partial(nsa_selected, k_blk=k_blk, v_blk=v_blk), SLC_QBLOCK, q, pos[None], idx, valid)
    return gates[..., 0:1] * o_cmp + gates[..., 1:2] * o_slc + gates[..., 2:3] * o_win


def nsa_layer(hp, hs, cache, layer, page_table, win_buf, w_in, w_out, q_gain, kc_gain, ks_gain, kw_gain,
              cmp_pos, cmp_w1, cmp_w2):
    tp, ts = hp.shape[1], hs.shape[1]
    pos_p = jnp.arange(tp, dtype=jnp.int32)
    qp, rows_p, win_p, gp = nsa_project(hp, pos_p, w_in, q_gain, ks_gain, kw_gain)
    o_win_p = window_prompt(qp, pos_p, win_p[:, :, 0], win_p[:, :, 1])
    op = nsa_mix(qp, pos_p, gp, rows_p[:, :, 0], rows_p[:, :, 1], rows_p[:, :, 2], rows_p[:, :, 3], o_win_p,
                 kc_gain, cmp_pos, cmp_w1, cmp_w2)
    past_len = page_table.shape[1] * cache.shape[2]
    pos_s = past_len + jnp.arange(ts, dtype=jnp.int32)
    qs, rows_s, win_s, gs = nsa_project(hs, pos_s, w_in, q_gain, ks_gain, kw_gain)
    full = [jnp.concatenate([past_rows(cache, layer, page_table, c), rows_s[:, :, c]], axis=1) for c in range(4)]
    n_win = win_buf.shape[1]
    buf = jnp.concatenate([win_buf, win_s], axis=1)
    kpos = past_len - n_win + jnp.arange(n_win + ts, dtype=jnp.int32)
    o_win_s = window_attend(qs, pos_s, buf[:, :, 0], buf[:, :, 1], kpos)
    os_ = nsa_mix(qs, pos_s, gs, full[0], full[1], full[2], full[3], o_win_s, kc_gain, cmp_pos, cmp_w1, cmp_w2)
    return (merge_heads(op, w_out), merge_heads(os_, w_out), rows_p, rows_s,
            win_p[:, -min(WINDOW, tp):], buf[:, -n_win:])


def moba_project(h, pos, w_in, q_gain, k_gain):
    q, k, v = jnp.split(h @ w_in, [ATTN_DIM, ATTN_DIM + MOBA_KV_DIM], axis=-1)
    q = rope(rms_norm(split_heads(q, N_HEADS), q_gain), pos)
    k = rope(rms_norm(split_heads(k, MOBA_KV_HEADS), k_gain), pos)
    return q, jnp.stack([k, split_heads(v, MOBA_KV_HEADS)], axis=2)


def moba_selected(q, pos, idx, valid, k_blk, v_blk):
    t = pos[0]
    B, T = q.shape[:2]
    n = idx.shape[-1]
    bi = jnp.arange(B)[:, None, None, None]
    gi = (jnp.arange(N_HEADS) // MOBA_GROUP)[None, None, :, None]
    kg = k_blk[bi, gi, idx]
    vg = v_blk[bi, gi, idx].reshape(B, T, N_HEADS, n * MOBA_BLOCK, HEAD_DIM)
    s = jnp.einsum('bthd,bthnkd->bthnk', q, kg).reshape(B, T, N_HEADS, n * MOBA_BLOCK) * SCALE
    kpos = idx[..., None] * MOBA_BLOCK + jnp.arange(MOBA_BLOCK, dtype=jnp.int32)
    mask = (valid[..., None] & (kpos <= t[None, :, None, None, None])).reshape(B, T, N_HEADS, n * MOBA_BLOCK)
    p = masked_softmax(s, mask)
    return jnp.einsum('bthm,bthmd->bthd', p.astype(vg.dtype), vg)


def moba_attend(q, pos, k, v):
    B, T = q.shape[:2]
    k_blk, v_blk = to_blocks(k, MOBA_BLOCK), to_blocks(v, MOBA_BLOCK)
    nb = k_blk.shape[2]
    k_mean = jnp.mean(k_blk.astype(jnp.float32), axis=3)
    qg = q.reshape(B, T, MOBA_KV_HEADS, MOBA_GROUP, HEAD_DIM).astype(jnp.float32)
    gate = jnp.einsum('btgrd,bgnd->btgrn', qg, k_mean).reshape(B, T, N_HEADS, nb)
    cur = pos // MOBA_BLOCK
    cand = (jnp.arange(nb, dtype=jnp.int32)[None, :] < cur[:, None])[None, :, None, :]
    _, top = lax.top_k(jnp.where(cand, gate, NEG_INF), min(MOBA_TOPK, nb))
    own = jnp.broadcast_to(cur[None, :, None, None], (B, T, N_HEADS, 1))
    idx = jnp.concatenate([top.astype(jnp.int32), own], axis=-1)
    valid = jnp.concatenate([top < cur[None, :, None, None], jnp.ones(own.shape, dtype=bool)], axis=-1)
    return sweep_queries(functools.partial(moba_selected, k_blk=k_blk, v_blk=v_blk), MOBA_QBLOCK, q, pos[None], idx, valid)


def moba_layer(hp, hs, cache, layer, page_table, w_in, w_out, q_gain, k_gain):
    tp, ts = hp.shape[1], hs.shape[1]
    pos_p = jnp.arange(tp, dtype=jnp.int32)
    qp, kvp = moba_project(hp, pos_p, w_in, q_gain, k_gain)
    op = moba_attend(qp, pos_p, kvp[:, :, 0], kvp[:, :, 1])
    past_len = page_table.shape[1] * cache.shape[2]
    pos_s = past_len + jnp.arange(ts, dtype=jnp.int32)
    qs, kvs = moba_project(hs, pos_s, w_in, q_gain, k_gain)
    k_full = jnp.concatenate([past_rows(cache, layer, page_table, 0), kvs[:, :, 0]], axis=1)
    v_full = jnp.concatenate([past_rows(cache, layer, page_table, 1), kvs[:, :, 1]], axis=1)
    os_ = moba_attend(qs, pos_s, k_full, v_full)
    return merge_heads(op, w_out), merge_heads(os_, w_out), kvp, kvs


def sb_project(h, w_in):
    q, k, v = jnp.split(h @ w_in, 3, axis=-1)
    return split_heads(q, N_HEADS), jnp.stack([split_heads(k, N_HEADS), split_heads(v, N_HEADS)], axis=2)


def sb_block(q, pos, k, v):
    t = pos[0]
    L = k.shape[1]
    z = jnp.einsum('bthd,bshd->bhts', q, k).astype(jnp.float32) * SCALE
    mask = (jnp.arange(L, dtype=jnp.int32)[None, :] < t[:, None])[None, None]
    log_beta = jax.nn.log_sigmoid(z)
    log_keep = jnp.where(mask, jax.nn.log_sigmoid(-z), 0.0)
    later = lax.cumsum(log_keep, axis=3, reverse=True) - log_keep
    a = jnp.where(mask, jnp.exp(log_beta + later), 0.0)
    return jnp.einsum('bhts,bshd->bthd', a.astype(v.dtype), v)


def sb_layer(hp, hs, cache, layer, page_table, w_in, w_out):
    tp, ts = hp.shape[1], hs.shape[1]
    pos_p = jnp.arange(tp, dtype=jnp.int32)
    qp, kvp = sb_project(hp, w_in)
    op = sweep_queries(functools.partial(sb_block, k=kvp[:, :, 0], v=kvp[:, :, 1]), SB_QBLOCK, qp, pos_p[None])
    past_len = page_table.shape[1] * cache.shape[2]
    pos_s = past_len + jnp.arange(ts, dtype=jnp.int32)
    qs, kvs = sb_project(hs, w_in)
    k_full = jnp.concatenate([past_rows(cache, layer, page_table, 0), kvs[:, :, 0]], axis=1)
    v_full = jnp.concatenate([past_rows(cache, layer, page_table, 1), kvs[:, :, 1]], axis=1)
    os_ = sweep_queries(functools.partial(sb_block, k=k_full, v=v_full), SB_QBLOCK, qs, pos_s[None])
    return merge_heads(op, w_out), merge_heads(os_, w_out), kvp, kvs


def setup_inputs(seed: int = 0) -> dict:
    key = jax.random.key(seed)
    keys = iter(jax.random.split(key, 32))

    def normal(shape, scale=1.0):
        return jax.random.normal(next(keys), shape, jnp.float32) * scale

    def gain(shape):
        return 1.0 + 0.01 * normal(shape)

    n_pages = PAST_LEN // PAGE_SIZE
    n_used = DEC_BATCH * n_pages
    n_pool = n_used + (n_used + 3) // 4
    win = min(WINDOW, PAST_LEN)
    page_table = jax.random.permutation(next(keys), n_pool)[:n_used].reshape(DEC_BATCH, n_pages).astype(jnp.int32)
    return {
        'x_prompt': normal((BATCH, SEQ, D_MODEL)),
        'x_sample': normal((DEC_BATCH, DEC_SEQ, D_MODEL)),
        'cache_nsa_kv': normal((N_NSA_LAYERS, n_pool, PAGE_SIZE, 4, NSA_KV_HEADS, HEAD_DIM)),
        'state_nsa_win': normal((N_NSA_LAYERS, DEC_BATCH, win, 2, NSA_KV_HEADS, HEAD_DIM)),
        'cache_moba_kv': normal((N_MOBA_LAYERS, n_pool, PAGE_SIZE, 2, MOBA_KV_HEADS, HEAD_DIM)),
        'cache_sb_kv': normal((N_SB_LAYERS, n_pool, PAGE_SIZE, 2, N_HEADS, HEAD_DIM)),
        'page_table': page_table,
        'mix_norm': gain((DEPTH, D_MODEL)),
        'ffn_norm': gain((DEPTH, D_MODEL)),
        'nsa_w_in': normal((N_NSA_LAYERS, D_MODEL, NSA_IN_DIM), D_MODEL ** -0.5),
        'nsa_w_out': normal((N_NSA_LAYERS, ATTN_DIM, D_MODEL), ATTN_DIM ** -0.5),
        'nsa_q_gain': gain((N_NSA_LAYERS, HEAD_DIM)),
        'nsa_kc_gain': gain((N_NSA_LAYERS, HEAD_DIM)),
        'nsa_ks_gain': gain((N_NSA_LAYERS, HEAD_DIM)),
        'nsa_kw_gain': gain((N_NSA_LAYERS, HEAD_DIM)),
        'nsa_cmp_pos': normal((N_NSA_LAYERS, 2, CMP_BLOCK, HEAD_DIM), 0.1),
        'nsa_cmp_w1': normal((N_NSA_LAYERS, 2, CMP_BLOCK * HEAD_DIM, CMP_HIDDEN), (CMP_BLOCK * HEAD_DIM) ** -0.5),
        'nsa_cmp_w2': normal((N_NSA_LAYERS, 2, CMP_HIDDEN, HEAD_DIM), CMP_HIDDEN ** -0.5),
        'moba_w_in': normal((N_MOBA_LAYERS, D_MODEL, ATTN_DIM + 2 * MOBA_KV_DIM), D_MODEL ** -0.5),
        'moba_w_out': normal((N_MOBA_LAYERS, ATTN_DIM, D_MODEL), ATTN_DIM ** -0.5),
        'moba_q_gain': gain((N_MOBA_LAYERS, HEAD_DIM)),
        'moba_k_gain': gain((N_MOBA_LAYERS, HEAD_DIM)),
        'sb_w_in': normal((N_SB_LAYERS, D_MODEL, 3 * ATTN_DIM), D_MODEL ** -0.5),
        'sb_w_out': normal((N_SB_LAYERS, ATTN_DIM, D_MODEL), ATTN_DIM ** -0.5),
        'ffn_w_gate': normal((DEPTH, D_MODEL, D_FF), D_MODEL ** -0.5),
        'ffn_w_up': normal((DEPTH, D_MODEL, D_FF), D_MODEL ** -0.5),
        'ffn_w_down': normal((DEPTH, D_FF, D_MODEL), D_FF ** -0.5),
    }


def reference(x_prompt, x_sample, cache_nsa_kv, state_nsa_win, cache_moba_kv, cache_sb_kv, page_table,
              mix_norm, ffn_norm, nsa_w_in, nsa_w_out, nsa_q_gain, nsa_kc_gain, nsa_ks_gain, nsa_kw_gain,
              nsa_cmp_pos, nsa_cmp_w1, nsa_cmp_w2, moba_w_in, moba_w_out, moba_q_gain, moba_k_gain,
              sb_w_in, sb_w_out, ffn_w_gate, ffn_w_up, ffn_w_down):
    xp, xs = x_prompt, x_sample
    nsa_kv_p, nsa_kv_s, nsa_win_p, nsa_win_s = [], [], [], []
    moba_kv_p, moba_kv_s, sb_kv_p, sb_kv_s = [], [], [], []
    for i in range(DEPTH):
        j = i // N_MIXERS
        hp = rms_norm(xp, mix_norm[i])
        hs = rms_norm(xs, mix_norm[i])
        if i % N_MIXERS == 0:
            op, os_, kvp, kvs, wp, ws = nsa_layer(hp, hs, cache_nsa_kv, j, page_table, state_nsa_win[j],
                                                  nsa_w_in[j], nsa_w_out[j], nsa_q_gain[j], nsa_kc_gain[j],
                                                  nsa_ks_gain[j], nsa_kw_gain[j], nsa_cmp_pos[j],
                                                  nsa_cmp_w1[j], nsa_cmp_w2[j])
            nsa_kv_p.append(kvp)
            nsa_kv_s.append(kvs)
            nsa_win_p.append(wp)
            nsa_win_s.append(ws)
        elif i % N_MIXERS == 1:
            op, os_, kvp, kvs = moba_layer(hp, hs, cache_moba_kv, j, page_table, moba_w_in[j], moba_w_out[j],
                                           moba_q_gain[j], moba_k_gain[j])
            moba_kv_p.append(kvp)
            moba_kv_s.append(kvs)
        else:
            op, os_, kvp, kvs = sb_layer(hp, hs, cache_sb_kv, j, page_table, sb_w_in[j], sb_w_out[j])
            sb_kv_p.append(kvp)
            sb_kv_s.append(kvs)
        xp = xp + op
        xs = xs + os_
        xp = xp + swiglu(rms_norm(xp, ffn_norm[i]), ffn_w_gate[i], ffn_w_up[i], ffn_w_down[i])
        xs = xs + swiglu(rms_norm(xs, ffn_norm[i]), ffn_w_gate[i], ffn_w_up[i], ffn_w_down[i])
    return (xp, xs, jnp.stack(nsa_kv_p), jnp.stack(nsa_kv_s), jnp.stack(nsa_win_p), jnp.stack(nsa_win_s),
            jnp.stack(moba_kv_p), jnp.stack(moba_kv_s), jnp.stack(sb_kv_p), jnp.stack(sb_kv_s))
```

```python
import functools
import math

import jax
import jax.numpy as jnp
from jax import lax
from jax.experimental import pallas as pl
from jax.experimental.pallas import tpu as pltpu

N_HEADS = 16
HEAD_DIM = 128
KV_HEADS = 4
GROUP = N_HEADS // KV_HEADS
CMP_BLOCK = 64
N_SELECT = 16
WINDOW = 512
MOBA_BLOCK = 256
MOBA_TOPK = 3
N_MIXERS = 3
ROPE_THETA = 10000.0
RMS_EPS = 1e-6
NEG_INF = -1e30
REMOVED = -3e38
SCALE = HEAD_DIM ** -0.5

VMEM_LIMIT = 56 * 1024 * 1024
F32 = jnp.float32
BF16 = jnp.bfloat16


def _cparams(sem):
    return pltpu.CompilerParams(dimension_semantics=sem, vmem_limit_bytes=VMEM_LIMIT)


def _dot(a, b):
    return jnp.dot(a, b, preferred_element_type=F32)


def _dot_nt(a, b):
    return lax.dot_general(a, b, (((1,), (1,)), ((), ())), preferred_element_type=F32)


def _iota(shape, dim):
    return lax.broadcasted_iota(jnp.int32, shape, dim)


def _masked_softmax(s, mask):
    s = jnp.where(mask, s, NEG_INF)
    m = jnp.max(s, axis=-1, keepdims=True)
    e = jnp.exp(s - m)
    p = e * (1.0 / jnp.sum(e, axis=-1, keepdims=True))
    return jnp.where(mask, p, 0.0)


def _head_norm_rope(x, gain, cos, sin):
    y = x * lax.rsqrt(jnp.mean(x * x, axis=-1, keepdims=True) + RMS_EPS) * gain
    return y * cos + pltpu.roll(y, HEAD_DIM // 2, 1) * sin


def _rope_tables(pos):
    half = HEAD_DIM // 2
    inv_freq = ROPE_THETA ** (-jnp.arange(half, dtype=F32) / half)
    ang = pos.astype(F32)[:, None] * inv_freq[None, :]
    cos, sin = jnp.cos(ang), jnp.sin(ang)
    return jnp.concatenate([cos, cos], axis=-1), jnp.concatenate([-sin, sin], axis=-1)


def _rank_select(val, cand, k, axis):
    n = val.shape[axis]
    idx = _iota(val.shape, axis)
    v = jnp.where(cand, val, NEG_INF)
    rank = jnp.zeros(val.shape, jnp.int32)
    for j in range(n):
        vj = lax.slice_in_dim(v, j, j + 1, axis=axis)
        ahead = (vj > v) | ((vj == v) & (idx > j))
        rank = rank + jnp.where(ahead, 1, 0)
    return cand & (rank < k)


def _topk_indices(val, k):
    rows, n = val.shape
    lane = _iota((rows, n), 1).astype(F32)
    out_lane = _iota((rows, 128), 1)
    out = jnp.zeros((rows, 128), F32)
    for i in range(k):
        m = jnp.max(val, axis=-1, keepdims=True)
        pick = jnp.min(jnp.where(val == m, lane, float(n)), axis=-1, keepdims=True)
        out = jnp.where(out_lane == i, pick, out)
        val = jnp.where(lane == pick, REMOVED, val)
    return out.astype(jnp.int32)


def _proj_kernel(x_ref, g_ref, w_ref, hg_ref, cos_ref, sin_ref, *refs, tiles, n_out):
    out_refs, h_scr = refs[:n_out], refs[n_out]
    j = pl.program_id(1)

    @pl.when(j == 0)
    def _():
        x = x_ref[...]
        h = x * lax.rsqrt(jnp.mean(x * x, axis=-1, keepdims=True) + RMS_EPS) * g_ref[...]
        h_scr[...] = h.astype(BF16)

    acc = _dot(h_scr[...], w_ref[...])
    for jt, (mode, oi) in enumerate(tiles):
        @pl.when(j == jt)
        def _(mode=mode, oi=oi):
            o_ref = out_refs[oi]
            if mode == "rope":
                cos, sin, gain = cos_ref[...], sin_ref[...], hg_ref[0]
                for hh in range(acc.shape[1] // HEAD_DIM):
                    sl = slice(hh * HEAD_DIM, (hh + 1) * HEAD_DIM)
                    o_ref[:, sl] = _head_norm_rope(acc[:, sl], gain, cos, sin).astype(o_ref.dtype)
            elif mode == "sigmoid":
                o_ref[...] = (1.0 / (1.0 + jnp.exp(-acc))).astype(o_ref.dtype)
            else:
                o_ref[...] = acc.astype(o_ref.dtype)


def _norm_proj(x, gain, w, head_gains, cos, sin, tiles, outs, tm, tn=512):
    m, d = x.shape
    n_tiles = len(tiles)
    assert w.shape == (d, n_tiles * tn) and m % tm == 0
    starts = []
    for oi in range(len(outs)):
        starts.append(min(jt for jt, t in enumerate(tiles) if t[1] == oi))

    def out_map(oi):
        s, cnt = starts[oi], outs[oi][0]
        return lambda i, j: (i, jnp.clip(j - s, 0, cnt - 1))

    return pl.pallas_call(
        functools.partial(_proj_kernel, tiles=tuple(tiles), n_out=len(outs)),
        grid=(m // tm, n_tiles),
        in_specs=[
            pl.BlockSpec((tm, d), lambda i, j: (i, 0)),
            pl.BlockSpec((1, d), lambda i, j: (0, 0)),
            pl.BlockSpec((d, tn), lambda i, j: (0, j)),
            pl.BlockSpec((1, 1, HEAD_DIM), lambda i, j: (j, 0, 0)),
            pl.BlockSpec((tm, HEAD_DIM), lambda i, j: (i, 0)),
            pl.BlockSpec((tm, HEAD_DIM), lambda i, j: (i, 0)),
        ],
        out_specs=[pl.BlockSpec((tm, tn), out_map(oi)) for oi in range(len(outs))],
        out_shape=[jax.ShapeDtypeStruct((m, cnt * tn), dt) for cnt, dt in outs],
        scratch_shapes=[pltpu.VMEM((tm, d), BF16)],
        compiler_params=_cparams(("parallel", "arbitrary")),
        name="norm_proj",
    )(x, gain.reshape(1, d), w, head_gains, cos, sin)


def _out_proj_kernel(o_ref, w_ref, x_ref, y_ref):
    y_ref[...] = x_ref[...] + _dot(o_ref[...], w_ref[...])


def _out_proj(o, w, x, tm, tn=512):
    m, k = o.shape
    n = w.shape[1]
    return pl.pallas_call(
        _out_proj_kernel,
        grid=(m // tm, n // tn),
        in_specs=[
            pl.BlockSpec((tm, k), lambda i, j: (i, 0)),
            pl.BlockSpec((k, tn), lambda i, j: (0, j)),
            pl.BlockSpec((tm, tn), lambda i, j: (i, j)),
        ],
        out_specs=pl.BlockSpec((tm, tn), lambda i, j: (i, j)),
        out_shape=jax.ShapeDtypeStruct((m, n), F32),
        compiler_params=_cparams(("parallel", "arbitrary")),
        name="out_proj",
    )(o, w, x)


def _ffn_kernel(x_ref, g_ref, wg_ref, wu_ref, wd_ref, y_ref, h_scr, acc_scr):
    f = pl.program_id(1)

    @pl.when(f == 0)
    def _():
        x = x_ref[...]
        h = x * lax.rsqrt(jnp.mean(x * x, axis=-1, keepdims=True) + RMS_EPS) * g_ref[...]
        h_scr[...] = h.astype(BF16)
        acc_scr[...] = jnp.zeros_like(acc_scr)

    h = h_scr[...]
    a = _dot(h, wg_ref[...])
    u = _dot(h, wu_ref[...])
    act = (a * (1.0 / (1.0 + jnp.exp(-a))) * u).astype(BF16)
    acc_scr[...] += _dot(act, wd_ref[...])

    @pl.when(f == pl.num_programs(1) - 1)
    def _():
        y_ref[...] = x_ref[...] + acc_scr[...]


def _ffn(x, gain, wg, wu, wd, tm, tf=512):
    m, d = x.shape
    dff = wg.shape[1]
    assert dff % tf == 0 and m % tm == 0
    return pl.pallas_call(
        _ffn_kernel,
        grid=(m // tm, dff // tf),
        in_specs=[
            pl.BlockSpec((tm, d), lambda i, f: (i, 0)),
            pl.BlockSpec((1, d), lambda i, f: (0, 0)),
            pl.BlockSpec((d, tf), lambda i, f: (0, f)),
            pl.BlockSpec((d, tf), lambda i, f: (0, f)),
            pl.BlockSpec((tf, d), lambda i, f: (f, 0)),
        ],
        out_specs=pl.BlockSpec((tm, d), lambda i, f: (i, 0)),
        out_shape=jax.ShapeDtypeStruct((m, d), F32),
        scratch_shapes=[pltpu.VMEM((tm, d), BF16), pltpu.VMEM((tm, d), F32)],
        compiler_params=_cparams(("parallel", "arbitrary")),
        name="ffn",
    )(x, gain.reshape(1, d), wg, wu, wd)


def _gelu_tanh(x):
    return 0.5 * x * (1.0 + jnp.tanh(math.sqrt(2.0 / math.pi) * (x + 0.044715 * (x * x * x))))


def _compress_slab(x_refs, pos_ref, w1_ref, w2_ref, gain_ref, cos_ref, sin_ref, ok_ref, ov_ref, nblk):
    jc = 8

    for s, o_ref in ((0, ok_ref), (1, ov_ref)):
        def body(jj, acc, s=s):
            pieces = []
            for g in range(KV_HEADS):
                x_ref = x_refs[s * KV_HEADS + g]
                cols = []
                for jo in range(jc):
                    j = jj * jc + jo
                    xj = x_ref[pl.ds(j, nblk, stride=CMP_BLOCK), :]
                    cols.append((xj + pos_ref[s, pl.ds(j, 1), :]).astype(BF16))
                pieces.append(jnp.concatenate(cols, axis=1))
            lhs = jnp.concatenate(pieces, axis=0)
            k0 = pl.multiple_of(jj * (jc * HEAD_DIM), jc * HEAD_DIM)
            return acc + _dot(lhs, w1_ref[s, pl.ds(k0, jc * HEAD_DIM), :])

        hidden = lax.fori_loop(0, CMP_BLOCK // jc, body, jnp.zeros((KV_HEADS * nblk, w1_ref.shape[2]), F32))
        y = _dot(_gelu_tanh(hidden).astype(BF16), w2_ref[s])
        for g in range(KV_HEADS):
            yg = y[g * nblk:(g + 1) * nblk]
            if s == 0:
                yg = _head_norm_rope(yg, gain_ref[...], cos_ref[...], sin_ref[...])
            o_ref[g] = yg.astype(o_ref.dtype)


def _compress_prompt_kernel(*refs, nblk):
    n_col = 2 * KV_HEADS
    _compress_slab(refs[:n_col], *refs[n_col:], nblk)


def _compress_prompt(rows, b, t, pos_emb, w1, w2, kc_gain, cos_c, sin_c):
    nc = t // CMP_BLOCK
    n_col = 2 * KV_HEADS
    out = jax.ShapeDtypeStruct((b, KV_HEADS, nc, HEAD_DIM), BF16)
    return pl.pallas_call(
        functools.partial(_compress_prompt_kernel, nblk=nc),
        grid=(b,),
        in_specs=[pl.BlockSpec((t, HEAD_DIM), lambda i, c=c: (i, c)) for c in range(n_col)] + [
            pl.BlockSpec(pos_emb.shape, lambda i: (0, 0, 0)),
            pl.BlockSpec(w1.shape, lambda i: (0, 0, 0)),
            pl.BlockSpec(w2.shape, lambda i: (0, 0, 0)),
            pl.BlockSpec((1, HEAD_DIM), lambda i: (0, 0)),
            pl.BlockSpec((nc, HEAD_DIM), lambda i: (0, 0)),
            pl.BlockSpec((nc, HEAD_DIM), lambda i: (0, 0)),
        ],
        out_specs=[pl.BlockSpec((None, KV_HEADS, nc, HEAD_DIM), lambda i: (i, 0, 0, 0))] * 2,
        out_shape=[out, out],
        compiler_params=_cparams(("arbitrary",)),
        name="nsa_compress_prompt",
    )(*([rows] * n_col), pos_emb, w1, w2, kc_gain.reshape(1, HEAD_DIM), cos_c, sin_c)


def _compress_sample_kernel(pt_ref, cache_ref, pos_ref, w1_ref, w2_ref, gain_ref, cos_ref, sin_ref,
                            ok_ref, ov_ref, slab, sems, *, layer, pages):
    b, c = pl.program_id(0), pl.program_id(1)
    n_col = slab.shape[0]
    psz = slab.shape[1] // pages

    def tile_copy(p, col):
        page = pt_ref[b, c * pages + p]
        return pltpu.make_async_copy(cache_ref.at[layer, page, :, pl.ds(col * HEAD_DIM, HEAD_DIM)],
                                     slab.at[col, pl.ds(p * psz, psz), :], sems.at[p, col])

    for p in range(pages):
        for col in range(n_col):
            tile_copy(p, col).start()
    for p in range(pages):
        for col in range(n_col):
            tile_copy(p, col).wait()
    _compress_slab([slab.at[col] for col in range(n_col)], pos_ref, w1_ref, w2_ref, gain_ref, cos_ref, sin_ref,
                   ok_ref, ov_ref, slab.shape[1] // CMP_BLOCK)


def _compress_sample(cache, layer, page_table, pos_emb, w1, w2, kc_gain, cos_c, sin_c, pages=16):
    b, n_pages = page_table.shape
    psz = cache.shape[2]
    pages = min(pages, n_pages)
    assert n_pages % pages == 0 and psz % CMP_BLOCK == 0
    nblk = pages * psz // CMP_BLOCK
    nc = n_pages * psz // CMP_BLOCK
    n_col = 2 * KV_HEADS
    out = jax.ShapeDtypeStruct((b, KV_HEADS, nc, HEAD_DIM), BF16)
    grid_spec = pltpu.PrefetchScalarGridSpec(
        num_scalar_prefetch=1,
        grid=(b, n_pages // pages),
        in_specs=[
            pl.BlockSpec(memory_space=pl.ANY),
            pl.BlockSpec(pos_emb.shape, lambda i, c, pt: (0, 0, 0)),
            pl.BlockSpec(w1.shape, lambda i, c, pt: (0, 0, 0)),
            pl.BlockSpec(w2.shape, lambda i, c, pt: (0, 0, 0)),
            pl.BlockSpec((1, HEAD_DIM), lambda i, c, pt: (0, 0)),
            pl.BlockSpec((nblk, HEAD_DIM), lambda i, c, pt: (c, 0)),
            pl.BlockSpec((nblk, HEAD_DIM), lambda i, c, pt: (c, 0)),
        ],
        out_specs=[pl.BlockSpec((None, KV_HEADS, nblk, HEAD_DIM), lambda i, c, pt: (i, 0, c, 0))] * 2,
        scratch_shapes=[pltpu.VMEM((n_col, pages * psz, HEAD_DIM), F32), pltpu.SemaphoreType.DMA((pages, n_col))],
    )
    return pl.pallas_call(
        functools.partial(_compress_sample_kernel, layer=layer, pages=pages),
        grid_spec=grid_spec,
        out_shape=[out, out],
        compiler_params=_cparams(("arbitrary", "arbitrary")),
        name="nsa_compress_sample",
    )(page_table, cache, pos_emb, w1, w2, kc_gain.reshape(1, HEAD_DIM), cos_c, sin_c)


def _nsa_prompt_kernel(q_ref, ks_ref, vs_ref, kw_ref, vw_ref, kc_ref, vc_ref, gt_ref, o_ref,
                       ks_s, vs_s, kw_s, vw_s, *, t_len, tq):
    qi = pl.program_id(2)

    @pl.when(qi == 0)
    def _():
        ks_s[...] = ks_ref[...].astype(BF16)
        vs_s[...] = vs_ref[...].astype(BF16)
        kw_s[...] = kw_ref[...].astype(BF16)
        vw_s[...] = vw_ref[...].astype(BF16)

    nc = kc_ref.shape[0]
    t0 = qi * tq
    tpos = t0 + _iota((tq, 1), 0)
    qb = q_ref[...]
    gt = gt_ref[...]
    kc, vc = kc_ref[...], vc_ref[...]

    cblk = _iota((tq, nc), 1)
    cmask = (cblk * CMP_BLOCK + (CMP_BLOCK - 1)) <= tpos
    imp = jnp.zeros((tq, nc), F32)
    o_cmp = []
    for r in range(GROUP):
        q_r = qb[:, r * HEAD_DIM:(r + 1) * HEAD_DIM]
        p = _masked_softmax(_dot_nt(q_r, kc) * SCALE, cmask)
        imp = imp + p
        o_cmp.append(_dot(p.astype(BF16), vc))

    cur = tpos >> 6
    sel = _rank_select(imp, cblk < cur, N_SELECT - 1, axis=1) | (cblk == cur)
    expand = jnp.where((_iota((nc, t_len), 1) >> 6) == _iota((nc, t_len), 0), 1.0, 0.0).astype(BF16)
    kpos = _iota((tq, t_len), 1)
    smask = (_dot(jnp.where(sel, 1.0, 0.0).astype(BF16), expand) > 0.5) & (kpos <= tpos)

    wn = min(WINDOW + tq, t_len)
    w0 = pl.multiple_of(jnp.maximum(t0 + tq - wn, 0), tq)
    wpos = w0 + _iota((tq, wn), 1)
    wd = tpos - wpos
    wmask = (wd >= 0) & (wd < WINDOW)
    kw, vw = kw_s[pl.ds(w0, wn), :], vw_s[pl.ds(w0, wn), :]
    ks, vs = ks_s[...], vs_s[...]

    for r in range(GROUP):
        q_r = qb[:, r * HEAD_DIM:(r + 1) * HEAD_DIM]
        p = _masked_softmax(_dot_nt(q_r, ks) * SCALE, smask)
        o_slc = _dot(p.astype(BF16), vs)
        p = _masked_softmax(_dot_nt(q_r, kw) * SCALE, wmask)
        o_win = _dot(p.astype(BF16), vw)
        o = (gt[:, 3 * r:3 * r + 1] * o_cmp[r] + gt[:, 3 * r + 1:3 * r + 2] * o_slc
             + gt[:, 3 * r + 2:3 * r + 3] * o_win)
        o_ref[:, r * HEAD_DIM:(r + 1) * HEAD_DIM] = o.astype(o_ref.dtype)


def _nsa_prompt_attn(q, rows, win, k_cmp, v_cmp, gates, b, t, tq=128):
    nq = t // tq
    nc = t // CMP_BLOCK
    gw = GROUP * HEAD_DIM
    kv_spec = lambda col0: pl.BlockSpec((t, HEAD_DIM), lambda i, g, qi: (i, col0 + g))
    cmp_spec = pl.BlockSpec((None, None, nc, HEAD_DIM), lambda i, g, qi: (i, g, 0, 0))
    return pl.pallas_call(
        functools.partial(_nsa_prompt_kernel, t_len=t, tq=tq),
        grid=(b, KV_HEADS, nq),
        in_specs=[
            pl.BlockSpec((tq, gw), lambda i, g, qi: (i * nq + qi, g)),
            kv_spec(2 * KV_HEADS), kv_spec(3 * KV_HEADS), kv_spec(0), kv_spec(KV_HEADS),
            cmp_spec, cmp_spec,
            pl.BlockSpec((tq, HEAD_DIM), lambda i, g, qi: (i * nq + qi, g)),
        ],
        out_specs=pl.BlockSpec((tq, gw), lambda i, g, qi: (i * nq + qi, g)),
        out_shape=jax.ShapeDtypeStruct((b * t, N_HEADS * HEAD_DIM), BF16),
        scratch_shapes=[pltpu.VMEM((t, HEAD_DIM), BF16)] * 4,
        compiler_params=_cparams(("parallel", "parallel", "arbitrary")),
        name="nsa_prompt_attn",
    )(q, rows, rows, win, win, k_cmp, v_cmp, gates)


def _nsa_sample_cmp_kernel(q_ref, kc_ref, vc_ref, o_ref, idx_ref, *, pos):
    nc = kc_ref.shape[0]
    cblk = _iota((GROUP, nc), 1)
    cmask = (cblk * CMP_BLOCK + (CMP_BLOCK - 1)) <= pos
    p = _masked_softmax(_dot_nt(q_ref[...], kc_ref[...]) * SCALE, cmask)
    o_ref[...] = _dot(p.astype(BF16), vc_ref[...])
    imp = jnp.sum(p, axis=0, keepdims=True)
    cand = _iota((1, nc), 1) < (pos // CMP_BLOCK)
    idx_ref[...] = _topk_indices(jnp.where(cand, imp, NEG_INF), N_SELECT - 1)


def _nsa_sample_cmp(q4, k_cmp, v_cmp, pos):
    b = q4.shape[0]
    nc = k_cmp.shape[2]
    assert nc >= N_SELECT - 1 and pos // CMP_BLOCK >= N_SELECT - 1
    spec4 = pl.BlockSpec((None, None, GROUP, HEAD_DIM), lambda i, g: (i, g, 0, 0))
    cspec = pl.BlockSpec((None, None, nc, HEAD_DIM), lambda i, g: (i, g, 0, 0))
    return pl.pallas_call(
        functools.partial(_nsa_sample_cmp_kernel, pos=pos),
        grid=(b, KV_HEADS),
        in_specs=[spec4, cspec, cspec],
        out_specs=[spec4, pl.BlockSpec((None, None, 1, 128), lambda i, g: (i, g, 0, 0))],
        out_shape=[jax.ShapeDtypeStruct((b, KV_HEADS, GROUP, HEAD_DIM), F32),
                   jax.ShapeDtypeStruct((b, KV_HEADS, 1, 128), jnp.int32)],
        compiler_params=_cparams(("parallel", "parallel")),
        name="nsa_sample_cmp",
    )(q4, k_cmp, v_cmp)


def _nsa_sample_slc_kernel(pt_ref, idx_ref, q_ref, kblk_ref, vblk_ref, knew_ref, vnew_ref,
                           kwin_ref, vwin_ref, kwnew_ref, vwnew_ref, ocmp_ref, gt_ref, o_ref,
                           m_scr, l_scr, acc_scr, *, n_sel):
    n = pl.program_id(2)
    q = q_ref[...]

    @pl.when(n == 0)
    def _():
        m_scr[...] = jnp.sum(q.astype(F32) * knew_ref[...].astype(BF16).astype(F32), axis=-1, keepdims=True) * SCALE
        l_scr[...] = jnp.ones_like(l_scr)
        acc_scr[...] = jnp.broadcast_to(vnew_ref[...].astype(BF16).astype(F32), acc_scr.shape)

    s = _dot_nt(q, kblk_ref[...].astype(BF16)) * SCALE
    m_old = m_scr[...]
    m_new = jnp.maximum(m_old, jnp.max(s, axis=-1, keepdims=True))
    alpha = jnp.exp(m_old - m_new)
    e = jnp.exp(s - m_new)
    l_scr[...] = alpha * l_scr[...] + jnp.sum(e, axis=-1, keepdims=True)
    acc_scr[...] = alpha * acc_scr[...] + _dot(e.astype(BF16), vblk_ref[...].astype(BF16))
    m_scr[...] = m_new

    @pl.when(n == n_sel - 1)
    def _():
        o_slc = acc_scr[...] * (1.0 / l_scr[...])
        nw = kwin_ref.shape[0]
        sw = _dot_nt(q, kwin_ref[...].astype(BF16)) * SCALE
        sw = jnp.where(nw - _iota((GROUP, nw), 1) < WINDOW, sw, NEG_INF)
        s_new = jnp.sum(q.astype(F32) * kwnew_ref[...].astype(BF16).astype(F32), axis=-1, keepdims=True) * SCALE
        mw = jnp.maximum(jnp.max(sw, axis=-1, keepdims=True), s_new)
        ew = jnp.exp(sw - mw)
        e_new = jnp.exp(s_new - mw)
        lw = jnp.sum(ew, axis=-1, keepdims=True) + e_new
        pw = ew * (1.0 / lw)
        p_new = (e_new * (1.0 / lw)).astype(BF16).astype(F32)
        o_win = _dot(pw.astype(BF16), vwin_ref[...].astype(BF16)) + p_new * vwnew_ref[...].astype(BF16).astype(F32)
        gt = gt_ref[...]
        rowi = _iota((GROUP, 1), 0)
        gcol = [sum(jnp.where(rowi == r, gt[:, 3 * r + c:3 * r + c + 1], 0.0) for r in range(GROUP)) for c in range(3)]
        o_ref[...] = gcol[0] * ocmp_ref[...] + gcol[1] * o_slc + gcol[2] * o_win


def _nsa_sample_slc(cache, layer, page_table, idx, q4, rows_s, win_buf, win_s, o_cmp, gates_s, halves):
    b = q4.shape[0]
    n_sel = idx.shape[1] // KV_HEADS
    nw = win_buf.shape[1]

    def blk_map(col0):
        def f(i, g, n, pt, ix):
            blk = ix[i, g * n_sel + n]
            return (layer, pt[i, blk // halves] * halves + blk % halves, 0, col0 + g)
        return f

    spec4 = pl.BlockSpec((None, None, GROUP, HEAD_DIM), lambda i, g, n, pt, ix: (i, g, 0, 0))
    new_spec = lambda col0: pl.BlockSpec((None, None, 1, HEAD_DIM), lambda i, g, n, pt, ix: (i, col0 + g, 0, 0))
    win_spec = lambda col0: pl.BlockSpec((None, nw, HEAD_DIM), lambda i, g, n, pt, ix: (i, 0, col0 + g))
    grid_spec = pltpu.PrefetchScalarGridSpec(
        num_scalar_prefetch=2,
        grid=(b, KV_HEADS, n_sel),
        in_specs=[
            spec4,
            pl.BlockSpec((None, None, CMP_BLOCK, HEAD_DIM), blk_map(2 * KV_HEADS)),
            pl.BlockSpec((None, None, CMP_BLOCK, HEAD_DIM), blk_map(3 * KV_HEADS)),
            new_spec(2 * KV_HEADS), new_spec(3 * KV_HEADS),
            win_spec(0), win_spec(KV_HEADS),
            new_spec(0), new_spec(KV_HEADS),
            spec4,
            pl.BlockSpec((None, None, 1, HEAD_DIM), lambda i, g, n, pt, ix: (i, g, 0, 0)),
        ],
        out_specs=spec4,
        scratch_shapes=[pltpu.VMEM((GROUP, 1), F32), pltpu.VMEM((GROUP, 1), F32), pltpu.VMEM((GROUP, HEAD_DIM), F32)],
    )
    return pl.pallas_call(
        functools.partial(_nsa_sample_slc_kernel, n_sel=n_sel),
        grid_spec=grid_spec,
        out_shape=jax.ShapeDtypeStruct((b, KV_HEADS, GROUP, HEAD_DIM), F32),
        compiler_params=_cparams(("parallel", "parallel", "arbitrary")),
        name="nsa_sample_slc",
    )(page_table, idx, q4, cache, cache, rows_s, rows_s, win_buf, win_buf, win_s, win_s, o_cmp, gates_s)


def _moba_prompt_kernel(q_ref, k_ref, v_ref, o_ref, k_s, v_s, km_s, *, t_len, tq):
    qi = pl.program_id(2)
    nb = km_s.shape[0]

    @pl.when(qi == 0)
    def _():
        k = k_ref[...]
        k_s[...] = k.astype(BF16)
        v_s[...] = v_ref[...].astype(BF16)
        km_s[...] = (jnp.sum(k.reshape(nb, MOBA_BLOCK, HEAD_DIM), axis=1) * (1.0 / MOBA_BLOCK)).astype(BF16)

    t0 = qi * tq
    tpos = t0 + _iota((tq, 1), 0)
    cur = tpos >> 8
    blk = _iota((tq, nb), 1)
    cand = blk < cur
    expand = jnp.where((_iota((nb, t_len), 1) >> 8) == _iota((nb, t_len), 0), 1.0, 0.0).astype(BF16)
    causal = _iota((tq, t_len), 1) <= tpos
    qb = q_ref[...]
    k, v, km = k_s[...], v_s[...], km_s[...]
    for r in range(GROUP):
        q_r = qb[:, r * HEAD_DIM:(r + 1) * HEAD_DIM]
        gate = _dot_nt(q_r, km)
        sel = _rank_select(gate, cand, MOBA_TOPK, axis=1) | (blk == cur)
        mask = (_dot(jnp.where(sel, 1.0, 0.0).astype(BF16), expand) > 0.5) & causal
        p = _masked_softmax(_dot_nt(q_r, k) * SCALE, mask)
        o_ref[:, r * HEAD_DIM:(r + 1) * HEAD_DIM] = _dot(p.astype(BF16), v).astype(o_ref.dtype)


def _moba_prompt_attn(q, kv, b, t, tq=128):
    assert t % MOBA_BLOCK == 0
    nq = t // tq
    gw = GROUP * HEAD_DIM
    kv_spec = lambda col0: pl.BlockSpec((t, HEAD_DIM), lambda i, g, qi: (i, col0 + g))
    return pl.pallas_call(
        functools.partial(_moba_prompt_kernel, t_len=t, tq=tq),
        grid=(b, KV_HEADS, nq),
        in_specs=[pl.BlockSpec((tq, gw), lambda i, g, qi: (i * nq + qi, g)), kv_spec(0), kv_spec(KV_HEADS)],
        out_specs=pl.BlockSpec((tq, gw), lambda i, g, qi: (i * nq + qi, g)),
        out_shape=jax.ShapeDtypeStruct((b * t, N_HEADS * HEAD_DIM), BF16),
        scratch_shapes=[pltpu.VMEM((t, HEAD_DIM), BF16), pltpu.VMEM((t, HEAD_DIM), BF16),
                        pltpu.VMEM((t // MOBA_BLOCK, HEAD_DIM), BF16)],
        compiler_params=_cparams(("parallel", "parallel", "arbitrary")),
        name="moba_prompt_attn",
    )(q, kv, kv)


def _moba_sample_gate_kernel(pt_ref, q_ref, k_ref, idx_ref, ksum, *, pages_per_block):
    p = pl.program_id(1)

    @pl.when(p == 0)
    def _():
        ksum[...] = jnp.zeros_like(ksum)

    row = p // pages_per_block
    ksum[pl.ds(row, 1), :] += jnp.sum(k_ref[...], axis=0, keepdims=True)

    @pl.when(p == pl.num_programs(1) - 1)
    def _():
        km = (ksum[...] * (1.0 / MOBA_BLOCK)).astype(BF16)
        for g in range(KV_HEADS):
            gate = _dot_nt(q_ref[g], km[:, g * HEAD_DIM:(g + 1) * HEAD_DIM])
            idx_ref[g] = _topk_indices(gate, MOBA_TOPK)


def _moba_sample_gate(cache, layer, page_table, q4):
    b, n_pages = page_table.shape
    psz = cache.shape[2]
    ppb = MOBA_BLOCK // psz
    nb = n_pages // ppb
    assert MOBA_BLOCK % psz == 0 and n_pages % ppb == 0 and nb >= MOBA_TOPK
    kw = KV_HEADS * HEAD_DIM
    grid_spec = pltpu.PrefetchScalarGridSpec(
        num_scalar_prefetch=1,
        grid=(b, n_pages),
        in_specs=[
            pl.BlockSpec((None, KV_HEADS, GROUP, HEAD_DIM), lambda i, p, pt: (i, 0, 0, 0)),
            pl.BlockSpec((None, None, psz, kw), lambda i, p, pt: (layer, pt[i, p], 0, 0)),
        ],
        out_specs=pl.BlockSpec((None, KV_HEADS, GROUP, 128), lambda i, p, pt: (i, 0, 0, 0)),
        scratch_shapes=[pltpu.VMEM((nb, kw), F32)],
    )
    return pl.pallas_call(
        functools.partial(_moba_sample_gate_kernel, pages_per_block=ppb),
        grid_spec=grid_spec,
        out_shape=jax.ShapeDtypeStruct((b, KV_HEADS, GROUP, 128), jnp.int32),
        compiler_params=_cparams(("parallel", "arbitrary")),
        name="moba_sample_gate",
    )(page_table, q4, cache)


def _moba_sample_attn_kernel(pt_ref, idx_ref, q_ref, k_ref, v_ref, knew_ref, vnew_ref, o_ref,
                             m_scr, l_scr, acc_scr):
    n = pl.program_id(2)
    q = q_ref[...]

    @pl.when(n == 0)
    def _():
        m_scr[...] = jnp.sum(q.astype(F32) * knew_ref[...].astype(BF16).astype(F32), axis=-1, keepdims=True) * SCALE
        l_scr[...] = jnp.ones_like(l_scr)
        acc_scr[...] = vnew_ref[...].astype(BF16).astype(F32)

    s = _dot_nt(q, k_ref[...].astype(BF16)) * SCALE
    m_old = m_scr[...]
    m_new = jnp.maximum(m_old, jnp.max(s, axis=-1, keepdims=True))
    alpha = jnp.exp(m_old - m_new)
    e = jnp.exp(s - m_new)
    l_scr[...] = alpha * l_scr[...] + jnp.sum(e, axis=-1, keepdims=True)
    acc_scr[...] = alpha * acc_scr[...] + _dot(e.astype(BF16), v_ref[...].astype(BF16))
    m_scr[...] = m_new

    @pl.when(n == pl.num_programs(2) - 1)
    def _():
        o_ref[...] = (acc_scr[...] * (1.0 / l_scr[...])).astype(o_ref.dtype)


def _moba_sample_attn(cache, layer, page_table, idx, q16, kv_new):
    b = q16.shape[0]
    psz = cache.shape[2]
    ppb = MOBA_BLOCK // psz
    n_sel = idx.shape[1] // N_HEADS
    n_steps = n_sel * ppb

    def page_map(col0):
        def f(i, h, n, pt, ix):
            return (layer, pt[i, ix[i, h * n_sel + n // ppb] * ppb + n % ppb], 0, col0 + h // GROUP)
        return f

    hspec = pl.BlockSpec((None, None, 1, HEAD_DIM), lambda i, h, n, pt, ix: (i, h, 0, 0))
    new_spec = lambda col0: pl.BlockSpec((None, None, 1, HEAD_DIM), lambda i, h, n, pt, ix: (i, col0 + h // GROUP, 0, 0))
    grid_spec = pltpu.PrefetchScalarGridSpec(
        num_scalar_prefetch=2,
        grid=(b, N_HEADS, n_steps),
        in_specs=[
            hspec,
            pl.BlockSpec((None, None, psz, HEAD_DIM), page_map(0)),
            pl.BlockSpec((None, None, psz, HEAD_DIM), page_map(KV_HEADS)),
            new_spec(0), new_spec(KV_HEADS),
        ],
        out_specs=hspec,
        scratch_shapes=[pltpu.VMEM((1, 1), F32), pltpu.VMEM((1, 1), F32), pltpu.VMEM((1, HEAD_DIM), F32)],
    )
    return pl.pallas_call(
        _moba_sample_attn_kernel,
        grid_spec=grid_spec,
        out_shape=jax.ShapeDtypeStruct((b, N_HEADS, 1, HEAD_DIM), BF16),
        compiler_params=_cparams(("parallel", "parallel", "arbitrary")),
        name="moba_sample_attn",
    )(page_table, idx, q16, cache, cache, kv_new, kv_new)


def _log_sigmoid(z):
    return jnp.minimum(z, 0.0) - jnp.log1p(jnp.exp(-jnp.abs(z)))


def _split3(x):
    a = x.astype(BF16)
    r = x - a.astype(F32)
    b = r.astype(BF16)
    c = (r - b.astype(F32)).astype(BF16)
    return a, b, c


def _sb_prompt_kernel(q_ref, k_ref, v_ref, o_ref, k_s, v_s, *, tq):
    qi = pl.program_id(2)

    @pl.when(qi == 0)
    def _():
        k_s[...] = k_ref[...].astype(BF16)
        v_s[...] = v_ref[...].astype(BF16)

    q = q_ref[...].astype(BF16)
    tpos = qi * tq + _iota((tq, 1), 0)
    after = jnp.where(_iota((tq, tq), 0) > _iota((tq, tq), 1), 1.0, 0.0).astype(BF16)

    def body(it, carry):
        run, acc = carry
        c = qi - it
        s0 = pl.multiple_of(c * tq, tq)
        mask = (s0 + _iota((tq, tq), 1)) < tpos
        z = _dot_nt(q, k_s[pl.ds(s0, tq), :]) * SCALE
        log_beta = _log_sigmoid(z)
        log_keep = jnp.where(mask, log_beta - z, 0.0)
        a3 = _split3(log_keep)
        within = _dot(a3[0], after) + _dot(a3[1], after) + _dot(a3[2], after)
        a = jnp.where(mask, jnp.exp(log_beta + within + run), 0.0)
        acc = acc + _dot(a.astype(BF16), v_s[pl.ds(s0, tq), :])
        run = run + within[:, 0:1] + log_keep[:, 0:1]
        return run, acc

    _, acc = lax.fori_loop(0, qi + 1, body, (jnp.zeros((tq, 1), F32), jnp.zeros((tq, HEAD_DIM), F32)))
    o_ref[...] = acc.astype(o_ref.dtype)


def _sb_prompt_attn(qkv, b, t, tq=128):
    nq = t // tq
    return pl.pallas_call(
        functools.partial(_sb_prompt_kernel, tq=tq),
        grid=(b, N_HEADS, nq),
        in_specs=[
            pl.BlockSpec((tq, HEAD_DIM), lambda i, h, qi: (i * nq + qi, h)),
            pl.BlockSpec((t, HEAD_DIM), lambda i, h, qi: (i, N_HEADS + h)),
            pl.BlockSpec((t, HEAD_DIM), lambda i, h, qi: (i, 2 * N_HEADS + h)),
        ],
        out_specs=pl.BlockSpec((tq, HEAD_DIM), lambda i, h, qi: (i * nq + qi, h)),
        out_shape=jax.ShapeDtypeStruct((b * t, N_HEADS * HEAD_DIM), BF16),
        scratch_shapes=[pltpu.VMEM((t, HEAD_DIM), BF16)] * 2,
        compiler_params=_cparams(("parallel", "parallel", "arbitrary")),
        name="sb_prompt_attn",
    )(qkv, qkv, qkv)


def _sb_sample_kernel(pt_ref, qbd_ref, kv_ref, o_ref, run_scr, acc_scr):
    p = pl.program_id(1)
    psz = kv_ref.shape[0]
    aw = N_HEADS * HEAD_DIM

    @pl.when(p == 0)
    def _():
        run_scr[...] = jnp.zeros_like(run_scr)
        acc_scr[...] = jnp.zeros_like(acc_scr)

    z = _dot(kv_ref[:, 0:aw].astype(BF16), qbd_ref[...]) * SCALE
    log_beta = _log_sigmoid(z)
    log_keep = log_beta - z
    after = jnp.where(_iota((psz, psz), 1) > _iota((psz, psz), 0), 1.0, 0.0).astype(BF16)
    a3 = _split3(log_keep)
    within = _dot(after, a3[0]) + _dot(after, a3[1]) + _dot(after, a3[2])
    a = jnp.exp(log_beta + within + run_scr[...])
    acc_scr[...] += _dot(a.T.astype(BF16), kv_ref[:, aw:2 * aw].astype(BF16))
    run_scr[...] += within[0:1, :] + log_keep[0:1, :]

    @pl.when(p == pl.num_programs(1) - 1)
    def _():
        acc = acc_scr[...]
        own = _iota(acc.shape, 0) == (_iota(acc.shape, 1) >> 7)
        o_ref[...] = jnp.sum(jnp.where(own, acc, 0.0), axis=0, keepdims=True)


def _sb_sample_attn(cache, layer, page_table, qbd):
    b, n_pages = page_table.shape
    psz, kvw = cache.shape[2], cache.shape[3]
    aw = N_HEADS * HEAD_DIM
    grid_spec = pltpu.PrefetchScalarGridSpec(
        num_scalar_prefetch=1,
        grid=(b, n_pages),
        in_specs=[
            pl.BlockSpec((None, aw, 128), lambda i, p, pt: (i, 0, 0)),
            pl.BlockSpec((None, None, psz, kvw), lambda i, p, pt: (layer, pt[i, n_pages - 1 - p], 0, 0)),
        ],
        out_specs=pl.BlockSpec((None, 1, aw), lambda i, p, pt: (i, 0, 0)),
        scratch_shapes=[pltpu.VMEM((1, 128), F32), pltpu.VMEM((128, aw), F32)],
    )
    return pl.pallas_call(
        _sb_sample_kernel,
        grid_spec=grid_spec,
        out_shape=jax.ShapeDtypeStruct((b, 1, aw), F32),
        compiler_params=_cparams(("parallel", "arbitrary")),
        name="sb_sample_attn",
    )(page_table, qbd, cache)


def _tile_rows(m):
    return 512 if m % 512 == 0 else m


def _nsa_weights(w_in, q_gain, ks_gain, kw_gain):
    d = w_in.shape[0]
    main = N_HEADS * HEAD_DIM + 6 * KV_HEADS * HEAD_DIM
    wg = w_in[:, main:].reshape(d, KV_HEADS, GROUP * 3)
    wg = jnp.pad(wg, ((0, 0), (0, 0), (0, HEAD_DIM - GROUP * 3))).reshape(d, KV_HEADS * HEAD_DIM)
    w = jnp.concatenate([w_in[:, :main], wg], axis=1).astype(BF16)
    tiles = [("rope", 0)] * 4 + [("plain", 1), ("plain", 1), ("rope", 1), ("plain", 1),
                                 ("rope", 2), ("plain", 2), ("sigmoid", 3)]
    ones = jnp.ones((HEAD_DIM,), F32)
    hg = jnp.stack([q_gain] * 4 + [ones, ones, ks_gain, ones, kw_gain, ones, ones])[:, None, :]
    outs = [(4, BF16), (4, F32), (2, F32), (1, F32)]
    return w, tiles, hg, outs


def _nsa_layer(xp, xs, b, t, cache, layer, page_table, win_buf, norm_gain, w_in, w_out, q_gain, kc_gain,
               ks_gain, kw_gain, cmp_pos, cmp_w1, cmp_w2):
    bs = xs.shape[0]
    n_pages, psz = page_table.shape[1], cache.shape[2]
    past = n_pages * psz
    w, tiles, hg, outs = _nsa_weights(w_in, q_gain, ks_gain, kw_gain)
    w1, w2 = cmp_w1.astype(BF16), cmp_w2.astype(BF16)
    w_out = w_out.astype(BF16)

    cos_p, sin_p = _rope_tables(jnp.arange(t, dtype=jnp.int32))
    cos_pt, sin_pt = jnp.tile(cos_p, (b, 1)), jnp.tile(sin_p, (b, 1))
    q, rows, win, gates = _norm_proj(xp, norm_gain, w, hg, cos_pt, sin_pt, tiles, outs, _tile_rows(b * t))
    nc = t // CMP_BLOCK
    cos_c, sin_c = _rope_tables(jnp.arange(nc, dtype=jnp.int32) * CMP_BLOCK + (CMP_BLOCK - 1))
    k_cmp, v_cmp = _compress_prompt(rows, b, t, cmp_pos, w1, w2, kc_gain, cos_c, sin_c)
    o = _nsa_prompt_attn(q, rows, win, k_cmp, v_cmp, gates, b, t)
    xp = _out_proj(o, w_out, xp, _tile_rows(b * t))

    cos_s, sin_s = _rope_tables(jnp.full((bs,), past, jnp.int32))
    qs, rows_s, win_s, gates_s = _norm_proj(xs, norm_gain, w, hg, cos_s, sin_s, tiles, outs, bs)
    ncs = past // CMP_BLOCK
    cos_cs, sin_cs = _rope_tables(jnp.arange(ncs, dtype=jnp.int32) * CMP_BLOCK + (CMP_BLOCK - 1))
    cache2 = cache.reshape(cache.shape[0], cache.shape[1], psz, -1)
    k_cmp_s, v_cmp_s = _compress_sample(cache2, layer, page_table, cmp_pos, w1, w2, kc_gain, cos_cs, sin_cs)
    q4 = qs.reshape(bs, KV_HEADS, GROUP, HEAD_DIM)
    o_cmp, idx = _nsa_sample_cmp(q4, k_cmp_s, v_cmp_s, past)
    idx = idx[:, :, 0, :N_SELECT - 1].reshape(bs, KV_HEADS * (N_SELECT - 1))
    halves = psz // CMP_BLOCK
    half_view = cache.reshape(cache.shape[0], cache.shape[1] * halves, CMP_BLOCK, -1)
    o_s = _nsa_sample_slc(half_view, layer, page_table, idx, q4,
                          rows_s.reshape(bs, 4 * KV_HEADS, 1, HEAD_DIM),
                          win_buf.reshape(bs, win_buf.shape[1], 2 * KV_HEADS * HEAD_DIM),
                          win_s.reshape(bs, 2 * KV_HEADS, 1, HEAD_DIM), o_cmp,
                          gates_s.reshape(bs, KV_HEADS, 1, HEAD_DIM), halves)
    xs = _out_proj(o_s.reshape(bs, N_HEADS * HEAD_DIM).astype(BF16), w_out, xs, bs)

    kv_p = rows.reshape(b, t, 4, KV_HEADS, HEAD_DIM)
    kv_s = rows_s.reshape(bs, 1, 4, KV_HEADS, HEAD_DIM)
    win_p = win.reshape(b, t, 2, KV_HEADS, HEAD_DIM)[:, -min(WINDOW, t):]
    win_new = jnp.concatenate([win_buf, win_s.reshape(bs, 1, 2, KV_HEADS, HEAD_DIM)], axis=1)[:, -win_buf.shape[1]:]
    return xp, xs, kv_p, kv_s, win_p, win_new


def _moba_layer(xp, xs, b, t, cache, layer, page_table, norm_gain, w_in, w_out, q_gain, k_gain):
    bs = xs.shape[0]
    n_pages, psz = page_table.shape[1], cache.shape[2]
    past = n_pages * psz
    w = w_in.astype(BF16)
    w_out = w_out.astype(BF16)
    tiles = [("rope", 0)] * 4 + [("rope", 1), ("plain", 1)]
    ones = jnp.ones((HEAD_DIM,), F32)
    hg = jnp.stack([q_gain] * 4 + [k_gain, ones])[:, None, :]
    outs = [(4, BF16), (2, F32)]

    cos_p, sin_p = _rope_tables(jnp.arange(t, dtype=jnp.int32))
    cos_pt, sin_pt = jnp.tile(cos_p, (b, 1)), jnp.tile(sin_p, (b, 1))
    q, kv = _norm_proj(xp, norm_gain, w, hg, cos_pt, sin_pt, tiles, outs, _tile_rows(b * t))
    o = _moba_prompt_attn(q, kv, b, t)
    xp = _out_proj(o, w_out, xp, _tile_rows(b * t))

    cos_s, sin_s = _rope_tables(jnp.full((bs,), past, jnp.int32))
    qs, kv_s = _norm_proj(xs, norm_gain, w, hg, cos_s, sin_s, tiles, outs, bs)
    cache2 = cache.reshape(cache.shape[0], cache.shape[1], psz, -1)
    idx = _moba_sample_gate(cache2, layer, page_table, qs.reshape(bs, KV_HEADS, GROUP, HEAD_DIM))
    idx = idx.reshape(bs, N_HEADS, 128)[:, :, :MOBA_TOPK].reshape(bs, N_HEADS * MOBA_TOPK)
    o_s = _moba_sample_attn(cache2, layer, page_table, idx, qs.reshape(bs, N_HEADS, 1, HEAD_DIM),
                            kv_s.reshape(bs, 2 * KV_HEADS, 1, HEAD_DIM))
    xs = _out_proj(o_s.reshape(bs, N_HEADS * HEAD_DIM), w_out, xs, bs)
    return (xp, xs, kv.reshape(b, t, 2, KV_HEADS, HEAD_DIM), kv_s.reshape(bs, 1, 2, KV_HEADS, HEAD_DIM))


def _sb_layer(xp, xs, b, t, cache, layer, page_table, norm_gain, w_in, w_out):
    bs = xs.shape[0]
    psz = cache.shape[2]
    aw = N_HEADS * HEAD_DIM
    w = w_in.astype(BF16)
    w_out = w_out.astype(BF16)
    n_tiles = w.shape[1] // 512
    tiles = [("plain", 0)] * n_tiles
    hg = jnp.ones((n_tiles, 1, HEAD_DIM), F32)
    outs = [(n_tiles, F32)]
    dummy = jnp.zeros((b * t, HEAD_DIM), F32)
    (qkv,) = _norm_proj(xp, norm_gain, w, hg, dummy, dummy, tiles, outs, _tile_rows(b * t))
    o = _sb_prompt_attn(qkv, b, t)
    xp = _out_proj(o, w_out, xp, _tile_rows(b * t))

    (qkv_s,) = _norm_proj(xs, norm_gain, w, hg, dummy[:bs], dummy[:bs], tiles, outs, bs)
    qs = qkv_s[:, :aw].astype(BF16).reshape(bs, N_HEADS, HEAD_DIM)
    eye = jnp.eye(N_HEADS, 128, dtype=BF16)
    qbd = (qs[:, :, :, None] * eye[None, :, None, :]).reshape(bs, aw, 128)
    cache2 = cache.reshape(cache.shape[0], cache.shape[1], psz, -1)
    o_s = _sb_sample_attn(cache2, layer, page_table, qbd)
    xs = _out_proj(o_s.reshape(bs, aw).astype(BF16), w_out, xs, bs)
    return (xp, xs, qkv[:, aw:].reshape(b, t, 2, N_HEADS, HEAD_DIM), qkv_s[:, aw:].reshape(bs, 1, 2, N_HEADS, HEAD_DIM))


def kernel(x_prompt, x_sample, cache_nsa_kv, state_nsa_win, cache_moba_kv, cache_sb_kv, page_table, mix_norm, ffn_norm, nsa_w_in, nsa_w_out, nsa_q_gain, nsa_kc_gain, nsa_ks_gain, nsa_kw_gain, nsa_cmp_pos, nsa_cmp_w1, nsa_cmp_w2, moba_w_in, moba_w_out, moba_q_gain, moba_k_gain, sb_w_in, sb_w_out, ffn_w_gate, ffn_w_up, ffn_w_down):
    b, t, d = x_prompt.shape
    bs, ts, _ = x_sample.shape
    assert ts == 1
    depth = mix_norm.shape[0]
    xp = x_prompt.reshape(b * t, d)
    xs = x_sample.reshape(bs * ts, d)
    nsa_kv_p, nsa_kv_s, nsa_win_p, nsa_win_s = [], [], [], []
    moba_kv_p, moba_kv_s, sb_kv_p, sb_kv_s = [], [], [], []
    for i in range(depth):
        j = i // N_MIXERS
        if i % N_MIXERS == 0:
            xp, xs, kvp, kvs, wp, ws = _nsa_layer(
                xp, xs, b, t, cache_nsa_kv, j, page_table, state_nsa_win[j], mix_norm[i], nsa_w_in[j], nsa_w_out[j],
                nsa_q_gain[j], nsa_kc_gain[j], nsa_ks_gain[j], nsa_kw_gain[j], nsa_cmp_pos[j], nsa_cmp_w1[j],
                nsa_cmp_w2[j])
            nsa_kv_p.append(kvp)
            nsa_kv_s.append(kvs)
            nsa_win_p.append(wp)
            nsa_win_s.append(ws)
        elif i % N_MIXERS == 1:
            xp, xs, kvp, kvs = _moba_layer(xp, xs, b, t, cache_moba_kv, j, page_table, mix_norm[i], moba_w_in[j],
                                           moba_w_out[j], moba_q_gain[j], moba_k_gain[j])
            moba_kv_p.append(kvp)
            moba_kv_s.append(kvs)
        else:
            xp, xs, kvp, kvs = _sb_layer(xp, xs, b, t, cache_sb_kv, j, page_table, mix_norm[i], sb_w_in[j], sb_w_out[j])
            sb_kv_p.append(kvp)
            sb_kv_s.append(kvs)
        wg, wu, wd = ffn_w_gate[i].astype(BF16), ffn_w_up[i].astype(BF16), ffn_w_down[i].astype(BF16)
        xp = _ffn(xp, ffn_norm[i], wg, wu, wd, _tile_rows(b * t))
        xs = _ffn(xs, ffn_norm[i], wg, wu, wd, bs * ts)
    return (xp.reshape(b, t, d), xs.reshape(bs, ts, d), jnp.stack(nsa_kv_p), jnp.stack(nsa_kv_s),
            jnp.stack(nsa_win_p), jnp.stack(nsa_win_s), jnp.stack(moba_kv_p), jnp.stack(moba_kv_s),
            jnp.stack(sb_kv_p), jnp.stack(sb_kv_s))
```

```python
import functools
import math

import jax
import jax.numpy as jnp
from jax import lax
from jax.experimental import pallas as pl
from jax.experimental.pallas import tpu as pltpu

N_HEADS = 16
HEAD_DIM = 128
KV_HEADS = 4
GROUP = N_HEADS // KV_HEADS
CMP_BLOCK = 64
N_SELECT = 16
WINDOW = 512
MOBA_BLOCK = 256
MOBA_TOPK = 3
N_MIXERS = 3
ROPE_THETA = 10000.0
RMS_EPS = 1e-6
NEG_INF = -1e30
REMOVED = -3e38
SCALE = HEAD_DIM ** -0.5

VMEM_LIMIT = 56 * 1024 * 1024
F32 = jnp.float32
BF16 = jnp.bfloat16


def _cparams(sem):
    return pltpu.CompilerParams(dimension_semantics=sem, vmem_limit_bytes=VMEM_LIMIT)


def _dot(a, b):
    return jnp.dot(a, b, preferred_element_type=F32)


def _dot_nt(a, b):
    return lax.dot_general(a, b, (((1,), (1,)), ((), ())), preferred_element_type=F32)


def _iota(shape, dim):
    return lax.broadcasted_iota(jnp.int32, shape, dim)


def _masked_softmax(s, mask):
    s = jnp.where(mask, s, NEG_INF)
    m = jnp.max(s, axis=-1, keepdims=True)
    e = jnp.exp(s - m)
    p = e * (1.0 / jnp.sum(e, axis=-1, keepdims=True))
    return jnp.where(mask, p, 0.0)


def _attend(q, k, v, mask):
    s = jnp.where(mask, _dot_nt(q, k) * SCALE, NEG_INF)
    e = jnp.exp(s - jnp.max(s, axis=-1, keepdims=True))
    return _dot(e.astype(BF16), v) * (1.0 / jnp.sum(e, axis=-1, keepdims=True))


def _head_norm_rope(x, gain, cos, sin):
    y = x * lax.rsqrt(jnp.mean(x * x, axis=-1, keepdims=True) + RMS_EPS) * gain
    return y * cos + pltpu.roll(y, HEAD_DIM // 2, 1) * sin


def _rope_tables(pos):
    half = HEAD_DIM // 2
    inv_freq = ROPE_THETA ** (-jnp.arange(half, dtype=F32) / half)
    ang = pos.astype(F32)[:, None] * inv_freq[None, :]
    cos, sin = jnp.cos(ang), jnp.sin(ang)
    return jnp.concatenate([cos, cos], axis=-1), jnp.concatenate([-sin, sin], axis=-1)


def _rank_select(val, cand, k, axis):
    n = val.shape[axis]
    idx = _iota(val.shape, axis)
    v = jnp.where(cand, val, NEG_INF)
    rank = jnp.zeros(val.shape, jnp.int32)
    for j in range(n):
        vj = lax.slice_in_dim(v, j, j + 1, axis=axis)
        ahead = (vj > v) | ((vj == v) & (idx > j))
        rank = rank + jnp.where(ahead, 1, 0)
    return cand & (rank < k)


def _topk_indices(val, k):
    rows, n = val.shape
    lane = _iota((rows, n), 1).astype(F32)
    out_lane = _iota((rows, 128), 1)
    out = jnp.zeros((rows, 128), F32)
    for i in range(k):
        m = jnp.max(val, axis=-1, keepdims=True)
        pick = jnp.min(jnp.where(val == m, lane, float(n)), axis=-1, keepdims=True)
        out = jnp.where(out_lane == i, pick, out)
        val = jnp.where(lane == pick, REMOVED, val)
    return out.astype(jnp.int32)


def _proj_kernel(x_ref, g_ref, w_ref, hg_ref, cos_ref, sin_ref, *refs, tiles, n_out):
    out_refs, h_scr = refs[:n_out], refs[n_out]
    j = pl.program_id(1)

    @pl.when(j == 0)
    def _():
        x = x_ref[...]
        h = x * lax.rsqrt(jnp.mean(x * x, axis=-1, keepdims=True) + RMS_EPS) * g_ref[...]
        h_scr[...] = h.astype(BF16)

    acc = _dot(h_scr[...], w_ref[...])
    for jt, (mode, oi) in enumerate(tiles):
        @pl.when(j == jt)
        def _(mode=mode, oi=oi):
            o_ref = out_refs[oi]
            if mode == "rope":
                cos, sin, gain = cos_ref[...], sin_ref[...], hg_ref[0]
                for hh in range(acc.shape[1] // HEAD_DIM):
                    sl = slice(hh * HEAD_DIM, (hh + 1) * HEAD_DIM)
                    o_ref[:, sl] = _head_norm_rope(acc[:, sl], gain, cos, sin).astype(o_ref.dtype)
            elif mode == "sigmoid":
                o_ref[...] = (1.0 / (1.0 + jnp.exp(-acc))).astype(o_ref.dtype)
            else:
                o_ref[...] = acc.astype(o_ref.dtype)


def _norm_proj(x, gain, w, head_gains, cos, sin, tiles, outs, tm, tn=512):
    m, d = x.shape
    n_tiles = len(tiles)
    assert w.shape == (d, n_tiles * tn) and m % tm == 0
    starts = []
    for oi in range(len(outs)):
        starts.append(min(jt for jt, t in enumerate(tiles) if t[1] == oi))

    def out_map(oi):
        s, cnt = starts[oi], outs[oi][0]
        return lambda i, j: (i, jnp.clip(j - s, 0, cnt - 1))

    return pl.pallas_call(
        functools.partial(_proj_kernel, tiles=tuple(tiles), n_out=len(outs)),
        grid=(m // tm, n_tiles),
        in_specs=[
            pl.BlockSpec((tm, d), lambda i, j: (i, 0)),
            pl.BlockSpec((1, d), lambda i, j: (0, 0)),
            pl.BlockSpec((d, tn), lambda i, j: (0, j)),
            pl.BlockSpec((1, 1, HEAD_DIM), lambda i, j: (j, 0, 0)),
            pl.BlockSpec((tm, HEAD_DIM), lambda i, j: (i, 0)),
            pl.BlockSpec((tm, HEAD_DIM), lambda i, j: (i, 0)),
        ],
        out_specs=[pl.BlockSpec((tm, tn), out_map(oi)) for oi in range(len(outs))],
        out_shape=[jax.ShapeDtypeStruct((m, cnt * tn), dt) for cnt, dt in outs],
        scratch_shapes=[pltpu.VMEM((tm, d), BF16)],
        compiler_params=_cparams(("parallel", "arbitrary")),
        name="norm_proj",
    )(x, gain.reshape(1, d), w, head_gains, cos, sin)


def _out_proj_kernel(o_ref, w_ref, x_ref, y_ref):
    y_ref[...] = x_ref[...] + _dot(o_ref[...], w_ref[...])


def _out_proj(o, w, x, tm, tn=512):
    m, k = o.shape
    n = w.shape[1]
    return pl.pallas_call(
        _out_proj_kernel,
        grid=(m // tm, n // tn),
        in_specs=[
            pl.BlockSpec((tm, k), lambda i, j: (i, 0)),
            pl.BlockSpec((k, tn), lambda i, j: (0, j)),
            pl.BlockSpec((tm, tn), lambda i, j: (i, j)),
        ],
        out_specs=pl.BlockSpec((tm, tn), lambda i, j: (i, j)),
        out_shape=jax.ShapeDtypeStruct((m, n), F32),
        compiler_params=_cparams(("parallel", "arbitrary")),
        name="out_proj",
    )(o, w, x)


def _ffn_kernel(x_ref, g_ref, wg_ref, wu_ref, wd_ref, y_ref, h_scr, acc_scr):
    f = pl.program_id(1)

    @pl.when(f == 0)
    def _():
        x = x_ref[...]
        h = x * lax.rsqrt(jnp.mean(x * x, axis=-1, keepdims=True) + RMS_EPS) * g_ref[...]
        h_scr[...] = h.astype(BF16)
        acc_scr[...] = jnp.zeros_like(acc_scr)

    h = h_scr[...]
    a = _dot(h, wg_ref[...])
    u = _dot(h, wu_ref[...])
    act = (a * (1.0 / (1.0 + jnp.exp(-a))) * u).astype(BF16)
    acc_scr[...] += _dot(act, wd_ref[...])

    @pl.when(f == pl.num_programs(1) - 1)
    def _():
        y_ref[...] = x_ref[...] + acc_scr[...]


def _ffn(x, gain, wg, wu, wd, tm, tf=512):
    m, d = x.shape
    dff = wg.shape[1]
    assert dff % tf == 0 and m % tm == 0
    return pl.pallas_call(
        _ffn_kernel,
        grid=(m // tm, dff // tf),
        in_specs=[
            pl.BlockSpec((tm, d), lambda i, f: (i, 0)),
            pl.BlockSpec((1, d), lambda i, f: (0, 0)),
            pl.BlockSpec((d, tf), lambda i, f: (0, f)),
            pl.BlockSpec((d, tf), lambda i, f: (0, f)),
            pl.BlockSpec((tf, d), lambda i, f: (f, 0)),
        ],
        out_specs=pl.BlockSpec((tm, d), lambda i, f: (i, 0)),
        out_shape=jax.ShapeDtypeStruct((m, d), F32),
        scratch_shapes=[pltpu.VMEM((tm, d), BF16), pltpu.VMEM((tm, d), F32)],
        compiler_params=_cparams(("parallel", "arbitrary")),
        name="ffn",
    )(x, gain.reshape(1, d), wg, wu, wd)


def _gelu_tanh(x):
    return 0.5 * x * (1.0 + jnp.tanh(math.sqrt(2.0 / math.pi) * (x + 0.044715 * (x * x * x))))


def _compress_slab(load, pos_ref, w1_ref, w2_ref, gain_ref, cos_ref, sin_ref, ok_ref, ov_ref, nblk):
    jc = 8

    for s, o_ref in ((0, ok_ref), (1, ov_ref)):
        def body(jj, acc, s=s):
            pieces = []
            for g in range(KV_HEADS):
                cols = []
                for jo in range(jc):
                    j = jj * jc + jo
                    xj = load(s * KV_HEADS + g, j)
                    cols.append((xj + pos_ref[s, pl.ds(j, 1), :]).astype(BF16))
                pieces.append(jnp.concatenate(cols, axis=1))
            lhs = jnp.concatenate(pieces, axis=0)
            k0 = pl.multiple_of(jj * (jc * HEAD_DIM), jc * HEAD_DIM)
            return acc + _dot(lhs, w1_ref[s, pl.ds(k0, jc * HEAD_DIM), :])

        hidden = lax.fori_loop(0, CMP_BLOCK // jc, body, jnp.zeros((KV_HEADS * nblk, w1_ref.shape[2]), F32))
        y = _dot(_gelu_tanh(hidden).astype(BF16), w2_ref[s])
        for g in range(KV_HEADS):
            yg = y[g * nblk:(g + 1) * nblk]
            if s == 0:
                yg = _head_norm_rope(yg, gain_ref[...], cos_ref[...], sin_ref[...])
            o_ref[g] = yg.astype(o_ref.dtype)


def _compress_prompt_kernel(*refs, nblk):
    n_col = 2 * KV_HEADS
    load = lambda c, j: refs[c][pl.ds(j, nblk, stride=CMP_BLOCK), :]
    _compress_slab(load, *refs[n_col:], nblk)


def _compress_prompt(rows, b, t, pos_emb, w1, w2, kc_gain, cos_c, sin_c):
    nc = t // CMP_BLOCK
    n_col = 2 * KV_HEADS
    out = jax.ShapeDtypeStruct((b, KV_HEADS, nc, HEAD_DIM), BF16)
    return pl.pallas_call(
        functools.partial(_compress_prompt_kernel, nblk=nc),
        grid=(b,),
        in_specs=[pl.BlockSpec((t, HEAD_DIM), lambda i, c=c: (i, c)) for c in range(n_col)] + [
            pl.BlockSpec(pos_emb.shape, lambda i: (0, 0, 0)),
            pl.BlockSpec(w1.shape, lambda i: (0, 0, 0)),
            pl.BlockSpec(w2.shape, lambda i: (0, 0, 0)),
            pl.BlockSpec((1, HEAD_DIM), lambda i: (0, 0)),
            pl.BlockSpec((nc, HEAD_DIM), lambda i: (0, 0)),
            pl.BlockSpec((nc, HEAD_DIM), lambda i: (0, 0)),
        ],
        out_specs=[pl.BlockSpec((None, KV_HEADS, nc, HEAD_DIM), lambda i: (i, 0, 0, 0))] * 2,
        out_shape=[out, out],
        compiler_params=_cparams(("arbitrary",)),
        name="nsa_compress_prompt",
    )(*([rows] * n_col), pos_emb, w1, w2, kc_gain.reshape(1, HEAD_DIM), cos_c, sin_c)


def _compress_sample_kernel(pt_ref, cache_ref, pos_ref, w1_ref, w2_ref, gain_ref, cos_ref, sin_ref,
                            ok_ref, ov_ref, slab, sems, *, layer, pages):
    b, c = pl.program_id(0), pl.program_id(1)
    psz = slab.shape[0] // pages
    nblk = slab.shape[0] // CMP_BLOCK

    def page_copy(p):
        page = pt_ref[b, c * pages + p]
        return pltpu.make_async_copy(cache_ref.at[layer, page, :, 0], slab.at[pl.ds(p * psz, psz)], sems.at[p])

    for p in range(pages):
        page_copy(p).start()
    for p in range(pages):
        page_copy(p).wait()
    load = lambda col, j: slab[pl.ds(j, nblk, stride=CMP_BLOCK), col, :]
    _compress_slab(load, pos_ref, w1_ref, w2_ref, gain_ref, cos_ref, sin_ref, ok_ref, ov_ref, nblk)


def _compress_sample(cache, layer, page_table, pos_emb, w1, w2, kc_gain, cos_c, sin_c, pages=16):
    b, n_pages = page_table.shape
    psz = cache.shape[2]
    pages = min(pages, n_pages)
    assert n_pages % pages == 0 and psz % CMP_BLOCK == 0
    nblk = pages * psz // CMP_BLOCK
    nc = n_pages * psz // CMP_BLOCK
    n_col = 2 * KV_HEADS
    out = jax.ShapeDtypeStruct((b, KV_HEADS, nc, HEAD_DIM), BF16)
    grid_spec = pltpu.PrefetchScalarGridSpec(
        num_scalar_prefetch=1,
        grid=(b, n_pages // pages),
        in_specs=[
            pl.BlockSpec(memory_space=pl.ANY),
            pl.BlockSpec(pos_emb.shape, lambda i, c, pt: (0, 0, 0)),
            pl.BlockSpec(w1.shape, lambda i, c, pt: (0, 0, 0)),
            pl.BlockSpec(w2.shape, lambda i, c, pt: (0, 0, 0)),
            pl.BlockSpec((1, HEAD_DIM), lambda i, c, pt: (0, 0)),
            pl.BlockSpec((nblk, HEAD_DIM), lambda i, c, pt: (c, 0)),
            pl.BlockSpec((nblk, HEAD_DIM), lambda i, c, pt: (c, 0)),
        ],
        out_specs=[pl.BlockSpec((None, KV_HEADS, nblk, HEAD_DIM), lambda i, c, pt: (i, 0, c, 0))] * 2,
        scratch_shapes=[pltpu.VMEM((pages * psz, n_col, HEAD_DIM), F32), pltpu.SemaphoreType.DMA((pages,))],
    )
    return pl.pallas_call(
        functools.partial(_compress_sample_kernel, layer=layer, pages=pages),
        grid_spec=grid_spec,
        out_shape=[out, out],
        compiler_params=_cparams(("arbitrary", "arbitrary")),
        name="nsa_compress_sample",
    )(page_table, cache, pos_emb, w1, w2, kc_gain.reshape(1, HEAD_DIM), cos_c, sin_c)


def _nsa_prompt_kernel(q_ref, ks_ref, vs_ref, kw_ref, vw_ref, kc_ref, vc_ref, gt_ref, o_ref,
                       ks_s, vs_s, kw_s, vw_s, *, t_len, tq):
    qi = pl.program_id(2)

    @pl.when(qi == 0)
    def _():
        ks_s[...] = ks_ref[...].astype(BF16)
        vs_s[...] = vs_ref[...].astype(BF16)
        kw_s[...] = kw_ref[...].astype(BF16)
        vw_s[...] = vw_ref[...].astype(BF16)

    nc = kc_ref.shape[0]
    t0 = qi * tq
    tpos = t0 + _iota((tq, 1), 0)
    qb = q_ref[...]
    gt = gt_ref[...]
    kc, vc = kc_ref[...], vc_ref[...]
    heads = [qb[:, r * HEAD_DIM:(r + 1) * HEAD_DIM] for r in range(GROUP)]

    cmask = (_iota((tq, nc), 1) * CMP_BLOCK + (CMP_BLOCK - 1)) <= tpos
    tpos_t = t0 + _iota((nc, tq), 1)
    cblk_t = _iota((nc, tq), 0)
    cmask_t = (cblk_t * CMP_BLOCK + (CMP_BLOCK - 1)) <= tpos_t
    imp_t = jnp.zeros((nc, tq), F32)
    o_cmp = []
    for q_r in heads:
        p = _masked_softmax(_dot_nt(q_r, kc) * SCALE, cmask)
        o_cmp.append(_dot(p.astype(BF16), vc))
        s_t = jnp.where(cmask_t, _dot_nt(kc, q_r) * SCALE, NEG_INF)
        e_t = jnp.exp(s_t - jnp.max(s_t, axis=0, keepdims=True))
        imp_t = imp_t + jnp.where(cmask_t, e_t * (1.0 / jnp.sum(e_t, axis=0, keepdims=True)), 0.0)

    cur_t = tpos_t >> 6
    sel_t = _rank_select(imp_t, cblk_t < cur_t, N_SELECT - 1, axis=0) | (cblk_t == cur_t)
    sel_t = jnp.where(sel_t, 1.0, 0.0).astype(BF16)

    wn = min(WINDOW + tq, t_len)
    w0 = pl.multiple_of(jnp.maximum(t0 + tq - wn, 0), tq)
    wd = tpos - (w0 + _iota((tq, wn), 1))
    wmask = (wd >= 0) & (wd < WINDOW)
    kw, vw = kw_s[pl.ds(w0, wn), :], vw_s[pl.ds(w0, wn), :]
    o_win = [_attend(q_r, kw, vw, wmask) for q_r in heads]

    n_prefix = 4 if t_len % (4 * tq) == 0 else 1
    step = t_len // n_prefix
    for pi in range(n_prefix):
        @pl.when((t0 + tq - 1) // step == pi)
        def _(lk=(pi + 1) * step):
            expand = jnp.where((_iota((nc, lk), 1) >> 6) == _iota((nc, lk), 0), 1.0, 0.0).astype(BF16)
            picked = lax.dot_general(sel_t, expand, (((0,), (0,)), ((), ())), preferred_element_type=F32)
            smask = (picked > 0.5) & (_iota((tq, lk), 1) <= tpos)
            ks, vs = ks_s[0:lk, :], vs_s[0:lk, :]
            for r, q_r in enumerate(heads):
                o_slc = _attend(q_r, ks, vs, smask)
                o = (gt[:, 3 * r:3 * r + 1] * o_cmp[r] + gt[:, 3 * r + 1:3 * r + 2] * o_slc
                     + gt[:, 3 * r + 2:3 * r + 3] * o_win[r])
                o_ref[:, r * HEAD_DIM:(r + 1) * HEAD_DIM] = o.astype(o_ref.dtype)


def _nsa_prompt_attn(q, rows, win, k_cmp, v_cmp, gates, b, t, tq=128):
    nq = t // tq
    nc = t // CMP_BLOCK
    gw = GROUP * HEAD_DIM
    kv_spec = lambda col0: pl.BlockSpec((t, HEAD_DIM), lambda i, g, qi: (i, col0 + g))
    cmp_spec = pl.BlockSpec((None, None, nc, HEAD_DIM), lambda i, g, qi: (i, g, 0, 0))
    return pl.pallas_call(
        functools.partial(_nsa_prompt_kernel, t_len=t, tq=tq),
        grid=(b, KV_HEADS, nq),
        in_specs=[
            pl.BlockSpec((tq, gw), lambda i, g, qi: (i * nq + qi, g)),
            kv_spec(2 * KV_HEADS), kv_spec(3 * KV_HEADS), kv_spec(0), kv_spec(KV_HEADS),
            cmp_spec, cmp_spec,
            pl.BlockSpec((tq, HEAD_DIM), lambda i, g, qi: (i * nq + qi, g)),
        ],
        out_specs=pl.BlockSpec((tq, gw), lambda i, g, qi: (i * nq + qi, g)),
        out_shape=jax.ShapeDtypeStruct((b * t, N_HEADS * HEAD_DIM), BF16),
        scratch_shapes=[pltpu.VMEM((t, HEAD_DIM), BF16)] * 4,
        compiler_params=_cparams(("parallel", "parallel", "arbitrary")),
        name="nsa_prompt_attn",
    )(q, rows, rows, win, win, k_cmp, v_cmp, gates)


def _nsa_sample_cmp_kernel(q_ref, kc_ref, vc_ref, o_ref, idx_ref, *, pos):
    nc = kc_ref.shape[0]
    cblk = _iota((GROUP, nc), 1)
    cmask = (cblk * CMP_BLOCK + (CMP_BLOCK - 1)) <= pos
    p = _masked_softmax(_dot_nt(q_ref[...], kc_ref[...]) * SCALE, cmask)
    o_ref[...] = _dot(p.astype(BF16), vc_ref[...])
    imp = jnp.sum(p, axis=0, keepdims=True)
    cand = _iota((1, nc), 1) < (pos // CMP_BLOCK)
    idx_ref[...] = _topk_indices(jnp.where(cand, imp, NEG_INF), N_SELECT - 1)


def _nsa_sample_cmp(q4, k_cmp, v_cmp, pos):
    b = q4.shape[0]
    nc = k_cmp.shape[2]
    assert nc >= N_SELECT - 1 and pos // CMP_BLOCK >= N_SELECT - 1
    spec4 = pl.BlockSpec((None, None, GROUP, HEAD_DIM), lambda i, g: (i, g, 0, 0))
    cspec = pl.BlockSpec((None, None, nc, HEAD_DIM), lambda i, g: (i, g, 0, 0))
    return pl.pallas_call(
        functools.partial(_nsa_sample_cmp_kernel, pos=pos),
        grid=(b, KV_HEADS),
        in_specs=[spec4, cspec, cspec],
        out_specs=[spec4, pl.BlockSpec((None, None, 1, 128), lambda i, g: (i, g, 0, 0))],
        out_shape=[jax.ShapeDtypeStruct((b, KV_HEADS, GROUP, HEAD_DIM), F32),
                   jax.ShapeDtypeStruct((b, KV_HEADS, 1, 128), jnp.int32)],
        compiler_params=_cparams(("parallel", "parallel")),
        name="nsa_sample_cmp",
    )(q4, k_cmp, v_cmp)


def _nsa_sample_slc_kernel(pt_ref, idx_ref, q_ref, blk_ref, knew_ref, vnew_ref,
                           win_ref, kwnew_ref, vwnew_ref, ocmp_ref, gt_ref, o_ref,
                           m_scr, l_scr, acc_scr, *, n_sel):
    g, n = pl.program_id(1), pl.program_id(2)
    q = q_ref[...]
    per_row = 4 * KV_HEADS
    kblk = blk_ref[pl.ds(2 * KV_HEADS + g, CMP_BLOCK, stride=per_row), :]
    vblk = blk_ref[pl.ds(3 * KV_HEADS + g, CMP_BLOCK, stride=per_row), :]

    @pl.when(n == 0)
    def _():
        m_scr[...] = jnp.sum(q.astype(F32) * knew_ref[...].astype(BF16).astype(F32), axis=-1, keepdims=True) * SCALE
        l_scr[...] = jnp.ones_like(l_scr)
        acc_scr[...] = jnp.broadcast_to(vnew_ref[...].astype(BF16).astype(F32), acc_scr.shape)

    s = _dot_nt(q, kblk.astype(BF16)) * SCALE
    m_old = m_scr[...]
    m_new = jnp.maximum(m_old, jnp.max(s, axis=-1, keepdims=True))
    alpha = jnp.exp(m_old - m_new)
    e = jnp.exp(s - m_new)
    l_scr[...] = alpha * l_scr[...] + jnp.sum(e, axis=-1, keepdims=True)
    acc_scr[...] = alpha * acc_scr[...] + _dot(e.astype(BF16), vblk.astype(BF16))
    m_scr[...] = m_new

    @pl.when(n == n_sel - 1)
    def _():
        o_slc = acc_scr[...] * (1.0 / l_scr[...])
        nw = win_ref.shape[0] // (2 * KV_HEADS)
        kwin = win_ref[pl.ds(g, nw, stride=2 * KV_HEADS), :]
        vwin = win_ref[pl.ds(KV_HEADS + g, nw, stride=2 * KV_HEADS), :]
        sw = _dot_nt(q, kwin.astype(BF16)) * SCALE
        sw = jnp.where(nw - _iota((GROUP, nw), 1) < WINDOW, sw, NEG_INF)
        s_new = jnp.sum(q.astype(F32) * kwnew_ref[...].astype(BF16).astype(F32), axis=-1, keepdims=True) * SCALE
        mw = jnp.maximum(jnp.max(sw, axis=-1, keepdims=True), s_new)
        ew = jnp.exp(sw - mw)
        e_new = jnp.exp(s_new - mw)
        lw = jnp.sum(ew, axis=-1, keepdims=True) + e_new
        pw = ew * (1.0 / lw)
        p_new = (e_new * (1.0 / lw)).astype(BF16).astype(F32)
        o_win = _dot(pw.astype(BF16), vwin.astype(BF16)) + p_new * vwnew_ref[...].astype(BF16).astype(F32)
        gt = gt_ref[...]
        rowi = _iota((GROUP, 1), 0)
        gcol = [sum(jnp.where(rowi == r, gt[:, 3 * r + c:3 * r + c + 1], 0.0) for r in range(GROUP)) for c in range(3)]
        o_ref[...] = gcol[0] * ocmp_ref[...] + gcol[1] * o_slc + gcol[2] * o_win


def _nsa_sample_slc(cache, layer, page_table, idx, q4, rows_s, win_buf, win_layer, win_s, o_cmp, gates_s, halves):
    b = q4.shape[0]
    n_sel = idx.shape[1] // KV_HEADS

    def blk_map(i, g, n, pt, ix):
        blk = ix[i, g * n_sel + n]
        return (layer, pt[i, blk // halves] * halves + blk % halves, 0, 0)

    spec4 = pl.BlockSpec((None, None, GROUP, HEAD_DIM), lambda i, g, n, pt, ix: (i, g, 0, 0))
    new_spec = lambda col0: pl.BlockSpec((None, None, 1, HEAD_DIM), lambda i, g, n, pt, ix: (i, col0 + g, 0, 0))
    grid_spec = pltpu.PrefetchScalarGridSpec(
        num_scalar_prefetch=2,
        grid=(b, KV_HEADS, n_sel),
        in_specs=[
            spec4,
            pl.BlockSpec((None, None) + cache.shape[2:], blk_map),
            new_spec(2 * KV_HEADS), new_spec(3 * KV_HEADS),
            pl.BlockSpec((None, None) + win_buf.shape[2:], lambda i, g, n, pt, ix: (win_layer, i, 0, 0)),
            new_spec(0), new_spec(KV_HEADS),
            spec4,
            pl.BlockSpec((None, None, 1, HEAD_DIM), lambda i, g, n, pt, ix: (i, g, 0, 0)),
        ],
        out_specs=spec4,
        scratch_shapes=[pltpu.VMEM((GROUP, 1), F32), pltpu.VMEM((GROUP, 1), F32), pltpu.VMEM((GROUP, HEAD_DIM), F32)],
    )
    return pl.pallas_call(
        functools.partial(_nsa_sample_slc_kernel, n_sel=n_sel),
        grid_spec=grid_spec,
        out_shape=jax.ShapeDtypeStruct((b, KV_HEADS, GROUP, HEAD_DIM), F32),
        compiler_params=_cparams(("parallel", "parallel", "arbitrary")),
        name="nsa_sample_slc",
    )(page_table, idx, q4, cache, rows_s, rows_s, win_buf, win_s, win_s, o_cmp, gates_s)


def _moba_prompt_kernel(q_ref, k_ref, v_ref, o_ref, k_s, v_s, km_s, *, t_len, tq):
    qi = pl.program_id(2)
    nb = km_s.shape[0]

    @pl.when(qi == 0)
    def _():
        k = k_ref[...]
        k_s[...] = k.astype(BF16)
        v_s[...] = v_ref[...].astype(BF16)
        km_s[...] = (jnp.sum(k.reshape(nb, MOBA_BLOCK, HEAD_DIM), axis=1) * (1.0 / MOBA_BLOCK)).astype(BF16)

    t0 = qi * tq
    tpos = t0 + _iota((tq, 1), 0)
    cur_t = (t0 + _iota((nb, tq), 1)) >> 8
    blk_t = _iota((nb, tq), 0)
    qb = q_ref[...]
    km = km_s[...]
    heads = [qb[:, r * HEAD_DIM:(r + 1) * HEAD_DIM] for r in range(GROUP)]
    sel_t = []
    for q_r in heads:
        gate_t = _dot_nt(km, q_r)
        sel = _rank_select(gate_t, blk_t < cur_t, MOBA_TOPK, axis=0) | (blk_t == cur_t)
        sel_t.append(jnp.where(sel, 1.0, 0.0).astype(BF16))

    n_prefix = 4 if t_len % (4 * MOBA_BLOCK) == 0 else 1
    step = t_len // n_prefix
    for pi in range(n_prefix):
        @pl.when((t0 + tq - 1) // step == pi)
        def _(lk=(pi + 1) * step):
            expand = jnp.where((_iota((nb, lk), 1) >> 8) == _iota((nb, lk), 0), 1.0, 0.0).astype(BF16)
            causal = _iota((tq, lk), 1) <= tpos
            k, v = k_s[0:lk, :], v_s[0:lk, :]
            for r, q_r in enumerate(heads):
                picked = lax.dot_general(sel_t[r], expand, (((0,), (0,)), ((), ())), preferred_element_type=F32)
                o_ref[:, r * HEAD_DIM:(r + 1) * HEAD_DIM] = _attend(q_r, k, v, (picked > 0.5) & causal).astype(o_ref.dtype)


def _moba_prompt_attn(q, kv, b, t, tq=128):
    assert t % MOBA_BLOCK == 0
    nq = t // tq
    gw = GROUP * HEAD_DIM
    kv_spec = lambda col0: pl.BlockSpec((t, HEAD_DIM), lambda i, g, qi: (i, col0 + g))
    return pl.pallas_call(
        functools.partial(_moba_prompt_kernel, t_len=t, tq=tq),
        grid=(b, KV_HEADS, nq),
        in_specs=[pl.BlockSpec((tq, gw), lambda i, g, qi: (i * nq + qi, g)), kv_spec(0), kv_spec(KV_HEADS)],
        out_specs=pl.BlockSpec((tq, gw), lambda i, g, qi: (i * nq + qi, g)),
        out_shape=jax.ShapeDtypeStruct((b * t, N_HEADS * HEAD_DIM), BF16),
        scratch_shapes=[pltpu.VMEM((t, HEAD_DIM), BF16), pltpu.VMEM((t, HEAD_DIM), BF16),
                        pltpu.VMEM((t // MOBA_BLOCK, HEAD_DIM), BF16)],
        compiler_params=_cparams(("parallel", "parallel", "arbitrary")),
        name="moba_prompt_attn",
    )(q, kv, kv)


def _moba_sample_gate_kernel(pt_ref, q_ref, k_ref, idx_ref, ksum, *, pages_per_block):
    p = pl.program_id(1)

    @pl.when(p == 0)
    def _():
        ksum[...] = jnp.zeros_like(ksum)

    per_row = 2 * KV_HEADS
    row = pl.multiple_of((p // pages_per_block) * per_row, per_row)
    page = k_ref[...]
    ksum[pl.ds(row, per_row), :] += jnp.sum(page.reshape(page.shape[0] // per_row, per_row, HEAD_DIM), axis=0)

    @pl.when(p == pl.num_programs(1) - 1)
    def _():
        nb = ksum.shape[0] // per_row
        for g in range(KV_HEADS):
            km = (ksum[pl.ds(g, nb, stride=per_row), :] * (1.0 / MOBA_BLOCK)).astype(BF16)
            idx_ref[g] = _topk_indices(_dot_nt(q_ref[g], km), MOBA_TOPK)


def _moba_sample_gate(cache, layer, page_table, q4):
    b, n_pages = page_table.shape
    per_row = 2 * KV_HEADS
    psz = cache.shape[2] // per_row
    ppb = MOBA_BLOCK // psz
    nb = n_pages // ppb
    assert MOBA_BLOCK % psz == 0 and n_pages % ppb == 0 and nb >= MOBA_TOPK
    grid_spec = pltpu.PrefetchScalarGridSpec(
        num_scalar_prefetch=1,
        grid=(b, n_pages),
        in_specs=[
            pl.BlockSpec((None, KV_HEADS, GROUP, HEAD_DIM), lambda i, p, pt: (i, 0, 0, 0)),
            pl.BlockSpec((None, None) + cache.shape[2:], lambda i, p, pt: (layer, pt[i, p], 0, 0)),
        ],
        out_specs=pl.BlockSpec((None, KV_HEADS, GROUP, 128), lambda i, p, pt: (i, 0, 0, 0)),
        scratch_shapes=[pltpu.VMEM((nb * per_row, HEAD_DIM), F32)],
    )
    return pl.pallas_call(
        functools.partial(_moba_sample_gate_kernel, pages_per_block=ppb),
        grid_spec=grid_spec,
        out_shape=jax.ShapeDtypeStruct((b, KV_HEADS, GROUP, 128), jnp.int32),
        compiler_params=_cparams(("parallel", "arbitrary")),
        name="moba_sample_gate",
    )(page_table, q4, cache)


def _moba_sample_attn_kernel(pt_ref, idx_ref, q_ref, page_ref, knew_ref, vnew_ref, o_ref,
                             m_scr, l_scr, acc_scr):
    n = pl.program_id(2)
    q = q_ref[...]
    per_row = 2 * KV_HEADS
    g = pl.program_id(1) // GROUP
    psz = page_ref.shape[0] // per_row
    k_page = page_ref[pl.ds(g, psz, stride=per_row), :].astype(BF16)
    v_page = page_ref[pl.ds(KV_HEADS + g, psz, stride=per_row), :].astype(BF16)

    @pl.when(n == 0)
    def _():
        m_scr[...] = jnp.sum(q.astype(F32) * knew_ref[...].astype(BF16).astype(F32), axis=-1, keepdims=True) * SCALE
        l_scr[...] = jnp.ones_like(l_scr)
        acc_scr[...] = vnew_ref[...].astype(BF16).astype(F32)

    s = _dot_nt(q, k_page) * SCALE
    m_old = m_scr[...]
    m_new = jnp.maximum(m_old, jnp.max(s, axis=-1, keepdims=True))
    alpha = jnp.exp(m_old - m_new)
    e = jnp.exp(s - m_new)
    l_scr[...] = alpha * l_scr[...] + jnp.sum(e, axis=-1, keepdims=True)
    acc_scr[...] = alpha * acc_scr[...] + _dot(e.astype(BF16), v_page)
    m_scr[...] = m_new

    @pl.when(n == pl.num_programs(2) - 1)
    def _():
        o_ref[...] = (acc_scr[...] * (1.0 / l_scr[...])).astype(o_ref.dtype)


def _moba_sample_attn(cache, layer, page_table, idx, q16, kv_new):
    b = q16.shape[0]
    psz = cache.shape[2] // (2 * KV_HEADS)
    ppb = MOBA_BLOCK // psz
    n_sel = idx.shape[1] // N_HEADS
    n_steps = n_sel * ppb

    def page_map(i, h, n, pt, ix):
        return (layer, pt[i, ix[i, h * n_sel + n // ppb] * ppb + n % ppb], 0, 0)

    hspec = pl.BlockSpec((None, None, 1, HEAD_DIM), lambda i, h, n, pt, ix: (i, h, 0, 0))
    new_spec = lambda col0: pl.BlockSpec((None, None, 1, HEAD_DIM), lambda i, h, n, pt, ix: (i, col0 + h // GROUP, 0, 0))
    grid_spec = pltpu.PrefetchScalarGridSpec(
        num_scalar_prefetch=2,
        grid=(b, N_HEADS, n_steps),
        in_specs=[
            hspec,
            pl.BlockSpec((None, None) + cache.shape[2:], page_map),
            new_spec(0), new_spec(KV_HEADS),
        ],
        out_specs=hspec,
        scratch_shapes=[pltpu.VMEM((1, 1), F32), pltpu.VMEM((1, 1), F32), pltpu.VMEM((1, HEAD_DIM), F32)],
    )
    return pl.pallas_call(
        _moba_sample_attn_kernel,
        grid_spec=grid_spec,
        out_shape=jax.ShapeDtypeStruct((b, N_HEADS, 1, HEAD_DIM), BF16),
        compiler_params=_cparams(("parallel", "parallel", "arbitrary")),
        name="moba_sample_attn",
    )(page_table, idx, q16, cache, kv_new, kv_new)


def _log_sigmoid(z):
    return jnp.minimum(z, 0.0) - jnp.log(1.0 + jnp.exp(-jnp.abs(z)))


def _split2(x):
    hi = x.astype(BF16)
    return hi, (x - hi.astype(F32)).astype(BF16)


EXP_UNDERFLOW = -104.0


def _sb_prompt_kernel(q_ref, k_ref, v_ref, o_ref, k_s, v_s, *, tq):
    qi = pl.program_id(2)

    @pl.when(qi == 0)
    def _():
        k_s[...] = k_ref[...].astype(BF16)
        v_s[...] = v_ref[...].astype(BF16)

    q = q_ref[...].astype(BF16)
    below = _iota((tq, tq), 0) > _iota((tq, tq), 1)
    after = jnp.where(below, 1.0, 0.0).astype(BF16)

    def chunk(c, run, acc, diagonal):
        s0 = pl.multiple_of(c * tq, tq)
        z = _dot_nt(q, k_s[pl.ds(s0, tq), :]) * SCALE
        log_beta = _log_sigmoid(z)
        log_keep = log_beta - z
        if diagonal:
            log_keep = jnp.where(below, log_keep, 0.0)
        hi, lo = _split2(log_keep)
        within = _dot(hi, after) + _dot(lo, after)
        a = jnp.exp(log_beta + within + run)
        if diagonal:
            a = jnp.where(below, a, 0.0)
        acc = acc + _dot(a.astype(BF16), v_s[pl.ds(s0, tq), :])
        return run + within[:, 0:1] + log_keep[:, 0:1], acc

    run, acc = chunk(qi, jnp.zeros((tq, 1), F32), jnp.zeros((tq, HEAD_DIM), F32), True)

    def cond(carry):
        c, run, _ = carry
        return (c >= 0) & (jnp.max(run) >= EXP_UNDERFLOW)

    def body(carry):
        c, run, acc = carry
        run, acc = chunk(c, run, acc, False)
        return c - 1, run, acc

    _, _, acc = lax.while_loop(cond, body, (qi - 1, run, acc))
    o_ref[...] = acc.astype(o_ref.dtype)


def _sb_prompt_attn(qkv, b, t, tq=256):
    nq = t // tq
    return pl.pallas_call(
        functools.partial(_sb_prompt_kernel, tq=tq),
        grid=(b, N_HEADS, nq),
        in_specs=[
            pl.BlockSpec((tq, HEAD_DIM), lambda i, h, qi: (i * nq + qi, h)),
            pl.BlockSpec((t, HEAD_DIM), lambda i, h, qi: (i, N_HEADS + h)),
            pl.BlockSpec((t, HEAD_DIM), lambda i, h, qi: (i, 2 * N_HEADS + h)),
        ],
        out_specs=pl.BlockSpec((tq, HEAD_DIM), lambda i, h, qi: (i * nq + qi, h)),
        out_shape=jax.ShapeDtypeStruct((b * t, N_HEADS * HEAD_DIM), BF16),
        scratch_shapes=[pltpu.VMEM((t, HEAD_DIM), BF16)] * 2,
        compiler_params=_cparams(("parallel", "parallel", "arbitrary")),
        name="sb_prompt_attn",
    )(qkv, qkv, qkv)


def _sb_sample_kernel(pt_ref, qbd_ref, kv_ref, o_ref, run_scr, acc_scr):
    p = pl.program_id(1)
    per_row = 2 * N_HEADS
    psz = kv_ref.shape[0] // per_row

    @pl.when(p == 0)
    def _():
        run_scr[...] = jnp.zeros_like(run_scr)
        acc_scr[...] = jnp.zeros_like(acc_scr)

    def heads_wide(first):
        return jnp.concatenate([kv_ref[pl.ds(first + h, psz, stride=per_row), :].astype(BF16)
                                for h in range(N_HEADS)], axis=1)

    z = _dot(heads_wide(0), qbd_ref[...]) * SCALE
    log_beta = _log_sigmoid(z)
    log_keep = log_beta - z
    after = jnp.where(_iota((psz, psz), 1) > _iota((psz, psz), 0), 1.0, 0.0).astype(BF16)
    hi, lo = _split2(log_keep)
    within = _dot(after, hi) + _dot(after, lo)
    a = jnp.exp(log_beta + within + run_scr[...])
    acc_scr[...] += _dot(a.T.astype(BF16), heads_wide(N_HEADS))
    run_scr[...] += within[0:1, :] + log_keep[0:1, :]

    @pl.when(p == pl.num_programs(1) - 1)
    def _():
        acc = acc_scr[...]
        own = _iota(acc.shape, 0) == (_iota(acc.shape, 1) >> 7)
        o_ref[...] = jnp.sum(jnp.where(own, acc, 0.0), axis=0, keepdims=True)


def _sb_sample_attn(cache, layer, page_table, qbd):
    b, n_pages = page_table.shape
    aw = N_HEADS * HEAD_DIM
    grid_spec = pltpu.PrefetchScalarGridSpec(
        num_scalar_prefetch=1,
        grid=(b, n_pages),
        in_specs=[
            pl.BlockSpec((None, aw, 128), lambda i, p, pt: (i, 0, 0)),
            pl.BlockSpec((None, None) + cache.shape[2:], lambda i, p, pt: (layer, pt[i, n_pages - 1 - p], 0, 0)),
        ],
        out_specs=pl.BlockSpec((None, 1, aw), lambda i, p, pt: (i, 0, 0)),
        scratch_shapes=[pltpu.VMEM((1, 128), F32), pltpu.VMEM((128, aw), F32)],
    )
    return pl.pallas_call(
        _sb_sample_kernel,
        grid_spec=grid_spec,
        out_shape=jax.ShapeDtypeStruct((b, 1, aw), F32),
        compiler_params=_cparams(("parallel", "arbitrary")),
        name="sb_sample_attn",
    )(page_table, qbd, cache)


def _tile_rows(m):
    return 512 if m % 512 == 0 else m


def _nsa_weights(w_in, q_gain, ks_gain, kw_gain):
    d = w_in.shape[0]
    main = N_HEADS * HEAD_DIM + 6 * KV_HEADS * HEAD_DIM
    wg = w_in[:, main:].reshape(d, KV_HEADS, GROUP * 3)
    wg = jnp.pad(wg, ((0, 0), (0, 0), (0, HEAD_DIM - GROUP * 3))).reshape(d, KV_HEADS * HEAD_DIM)
    w = jnp.concatenate([w_in[:, :main], wg], axis=1).astype(BF16)
    tiles = [("rope", 0)] * 4 + [("plain", 1), ("plain", 1), ("rope", 1), ("plain", 1),
                                 ("rope", 2), ("plain", 2), ("sigmoid", 3)]
    ones = jnp.ones((HEAD_DIM,), F32)
    hg = jnp.stack([q_gain] * 4 + [ones, ones, ks_gain, ones, kw_gain, ones, ones])[:, None, :]
    outs = [(4, BF16), (4, F32), (2, F32), (1, F32)]
    return w, tiles, hg, outs


def _nsa_layer(xp, xs, b, t, cache, layer, page_table, win_all, norm_gain, w_in, w_out, q_gain, kc_gain,
               ks_gain, kw_gain, cmp_pos, cmp_w1, cmp_w2):
    bs = xs.shape[0]
    n_pages, psz = page_table.shape[1], cache.shape[2]
    past = n_pages * psz
    w, tiles, hg, outs = _nsa_weights(w_in, q_gain, ks_gain, kw_gain)
    w1, w2 = cmp_w1.astype(BF16), cmp_w2.astype(BF16)
    w_out = w_out.astype(BF16)

    cos_p, sin_p = _rope_tables(jnp.arange(t, dtype=jnp.int32))
    cos_pt, sin_pt = jnp.tile(cos_p, (b, 1)), jnp.tile(sin_p, (b, 1))
    q, rows, win, gates = _norm_proj(xp, norm_gain, w, hg, cos_pt, sin_pt, tiles, outs, _tile_rows(b * t))
    nc = t // CMP_BLOCK
    cos_c, sin_c = _rope_tables(jnp.arange(nc, dtype=jnp.int32) * CMP_BLOCK + (CMP_BLOCK - 1))
    k_cmp, v_cmp = _compress_prompt(rows, b, t, cmp_pos, w1, w2, kc_gain, cos_c, sin_c)
    o = _nsa_prompt_attn(q, rows, win, k_cmp, v_cmp, gates, b, t)
    xp = _out_proj(o, w_out, xp, _tile_rows(b * t))

    cos_s, sin_s = _rope_tables(jnp.full((bs,), past, jnp.int32))
    qs, rows_s, win_s, gates_s = _norm_proj(xs, norm_gain, w, hg, cos_s, sin_s, tiles, outs, bs)
    ncs = past // CMP_BLOCK
    cos_cs, sin_cs = _rope_tables(jnp.arange(ncs, dtype=jnp.int32) * CMP_BLOCK + (CMP_BLOCK - 1))
    n_layers, n_pool = cache.shape[:2]
    k_cmp_s, v_cmp_s = _compress_sample(cache.reshape(n_layers, n_pool, psz, 2, 2 * KV_HEADS, HEAD_DIM), layer,
                                        page_table, cmp_pos, w1, w2, kc_gain, cos_cs, sin_cs)
    q4 = qs.reshape(bs, KV_HEADS, GROUP, HEAD_DIM)
    o_cmp, idx = _nsa_sample_cmp(q4, k_cmp_s, v_cmp_s, past)
    idx = idx[:, :, 0, :N_SELECT - 1].reshape(bs, KV_HEADS * (N_SELECT - 1))
    halves = psz // CMP_BLOCK
    block_view = cache.reshape(n_layers, n_pool * halves, CMP_BLOCK * 4 * KV_HEADS, HEAD_DIM)
    nw = win_all.shape[2]
    win_view = win_all.reshape(win_all.shape[0], bs, nw * 2 * KV_HEADS, HEAD_DIM)
    o_s = _nsa_sample_slc(block_view, layer, page_table, idx, q4,
                          rows_s.reshape(bs, 4 * KV_HEADS, 1, HEAD_DIM), win_view, layer,
                          win_s.reshape(bs, 2 * KV_HEADS, 1, HEAD_DIM), o_cmp,
                          gates_s.reshape(bs, KV_HEADS, 1, HEAD_DIM), halves)
    xs = _out_proj(o_s.reshape(bs, N_HEADS * HEAD_DIM).astype(BF16), w_out, xs, bs)

    kv_p = rows.reshape(b, t, 4, KV_HEADS, HEAD_DIM)
    kv_s = rows_s.reshape(bs, 1, 4, KV_HEADS, HEAD_DIM)
    win_p = win.reshape(b, t, 2, KV_HEADS, HEAD_DIM)[:, -min(WINDOW, t):]
    win_new = jnp.concatenate([win_all[layer], win_s.reshape(bs, 1, 2, KV_HEADS, HEAD_DIM)], axis=1)[:, -nw:]
    return xp, xs, kv_p, kv_s, win_p, win_new


def _moba_layer(xp, xs, b, t, cache, layer, page_table, norm_gain, w_in, w_out, q_gain, k_gain):
    bs = xs.shape[0]
    n_pages, psz = page_table.shape[1], cache.shape[2]
    past = n_pages * psz
    w = w_in.astype(BF16)
    w_out = w_out.astype(BF16)
    tiles = [("rope", 0)] * 4 + [("rope", 1), ("plain", 1)]
    ones = jnp.ones((HEAD_DIM,), F32)
    hg = jnp.stack([q_gain] * 4 + [k_gain, ones])[:, None, :]
    outs = [(4, BF16), (2, F32)]

    cos_p, sin_p = _rope_tables(jnp.arange(t, dtype=jnp.int32))
    cos_pt, sin_pt = jnp.tile(cos_p, (b, 1)), jnp.tile(sin_p, (b, 1))
    q, kv = _norm_proj(xp, norm_gain, w, hg, cos_pt, sin_pt, tiles, outs, _tile_rows(b * t))
    o = _moba_prompt_attn(q, kv, b, t)
    xp = _out_proj(o, w_out, xp, _tile_rows(b * t))

    cos_s, sin_s = _rope_tables(jnp.full((bs,), past, jnp.int32))
    qs, kv_s = _norm_proj(xs, norm_gain, w, hg, cos_s, sin_s, tiles, outs, bs)
    cache2 = cache.reshape(cache.shape[0], cache.shape[1], psz * 2 * KV_HEADS, HEAD_DIM)
    idx = _moba_sample_gate(cache2, layer, page_table, qs.reshape(bs, KV_HEADS, GROUP, HEAD_DIM))
    idx = idx.reshape(bs, N_HEADS, 128)[:, :, :MOBA_TOPK].reshape(bs, N_HEADS * MOBA_TOPK)
    o_s = _moba_sample_attn(cache2, layer, page_table, idx, qs.reshape(bs, N_HEADS, 1, HEAD_DIM),
                            kv_s.reshape(bs, 2 * KV_HEADS, 1, HEAD_DIM))
    xs = _out_proj(o_s.reshape(bs, N_HEADS * HEAD_DIM), w_out, xs, bs)
    return (xp, xs, kv.reshape(b, t, 2, KV_HEADS, HEAD_DIM), kv_s.reshape(bs, 1, 2, KV_HEADS, HEAD_DIM))


def _sb_layer(xp, xs, b, t, cache, layer, page_table, norm_gain, w_in, w_out):
    bs = xs.shape[0]
    psz = cache.shape[2]
    aw = N_HEADS * HEAD_DIM
    w = w_in.astype(BF16)
    w_out = w_out.astype(BF16)
    n_tiles = w.shape[1] // 512
    tiles = [("plain", 0)] * n_tiles
    hg = jnp.ones((n_tiles, 1, HEAD_DIM), F32)
    outs = [(n_tiles, F32)]
    dummy = jnp.zeros((b * t, HEAD_DIM), F32)
    (qkv,) = _norm_proj(xp, norm_gain, w, hg, dummy, dummy, tiles, outs, _tile_rows(b * t))
    o = _sb_prompt_attn(qkv, b, t)
    xp = _out_proj(o, w_out, xp, _tile_rows(b * t))

    (qkv_s,) = _norm_proj(xs, norm_gain, w, hg, dummy[:bs], dummy[:bs], tiles, outs, bs)
    qs = qkv_s[:, :aw].astype(BF16).reshape(bs, N_HEADS, HEAD_DIM)
    eye = jnp.eye(N_HEADS, 128, dtype=BF16)
    qbd = (qs[:, :, :, None] * eye[None, :, None, :]).reshape(bs, aw, 128)
    cache2 = cache.reshape(cache.shape[0], cache.shape[1], psz * 2 * N_HEADS, HEAD_DIM)
    o_s = _sb_sample_attn(cache2, layer, page_table, qbd)
    xs = _out_proj(o_s.reshape(bs, aw).astype(BF16), w_out, xs, bs)
    return (xp, xs, qkv[:, aw:].reshape(b, t, 2, N_HEADS, HEAD_DIM), qkv_s[:, aw:].reshape(bs, 1, 2, N_HEADS, HEAD_DIM))


def kernel(x_prompt, x_sample, cache_nsa_kv, state_nsa_win, cache_moba_kv, cache_sb_kv, page_table, mix_norm, ffn_norm, nsa_w_in, nsa_w_out, nsa_q_gain, nsa_kc_gain, nsa_ks_gain, nsa_kw_gain, nsa_cmp_pos, nsa_cmp_w1, nsa_cmp_w2, moba_w_in, moba_w_out, moba_q_gain, moba_k_gain, sb_w_in, sb_w_out, ffn_w_gate, ffn_w_up, ffn_w_down):
    b, t, d = x_prompt.shape
    bs, ts, _ = x_sample.shape
    assert ts == 1
    depth = mix_norm.shape[0]
    xp = x_prompt.reshape(b * t, d)
    xs = x_sample.reshape(bs * ts, d)
    nsa_kv_p, nsa_kv_s, nsa_win_p, nsa_win_s = [], [], [], []
    moba_kv_p, moba_kv_s, sb_kv_p, sb_kv_s = [], [], [], []
    for i in range(depth):
        j = i // N_MIXERS
        if i % N_MIXERS == 0:
            xp, xs, kvp, kvs, wp, ws = _nsa_layer(
                xp, xs, b, t, cache_nsa_kv, j, page_table, state_nsa_win, mix_norm[i], nsa_w_in[j], nsa_w_out[j],
                nsa_q_gain[j], nsa_kc_gain[j], nsa_ks_gain[j], nsa_kw_gain[j], nsa_cmp_pos[j], nsa_cmp_w1[j],
                nsa_cmp_w2[j])
            nsa_kv_p.append(kvp)
            nsa_kv_s.append(kvs)
            nsa_win_p.append(wp)
            nsa_win_s.append(ws)
        elif i % N_MIXERS == 1:
            xp, xs, kvp, kvs = _moba_layer(xp, xs, b, t, cache_moba_kv, j, page_table, mix_norm[i], moba_w_in[j],
                                           moba_w_out[j], moba_q_gain[j], moba_k_gain[j])
            moba_kv_p.append(kvp)
            moba_kv_s.append(kvs)
        else:
            xp, xs, kvp, kvs = _sb_layer(xp, xs, b, t, cache_sb_kv, j, page_table, mix_norm[i], sb_w_in[j], sb_w_out[j])
            sb_kv_p.append(kvp)
            sb_kv_s.append(kvs)
        wg, wu, wd = ffn_w_gate[i].astype(BF16), ffn_w_up[i].astype(BF16), ffn_w_down[i].astype(BF16)
        xp = _ffn(xp, ffn_norm[i], wg, wu, wd, _tile_rows(b * t))
        xs = _ffn(xs, ffn_norm[i], wg, wu, wd, bs * ts)
    return (xp.reshape(b, t, d), xs.reshape(bs, ts, d), jnp.stack(nsa_kv_p), jnp.stack(nsa_kv_s),
            jnp.stack(nsa_win_p), jnp.stack(nsa_win_s), jnp.stack(moba_kv_p), jnp.stack(moba_kv_s),
            jnp.stack(sb_kv_p), jnp.stack(sb_kv_s))
```

```python
import functools
import math

import jax
import jax.numpy as jnp
from jax import lax
from jax.experimental import pallas as pl
from jax.experimental.pallas import tpu as pltpu

N_HEADS = 16
HEAD_DIM = 128
KV_HEADS = 4
GROUP = N_HEADS // KV_HEADS
CMP_BLOCK = 64
N_SELECT = 16
WINDOW = 512
MOBA_BLOCK = 256
MOBA_TOPK = 3
N_MIXERS = 3
ROPE_THETA = 10000.0
RMS_EPS = 1e-6
NEG_INF = -1e30
REMOVED = -3e38
SCALE = HEAD_DIM ** -0.5

VMEM_LIMIT = 56 * 1024 * 1024
F32 = jnp.float32
BF16 = jnp.bfloat16


def _cparams(sem):
    return pltpu.CompilerParams(dimension_semantics=sem, vmem_limit_bytes=VMEM_LIMIT)


def _dot(a, b):
    return jnp.dot(a, b, preferred_element_type=F32)


def _dot_nt(a, b):
    return lax.dot_general(a, b, (((1,), (1,)), ((), ())), preferred_element_type=F32)


def _iota(shape, dim):
    return lax.broadcasted_iota(jnp.int32, shape, dim)


def _masked_softmax(s, mask):
    s = jnp.where(mask, s, NEG_INF)
    m = jnp.max(s, axis=-1, keepdims=True)
    e = jnp.exp(s - m)
    p = e * (1.0 / jnp.sum(e, axis=-1, keepdims=True))
    return jnp.where(mask, p, 0.0)


def _attend(q, k, v, mask):
    s = jnp.where(mask, _dot_nt(q, k) * SCALE, NEG_INF)
    e = jnp.exp(s - jnp.max(s, axis=-1, keepdims=True))
    return _dot(e.astype(BF16), v) * (1.0 / jnp.sum(e, axis=-1, keepdims=True))


def _head_norm_rope(x, gain, cos, sin):
    y = x * lax.rsqrt(jnp.mean(x * x, axis=-1, keepdims=True) + RMS_EPS) * gain
    return y * cos + pltpu.roll(y, HEAD_DIM // 2, 1) * sin


def _rope_tables(pos):
    half = HEAD_DIM // 2
    inv_freq = ROPE_THETA ** (-jnp.arange(half, dtype=F32) / half)
    ang = pos.astype(F32)[:, None] * inv_freq[None, :]
    cos, sin = jnp.cos(ang), jnp.sin(ang)
    return jnp.concatenate([cos, cos], axis=-1), jnp.concatenate([-sin, sin], axis=-1)


def _rank_select(val, cand, k, axis):
    n = val.shape[axis]
    idx = _iota(val.shape, axis)
    v = jnp.where(cand, val, NEG_INF)
    rank = jnp.zeros(val.shape, jnp.int32)
    for j in range(n):
        vj = lax.slice_in_dim(v, j, j + 1, axis=axis)
        ahead = (vj > v) | ((vj == v) & (idx > j))
        rank = rank + jnp.where(ahead, 1, 0)
    return cand & (rank < k)


def _topk_indices(val, k):
    rows, n = val.shape
    lane = _iota((rows, n), 1).astype(F32)
    out_lane = _iota((rows, 128), 1)
    out = jnp.zeros((rows, 128), F32)
    for i in range(k):
        m = jnp.max(val, axis=-1, keepdims=True)
        pick = jnp.min(jnp.where(val == m, lane, float(n)), axis=-1, keepdims=True)
        out = jnp.where(out_lane == i, pick, out)
        val = jnp.where(lane == pick, REMOVED, val)
    return out.astype(jnp.int32)


def _double_buffer(step, n_steps, copies):
    slot = step % 2

    @pl.when(step == 0)
    def _():
        for c in copies(step, slot):
            c.start()

    @pl.when(step + 1 < n_steps)
    def _():
        for c in copies(step + 1, 1 - slot):
            c.start()

    for c in copies(step, slot):
        c.wait()
    return slot


def _proj_kernel(x_ref, g_ref, w_ref, hg_ref, cos_ref, sin_ref, *refs, tiles, n_out):
    out_refs, h_scr = refs[:n_out], refs[n_out]
    j = pl.program_id(1)

    @pl.when(j == 0)
    def _():
        x = x_ref[...]
        h = x * lax.rsqrt(jnp.mean(x * x, axis=-1, keepdims=True) + RMS_EPS) * g_ref[...]
        h_scr[...] = h.astype(BF16)

    acc = _dot(h_scr[...], w_ref[...])
    for jt, (mode, oi) in enumerate(tiles):
        @pl.when(j == jt)
        def _(mode=mode, oi=oi):
            o_ref = out_refs[oi]
            if mode == "rope":
                cos, sin, gain = cos_ref[...], sin_ref[...], hg_ref[0]
                for hh in range(acc.shape[1] // HEAD_DIM):
                    sl = slice(hh * HEAD_DIM, (hh + 1) * HEAD_DIM)
                    o_ref[:, sl] = _head_norm_rope(acc[:, sl], gain, cos, sin).astype(o_ref.dtype)
            elif mode == "sigmoid":
                o_ref[...] = (1.0 / (1.0 + jnp.exp(-acc))).astype(o_ref.dtype)
            else:
                o_ref[...] = acc.astype(o_ref.dtype)


def _norm_proj(x, gain, w, head_gains, cos, sin, tiles, outs, tm, tn=512):
    m, d = x.shape
    n_tiles = len(tiles)
    assert w.shape == (d, n_tiles * tn) and m % tm == 0
    starts = []
    for oi in range(len(outs)):
        starts.append(min(jt for jt, t in enumerate(tiles) if t[1] == oi))

    def out_map(oi):
        s, cnt = starts[oi], outs[oi][0]
        return lambda i, j: (i, jnp.clip(j - s, 0, cnt - 1))

    return pl.pallas_call(
        functools.partial(_proj_kernel, tiles=tuple(tiles), n_out=len(outs)),
        grid=(m // tm, n_tiles),
        in_specs=[
            pl.BlockSpec((tm, d), lambda i, j: (i, 0)),
            pl.BlockSpec((1, d), lambda i, j: (0, 0)),
            pl.BlockSpec((d, tn), lambda i, j: (0, j)),
            pl.BlockSpec((1, 1, HEAD_DIM), lambda i, j: (j, 0, 0)),
            pl.BlockSpec((tm, HEAD_DIM), lambda i, j: (i, 0)),
            pl.BlockSpec((tm, HEAD_DIM), lambda i, j: (i, 0)),
        ],
        out_specs=[pl.BlockSpec((tm, tn), out_map(oi)) for oi in range(len(outs))],
        out_shape=[jax.ShapeDtypeStruct((m, cnt * tn), dt) for cnt, dt in outs],
        scratch_shapes=[pltpu.VMEM((tm, d), BF16)],
        compiler_params=_cparams(("parallel", "arbitrary")),
        name="norm_proj",
    )(x, gain.reshape(1, d), w, head_gains, cos, sin)


def _out_proj_kernel(o_ref, w_ref, x_ref, y_ref):
    y_ref[...] = x_ref[...] + _dot(o_ref[...], w_ref[...])


def _out_proj(o, w, x, tm, tn=512):
    m, k = o.shape
    n = w.shape[1]
    return pl.pallas_call(
        _out_proj_kernel,
        grid=(m // tm, n // tn),
        in_specs=[
            pl.BlockSpec((tm, k), lambda i, j: (i, 0)),
            pl.BlockSpec((k, tn), lambda i, j: (0, j)),
            pl.BlockSpec((tm, tn), lambda i, j: (i, j)),
        ],
        out_specs=pl.BlockSpec((tm, tn), lambda i, j: (i, j)),
        out_shape=jax.ShapeDtypeStruct((m, n), F32),
        compiler_params=_cparams(("parallel", "arbitrary")),
        name="out_proj",
    )(o, w, x)


def _ffn_kernel(x_ref, g_ref, wg_ref, wu_ref, wd_ref, y_ref, h_scr, acc_scr):
    f = pl.program_id(1)

    @pl.when(f == 0)
    def _():
        x = x_ref[...]
        h = x * lax.rsqrt(jnp.mean(x * x, axis=-1, keepdims=True) + RMS_EPS) * g_ref[...]
        h_scr[...] = h.astype(BF16)
        acc_scr[...] = jnp.zeros_like(acc_scr)

    h = h_scr[...]
    a = _dot(h, wg_ref[...])
    u = _dot(h, wu_ref[...])
    act = (a * (1.0 / (1.0 + jnp.exp(-a))) * u).astype(BF16)
    acc_scr[...] += _dot(act, wd_ref[...])

    @pl.when(f == pl.num_programs(1) - 1)
    def _():
        y_ref[...] = x_ref[...] + acc_scr[...]


def _ffn(x, gain, wg, wu, wd, tm, tf=512):
    m, d = x.shape
    dff = wg.shape[1]
    assert dff % tf == 0 and m % tm == 0
    return pl.pallas_call(
        _ffn_kernel,
        grid=(m // tm, dff // tf),
        in_specs=[
            pl.BlockSpec((tm, d), lambda i, f: (i, 0)),
            pl.BlockSpec((1, d), lambda i, f: (0, 0)),
            pl.BlockSpec((d, tf), lambda i, f: (0, f)),
            pl.BlockSpec((d, tf), lambda i, f: (0, f)),
            pl.BlockSpec((tf, d), lambda i, f: (f, 0)),
        ],
        out_specs=pl.BlockSpec((tm, d), lambda i, f: (i, 0)),
        out_shape=jax.ShapeDtypeStruct((m, d), F32),
        scratch_shapes=[pltpu.VMEM((tm, d), BF16), pltpu.VMEM((tm, d), F32)],
        compiler_params=_cparams(("parallel", "arbitrary")),
        name="ffn",
    )(x, gain.reshape(1, d), wg, wu, wd)


def _gelu_tanh(x):
    return 0.5 * x * (1.0 + jnp.tanh(math.sqrt(2.0 / math.pi) * (x + 0.044715 * (x * x * x))))


def _compress_slab(load, pos_ref, w1_ref, w2_ref, gain_ref, cos_ref, sin_ref, ok_ref, ov_ref, nblk):
    jc = 8

    for s, o_ref in ((0, ok_ref), (1, ov_ref)):
        def body(jj, acc, s=s):
            pieces = []
            for g in range(KV_HEADS):
                cols = []
                for jo in range(jc):
                    j = jj * jc + jo
                    xj = load(s * KV_HEADS + g, j)
                    cols.append((xj + pos_ref[s, pl.ds(j, 1), :]).astype(BF16))
                pieces.append(jnp.concatenate(cols, axis=1))
            lhs = jnp.concatenate(pieces, axis=0)
            k0 = pl.multiple_of(jj * (jc * HEAD_DIM), jc * HEAD_DIM)
            return acc + _dot(lhs, w1_ref[s, pl.ds(k0, jc * HEAD_DIM), :])

        hidden = lax.fori_loop(0, CMP_BLOCK // jc, body, jnp.zeros((KV_HEADS * nblk, w1_ref.shape[2]), F32))
        y = _dot(_gelu_tanh(hidden).astype(BF16), w2_ref[s])
        for g in range(KV_HEADS):
            yg = y[g * nblk:(g + 1) * nblk]
            if s == 0:
                yg = _head_norm_rope(yg, gain_ref[...], cos_ref[...], sin_ref[...])
            o_ref[g] = yg.astype(o_ref.dtype)


def _compress_prompt_kernel(*refs, nblk):
    n_col = 2 * KV_HEADS
    load = lambda c, j: refs[c][pl.ds(j, nblk, stride=CMP_BLOCK), :]
    _compress_slab(load, *refs[n_col:], nblk)


def _compress_prompt(rows, b, t, pos_emb, w1, w2, kc_gain, cos_c, sin_c):
    nc = t // CMP_BLOCK
    n_col = 2 * KV_HEADS
    out = jax.ShapeDtypeStruct((b, KV_HEADS, nc, HEAD_DIM), BF16)
    return pl.pallas_call(
        functools.partial(_compress_prompt_kernel, nblk=nc),
        grid=(b,),
        in_specs=[pl.BlockSpec((t, HEAD_DIM), lambda i, c=c: (i, c)) for c in range(n_col)] + [
            pl.BlockSpec(pos_emb.shape, lambda i: (0, 0, 0)),
            pl.BlockSpec(w1.shape, lambda i: (0, 0, 0)),
            pl.BlockSpec(w2.shape, lambda i: (0, 0, 0)),
            pl.BlockSpec((1, HEAD_DIM), lambda i: (0, 0)),
            pl.BlockSpec((nc, HEAD_DIM), lambda i: (0, 0)),
            pl.BlockSpec((nc, HEAD_DIM), lambda i: (0, 0)),
        ],
        out_specs=[pl.BlockSpec((None, KV_HEADS, nc, HEAD_DIM), lambda i: (i, 0, 0, 0))] * 2,
        out_shape=[out, out],
        compiler_params=_cparams(("arbitrary",)),
        name="nsa_compress_prompt",
    )(*([rows] * n_col), pos_emb, w1, w2, kc_gain.reshape(1, HEAD_DIM), cos_c, sin_c)


def _compress_sample_kernel(pt_ref, cache_ref, pos_ref, w1_ref, w2_ref, gain_ref, cos_ref, sin_ref,
                            ok_ref, ov_ref, slab, sems, y_scr, *, layer, pages, n_seq, n_chunks):
    step = pl.program_id(0) * n_chunks + pl.program_id(1)
    n_col = slab.shape[1]
    psz = slab.shape[0] // (2 * pages)
    nblk = pages * psz // CMP_BLOCK
    m = nblk * n_col

    def copies(st, slot):
        b, c = st // n_chunks, st % n_chunks
        return [pltpu.make_async_copy(cache_ref.at[layer, pt_ref[b, c * pages + p], :, 0],
                                      slab.at[pl.ds((slot * pages + p) * psz, psz)], sems.at[slot, p])
                for p in range(pages)]

    slot = _double_buffer(step, n_seq * n_chunks, copies)
    base = slot * (pages * psz)
    jc = 8

    def hidden_for(s):
        def body(jj, acc):
            cols = []
            for jo in range(jc):
                j = jj * jc + jo
                xj = slab[pl.ds(base + j, nblk, stride=CMP_BLOCK)] + pos_ref[pl.ds(j, 1)]
                cols.append(xj.reshape(m, HEAD_DIM).astype(BF16))
            k0 = pl.multiple_of(jj * (jc * HEAD_DIM), jc * HEAD_DIM)
            return acc + _dot(jnp.concatenate(cols, axis=1), w1_ref[s, pl.ds(k0, jc * HEAD_DIM), :])
        return lax.fori_loop(0, CMP_BLOCK // jc, body, jnp.zeros((m, w1_ref.shape[2]), F32))

    is_key = (_iota((m, 1), 0) & (n_col - 1)) < KV_HEADS
    act = _gelu_tanh(jnp.where(is_key, hidden_for(0), hidden_for(1))).astype(BF16)
    y_scr[...] = jnp.where(is_key, _dot(act, w2_ref[0]), _dot(act, w2_ref[1]))
    for g in range(KV_HEADS):
        yk = y_scr[pl.ds(g, nblk, stride=n_col), :]
        ok_ref[g] = _head_norm_rope(yk, gain_ref[...], cos_ref[...], sin_ref[...]).astype(ok_ref.dtype)
        ov_ref[g] = y_scr[pl.ds(KV_HEADS + g, nblk, stride=n_col), :].astype(ov_ref.dtype)


def _compress_sample(cache, layer, page_table, pos_emb, w1, w2, kc_gain, cos_c, sin_c, pages=16):
    b, n_pages = page_table.shape
    psz = cache.shape[2]
    pages = min(pages, n_pages)
    assert n_pages % pages == 0 and psz % CMP_BLOCK == 0
    nblk = pages * psz // CMP_BLOCK
    nc = n_pages * psz // CMP_BLOCK
    n_col = 2 * KV_HEADS
    pos_rows = jnp.repeat(jnp.swapaxes(pos_emb, 0, 1), KV_HEADS, axis=1)
    out = jax.ShapeDtypeStruct((b, KV_HEADS, nc, HEAD_DIM), BF16)
    grid_spec = pltpu.PrefetchScalarGridSpec(
        num_scalar_prefetch=1,
        grid=(b, n_pages // pages),
        in_specs=[
            pl.BlockSpec(memory_space=pl.ANY),
            pl.BlockSpec(pos_rows.shape, lambda i, c, pt: (0, 0, 0)),
            pl.BlockSpec(w1.shape, lambda i, c, pt: (0, 0, 0)),
            pl.BlockSpec(w2.shape, lambda i, c, pt: (0, 0, 0)),
            pl.BlockSpec((1, HEAD_DIM), lambda i, c, pt: (0, 0)),
            pl.BlockSpec((nblk, HEAD_DIM), lambda i, c, pt: (c, 0)),
            pl.BlockSpec((nblk, HEAD_DIM), lambda i, c, pt: (c, 0)),
        ],
        out_specs=[pl.BlockSpec((None, KV_HEADS, nblk, HEAD_DIM), lambda i, c, pt: (i, 0, c, 0))] * 2,
        scratch_shapes=[pltpu.VMEM((2 * pages * psz, n_col, HEAD_DIM), F32), pltpu.SemaphoreType.DMA((2, pages)),
                        pltpu.VMEM((nblk * n_col, HEAD_DIM), F32)],
    )
    return pl.pallas_call(
        functools.partial(_compress_sample_kernel, layer=layer, pages=pages, n_seq=b, n_chunks=n_pages // pages),
        grid_spec=grid_spec,
        out_shape=[out, out],
        compiler_params=_cparams(("arbitrary", "arbitrary")),
        name="nsa_compress_sample",
    )(page_table, cache, pos_rows, w1, w2, kc_gain.reshape(1, HEAD_DIM), cos_c, sin_c)


def _nsa_prompt_kernel(q_ref, ks_ref, vs_ref, kw_ref, vw_ref, kc_ref, vc_ref, gt_ref, o_ref,
                       ks_s, vs_s, kw_s, vw_s, *, t_len, tq):
    qi = pl.program_id(2)

    @pl.when(qi == 0)
    def _():
        ks_s[...] = ks_ref[...].astype(BF16)
        vs_s[...] = vs_ref[...].astype(BF16)
        kw_s[...] = kw_ref[...].astype(BF16)
        vw_s[...] = vw_ref[...].astype(BF16)

    nc = kc_ref.shape[0]
    t0 = qi * tq
    tpos = t0 + _iota((tq, 1), 0)
    qb = q_ref[...]
    gt = gt_ref[...]
    kc, vc = kc_ref[...], vc_ref[...]
    heads = [qb[:, r * HEAD_DIM:(r + 1) * HEAD_DIM] for r in range(GROUP)]

    cmask = (_iota((tq, nc), 1) * CMP_BLOCK + (CMP_BLOCK - 1)) <= tpos
    tpos_t = t0 + _iota((nc, tq), 1)
    cblk_t = _iota((nc, tq), 0)
    cmask_t = (cblk_t * CMP_BLOCK + (CMP_BLOCK - 1)) <= tpos_t
    imp_t = jnp.zeros((nc, tq), F32)
    o_cmp = []
    for q_r in heads:
        p = _masked_softmax(_dot_nt(q_r, kc) * SCALE, cmask)
        o_cmp.append(_dot(p.astype(BF16), vc))
        s_t = jnp.where(cmask_t, _dot_nt(kc, q_r) * SCALE, NEG_INF)
        e_t = jnp.exp(s_t - jnp.max(s_t, axis=0, keepdims=True))
        imp_t = imp_t + jnp.where(cmask_t, e_t * (1.0 / jnp.sum(e_t, axis=0, keepdims=True)), 0.0)

    cur_t = tpos_t >> 6
    sel_t = _rank_select(imp_t, cblk_t < cur_t, N_SELECT - 1, axis=0) | (cblk_t == cur_t)
    sel_t = jnp.where(sel_t, 1.0, 0.0).astype(BF16)

    wn = min(WINDOW + tq, t_len)
    w0 = pl.multiple_of(jnp.maximum(t0 + tq - wn, 0), tq)
    wd = tpos - (w0 + _iota((tq, wn), 1))
    wmask = (wd >= 0) & (wd < WINDOW)
    kw, vw = kw_s[pl.ds(w0, wn), :], vw_s[pl.ds(w0, wn), :]
    o_win = [_attend(q_r, kw, vw, wmask) for q_r in heads]

    n_prefix = 4 if t_len % (4 * tq) == 0 else 1
    step = t_len // n_prefix
    for pi in range(n_prefix):
        @pl.when((t0 + tq - 1) // step == pi)
        def _(lk=(pi + 1) * step):
            expand = jnp.where((_iota((nc, lk), 1) >> 6) == _iota((nc, lk), 0), 1.0, 0.0).astype(BF16)
            picked = lax.dot_general(sel_t, expand, (((0,), (0,)), ((), ())), preferred_element_type=F32)
            smask = (picked > 0.5) & (_iota((tq, lk), 1) <= tpos)
            ks, vs = ks_s[0:lk, :], vs_s[0:lk, :]
            for r, q_r in enumerate(heads):
                o_slc = _attend(q_r, ks, vs, smask)
                o = (gt[:, 3 * r:3 * r + 1] * o_cmp[r] + gt[:, 3 * r + 1:3 * r + 2] * o_slc
                     + gt[:, 3 * r + 2:3 * r + 3] * o_win[r])
                o_ref[:, r * HEAD_DIM:(r + 1) * HEAD_DIM] = o.astype(o_ref.dtype)


def _nsa_prompt_attn(q, rows, win, k_cmp, v_cmp, gates, b, t, tq=128):
    nq = t // tq
    nc = t // CMP_BLOCK
    gw = GROUP * HEAD_DIM
    kv_spec = lambda col0: pl.BlockSpec((t, HEAD_DIM), lambda i, g, qi: (i, col0 + g))
    cmp_spec = pl.BlockSpec((None, None, nc, HEAD_DIM), lambda i, g, qi: (i, g, 0, 0))
    return pl.pallas_call(
        functools.partial(_nsa_prompt_kernel, t_len=t, tq=tq),
        grid=(b, KV_HEADS, nq),
        in_specs=[
            pl.BlockSpec((tq, gw), lambda i, g, qi: (i * nq + qi, g)),
            kv_spec(2 * KV_HEADS), kv_spec(3 * KV_HEADS), kv_spec(0), kv_spec(KV_HEADS),
            cmp_spec, cmp_spec,
            pl.BlockSpec((tq, HEAD_DIM), lambda i, g, qi: (i * nq + qi, g)),
        ],
        out_specs=pl.BlockSpec((tq, gw), lambda i, g, qi: (i * nq + qi, g)),
        out_shape=jax.ShapeDtypeStruct((b * t, N_HEADS * HEAD_DIM), BF16),
        scratch_shapes=[pltpu.VMEM((t, HEAD_DIM), BF16)] * 4,
        compiler_params=_cparams(("parallel", "parallel", "arbitrary")),
        name="nsa_prompt_attn",
    )(q, rows, rows, win, win, k_cmp, v_cmp, gates)


def _nsa_sample_cmp_kernel(q_ref, kc_ref, vc_ref, o_ref, idx_ref, *, pos):
    nc = kc_ref.shape[0]
    cblk = _iota((GROUP, nc), 1)
    cmask = (cblk * CMP_BLOCK + (CMP_BLOCK - 1)) <= pos
    p = _masked_softmax(_dot_nt(q_ref[...], kc_ref[...]) * SCALE, cmask)
    o_ref[...] = _dot(p.astype(BF16), vc_ref[...])
    imp = jnp.sum(p, axis=0, keepdims=True)
    cand = _iota((1, nc), 1) < (pos // CMP_BLOCK)
    idx_ref[...] = _topk_indices(jnp.where(cand, imp, NEG_INF), N_SELECT - 1)


def _nsa_sample_cmp(q4, k_cmp, v_cmp, pos):
    b = q4.shape[0]
    nc = k_cmp.shape[2]
    assert nc >= N_SELECT - 1 and pos // CMP_BLOCK >= N_SELECT - 1
    spec4 = pl.BlockSpec((None, None, GROUP, HEAD_DIM), lambda i, g: (i, g, 0, 0))
    cspec = pl.BlockSpec((None, None, nc, HEAD_DIM), lambda i, g: (i, g, 0, 0))
    return pl.pallas_call(
        functools.partial(_nsa_sample_cmp_kernel, pos=pos),
        grid=(b, KV_HEADS),
        in_specs=[spec4, cspec, cspec],
        out_specs=[spec4, pl.BlockSpec((None, None, 1, 128), lambda i, g: (i, g, 0, 0))],
        out_shape=[jax.ShapeDtypeStruct((b, KV_HEADS, GROUP, HEAD_DIM), F32),
                   jax.ShapeDtypeStruct((b, KV_HEADS, 1, 128), jnp.int32)],
        compiler_params=_cparams(("parallel", "parallel")),
        name="nsa_sample_cmp",
    )(q4, k_cmp, v_cmp)


def _softmax_with_extra(s, s_new):
    m = jnp.maximum(jnp.max(s, axis=-1, keepdims=True), s_new)
    e, e_new = jnp.exp(s - m), jnp.exp(s_new - m)
    inv = 1.0 / (jnp.sum(e, axis=-1, keepdims=True) + e_new)
    return e * inv, e_new * inv


def _row_dot(q, k_row):
    return jnp.sum(q.astype(F32) * k_row.astype(BF16).astype(F32), axis=-1, keepdims=True)


def _nsa_sample_slc_kernel(pt_ref, idx_ref, q_ref, cache_ref, knew_ref, vnew_ref,
                           win_ref, kwnew_ref, vwnew_ref, ocmp_ref, gt_ref, o_ref,
                           slab, sems, *, layer, n_sel, halves, n_seq):
    g = pl.program_id(1)
    step = pl.program_id(0) * KV_HEADS + g
    per_row = 4 * KV_HEADS
    kv_rows = 2 * KV_HEADS
    rows = slab.shape[0] // 2
    blk_rows = CMP_BLOCK * per_row

    def copies(st, slot):
        b, gg = st // KV_HEADS, st % KV_HEADS
        out = []
        for n in range(n_sel):
            blk = idx_ref[b, gg * n_sel + n]
            src = cache_ref.at[layer, pt_ref[b, blk // halves] * halves + blk % halves]
            out.append(pltpu.make_async_copy(src, slab.at[pl.ds((slot * n_sel + n) * blk_rows, blk_rows)],
                                             sems.at[slot, n]))
        return out

    slot = _double_buffer(step, n_seq * KV_HEADS, copies)
    base = slot * rows
    n_keys = n_sel * CMP_BLOCK
    q = q_ref[...]
    k_sel = slab[pl.ds(base + 2 * KV_HEADS + g, n_keys, stride=per_row), :].astype(BF16)
    v_sel = slab[pl.ds(base + 3 * KV_HEADS + g, n_keys, stride=per_row), :].astype(BF16)
    p, p_new = _softmax_with_extra(_dot_nt(q, k_sel) * SCALE, _row_dot(q, knew_ref[...]) * SCALE)
    o_slc = _dot(p.astype(BF16), v_sel) + p_new.astype(BF16).astype(F32) * vnew_ref[...].astype(BF16).astype(F32)

    nw = win_ref.shape[0] // kv_rows
    kwin = win_ref[pl.ds(g, nw, stride=kv_rows), :].astype(BF16)
    vwin = win_ref[pl.ds(KV_HEADS + g, nw, stride=kv_rows), :].astype(BF16)
    sw = jnp.where(nw - _iota((GROUP, nw), 1) < WINDOW, _dot_nt(q, kwin) * SCALE, NEG_INF)
    p, p_new = _softmax_with_extra(sw, _row_dot(q, kwnew_ref[...]) * SCALE)
    o_win = _dot(p.astype(BF16), vwin) + p_new.astype(BF16).astype(F32) * vwnew_ref[...].astype(BF16).astype(F32)

    gt = gt_ref[...]
    rowi = _iota((GROUP, 1), 0)
    gcol = [sum(jnp.where(rowi == r, gt[:, 3 * r + c:3 * r + c + 1], 0.0) for r in range(GROUP)) for c in range(3)]
    o_ref[...] = gcol[0] * ocmp_ref[...] + gcol[1] * o_slc + gcol[2] * o_win


def _nsa_sample_slc(cache, layer, page_table, idx, q4, rows_s, win_buf, win_layer, win_s, o_cmp, gates_s, halves):
    b = q4.shape[0]
    n_sel = idx.shape[1] // KV_HEADS
    kv_rows = 2 * KV_HEADS
    spec4 = pl.BlockSpec((None, None, GROUP, HEAD_DIM), lambda i, g, pt, ix: (i, g, 0, 0))
    new_spec = lambda col0: pl.BlockSpec((None, None, 1, HEAD_DIM), lambda i, g, pt, ix: (i, col0 + g, 0, 0))
    grid_spec = pltpu.PrefetchScalarGridSpec(
        num_scalar_prefetch=2,
        grid=(b, KV_HEADS),
        in_specs=[
            spec4,
            pl.BlockSpec(memory_space=pl.ANY),
            new_spec(2 * KV_HEADS), new_spec(3 * KV_HEADS),
            pl.BlockSpec((None, None) + win_buf.shape[2:], lambda i, g, pt, ix: (win_layer, i, 0, 0)),
            new_spec(0), new_spec(KV_HEADS),
            spec4,
            pl.BlockSpec((None, None, 1, HEAD_DIM), lambda i, g, pt, ix: (i, g, 0, 0)),
        ],
        out_specs=spec4,
        scratch_shapes=[pltpu.VMEM((2 * n_sel * cache.shape[2], HEAD_DIM), F32),
                        pltpu.SemaphoreType.DMA((2, n_sel))],
    )
    return pl.pallas_call(
        functools.partial(_nsa_sample_slc_kernel, layer=layer, n_sel=n_sel, halves=halves, n_seq=b),
        grid_spec=grid_spec,
        out_shape=jax.ShapeDtypeStruct((b, KV_HEADS, GROUP, HEAD_DIM), F32),
        compiler_params=_cparams(("arbitrary", "arbitrary")),
        name="nsa_sample_slc",
    )(page_table, idx, q4, cache, rows_s, rows_s, win_buf, win_s, win_s, o_cmp, gates_s)


def _moba_prompt_kernel(q_ref, k_ref, v_ref, o_ref, k_s, v_s, km_s, *, t_len, tq):
    qi = pl.program_id(2)
    nb = km_s.shape[0]

    @pl.when(qi == 0)
    def _():
        k = k_ref[...]
        k_s[...] = k.astype(BF16)
        v_s[...] = v_ref[...].astype(BF16)
        km_s[...] = (jnp.sum(k.reshape(nb, MOBA_BLOCK, HEAD_DIM), axis=1) * (1.0 / MOBA_BLOCK)).astype(BF16)

    t0 = qi * tq
    tpos = t0 + _iota((tq, 1), 0)
    cur_t = (t0 + _iota((nb, tq), 1)) >> 8
    blk_t = _iota((nb, tq), 0)
    qb = q_ref[...]
    km = km_s[...]
    heads = [qb[:, r * HEAD_DIM:(r + 1) * HEAD_DIM] for r in range(GROUP)]
    sel_t = []
    for q_r in heads:
        gate_t = _dot_nt(km, q_r)
        sel = _rank_select(gate_t, blk_t < cur_t, MOBA_TOPK, axis=0) | (blk_t == cur_t)
        sel_t.append(jnp.where(sel, 1.0, 0.0).astype(BF16))

    n_prefix = 4 if t_len % (4 * MOBA_BLOCK) == 0 else 1
    step = t_len // n_prefix
    for pi in range(n_prefix):
        @pl.when((t0 + tq - 1) // step == pi)
        def _(lk=(pi + 1) * step):
            expand = jnp.where((_iota((nb, lk), 1) >> 8) == _iota((nb, lk), 0), 1.0, 0.0).astype(BF16)
            causal = _iota((tq, lk), 1) <= tpos
            k, v = k_s[0:lk, :], v_s[0:lk, :]
            for r, q_r in enumerate(heads):
                picked = lax.dot_general(sel_t[r], expand, (((0,), (0,)), ((), ())), preferred_element_type=F32)
                o_ref[:, r * HEAD_DIM:(r + 1) * HEAD_DIM] = _attend(q_r, k, v, (picked > 0.5) & causal).astype(o_ref.dtype)


def _moba_prompt_attn(q, kv, b, t, tq=128):
    assert t % MOBA_BLOCK == 0
    nq = t // tq
    gw = GROUP * HEAD_DIM
    kv_spec = lambda col0: pl.BlockSpec((t, HEAD_DIM), lambda i, g, qi: (i, col0 + g))
    return pl.pallas_call(
        functools.partial(_moba_prompt_kernel, t_len=t, tq=tq),
        grid=(b, KV_HEADS, nq),
        in_specs=[pl.BlockSpec((tq, gw), lambda i, g, qi: (i * nq + qi, g)), kv_spec(0), kv_spec(KV_HEADS)],
        out_specs=pl.BlockSpec((tq, gw), lambda i, g, qi: (i * nq + qi, g)),
        out_shape=jax.ShapeDtypeStruct((b * t, N_HEADS * HEAD_DIM), BF16),
        scratch_shapes=[pltpu.VMEM((t, HEAD_DIM), BF16), pltpu.VMEM((t, HEAD_DIM), BF16),
                        pltpu.VMEM((t // MOBA_BLOCK, HEAD_DIM), BF16)],
        compiler_params=_cparams(("parallel", "parallel", "arbitrary")),
        name="moba_prompt_attn",
    )(q, kv, kv)


def _moba_sample_gate_kernel(pt_ref, q_ref, cache_ref, idx_ref, slab, sems, ksum, *, layer, pages, ppb, n_seq, n_chunks):
    c = pl.program_id(1)
    step = pl.program_id(0) * n_chunks + c
    per_row = 2 * KV_HEADS
    rows = slab.shape[0] // 2
    page_rows = rows // pages

    def copies(st, slot):
        b, cc = st // n_chunks, st % n_chunks
        return [pltpu.make_async_copy(cache_ref.at[layer, pt_ref[b, cc * pages + p]],
                                      slab.at[pl.ds((slot * pages + p) * page_rows, page_rows)], sems.at[slot, p])
                for p in range(pages)]

    slot = _double_buffer(step, n_seq * n_chunks, copies)
    x = slab[pl.ds(pl.multiple_of(slot * rows, rows), rows), :]
    n_blk = pages // ppb
    sums = jnp.sum(x.reshape(n_blk, rows // (n_blk * per_row), per_row, HEAD_DIM), axis=1)
    ksum[pl.ds(pl.multiple_of(c * (n_blk * per_row), n_blk * per_row), n_blk * per_row), :] = sums.reshape(
        n_blk * per_row, HEAD_DIM)

    @pl.when(c == n_chunks - 1)
    def _():
        nb = ksum.shape[0] // per_row
        for g in range(KV_HEADS):
            km = (ksum[pl.ds(g, nb, stride=per_row), :] * (1.0 / MOBA_BLOCK)).astype(BF16)
            idx_ref[g] = _topk_indices(_dot_nt(q_ref[g], km), MOBA_TOPK)


def _moba_sample_gate(cache, layer, page_table, q4, pages=16):
    b, n_pages = page_table.shape
    per_row = 2 * KV_HEADS
    page_rows = cache.shape[2]
    psz = page_rows // per_row
    ppb = MOBA_BLOCK // psz
    nb = n_pages // ppb
    pages = min(pages, n_pages)
    assert MOBA_BLOCK % psz == 0 and n_pages % pages == 0 and pages % ppb == 0 and nb >= MOBA_TOPK
    grid_spec = pltpu.PrefetchScalarGridSpec(
        num_scalar_prefetch=1,
        grid=(b, n_pages // pages),
        in_specs=[
            pl.BlockSpec((None, KV_HEADS, GROUP, HEAD_DIM), lambda i, c, pt: (i, 0, 0, 0)),
            pl.BlockSpec(memory_space=pl.ANY),
        ],
        out_specs=pl.BlockSpec((None, KV_HEADS, GROUP, 128), lambda i, c, pt: (i, 0, 0, 0)),
        scratch_shapes=[pltpu.VMEM((2 * pages * page_rows, HEAD_DIM), F32), pltpu.SemaphoreType.DMA((2, pages)),
                        pltpu.VMEM((nb * per_row, HEAD_DIM), F32)],
    )
    return pl.pallas_call(
        functools.partial(_moba_sample_gate_kernel, layer=layer, pages=pages, ppb=ppb, n_seq=b,
                          n_chunks=n_pages // pages),
        grid_spec=grid_spec,
        out_shape=jax.ShapeDtypeStruct((b, KV_HEADS, GROUP, 128), jnp.int32),
        compiler_params=_cparams(("arbitrary", "arbitrary")),
        name="moba_sample_gate",
    )(page_table, q4, cache)


def _moba_sample_attn_kernel(pt_ref, idx_ref, q_ref, cache_ref, knew_ref, vnew_ref, o_ref, slab, sems,
                             *, layer, n_sel, ppb, n_seq):
    h = pl.program_id(1)
    step = pl.program_id(0) * N_HEADS + h
    per_row = 2 * KV_HEADS
    n_pg = n_sel * ppb
    rows = slab.shape[0] // 2
    page_rows = rows // n_pg

    def copies(st, slot):
        b, hh = st // N_HEADS, st % N_HEADS
        out = []
        for n in range(n_pg):
            page = pt_ref[b, idx_ref[b, hh * n_sel + n // ppb] * ppb + n % ppb]
            out.append(pltpu.make_async_copy(cache_ref.at[layer, page],
                                             slab.at[pl.ds((slot * n_pg + n) * page_rows, page_rows)],
                                             sems.at[slot, n]))
        return out

    slot = _double_buffer(step, n_seq * N_HEADS, copies)
    base = slot * rows
    g = h // GROUP
    n_keys = rows // per_row
    q = q_ref[...]
    k_sel = slab[pl.ds(base + g, n_keys, stride=per_row), :].astype(BF16)
    v_sel = slab[pl.ds(base + KV_HEADS + g, n_keys, stride=per_row), :].astype(BF16)
    p, p_new = _softmax_with_extra(_dot_nt(q, k_sel) * SCALE, _row_dot(q, knew_ref[...]) * SCALE)
    o = _dot(p.astype(BF16), v_sel) + p_new.astype(BF16).astype(F32) * vnew_ref[...].astype(BF16).astype(F32)
    o_ref[...] = o.astype(o_ref.dtype)


def _moba_sample_attn(cache, layer, page_table, idx, q16, kv_new):
    b = q16.shape[0]
    page_rows = cache.shape[2]
    ppb = MOBA_BLOCK // (page_rows // (2 * KV_HEADS))
    n_sel = idx.shape[1] // N_HEADS
    n_pg = n_sel * ppb
    hspec = pl.BlockSpec((None, None, 1, HEAD_DIM), lambda i, h, pt, ix: (i, h, 0, 0))
    new_spec = lambda col0: pl.BlockSpec((None, None, 1, HEAD_DIM), lambda i, h, pt, ix: (i, col0 + h // GROUP, 0, 0))
    grid_spec = pltpu.PrefetchScalarGridSpec(
        num_scalar_prefetch=2,
        grid=(b, N_HEADS),
        in_specs=[hspec, pl.BlockSpec(memory_space=pl.ANY), new_spec(0), new_spec(KV_HEADS)],
        out_specs=hspec,
        scratch_shapes=[pltpu.VMEM((2 * n_pg * page_rows, HEAD_DIM), F32), pltpu.SemaphoreType.DMA((2, n_pg))],
    )
    return pl.pallas_call(
        functools.partial(_moba_sample_attn_kernel, layer=layer, n_sel=n_sel, ppb=ppb, n_seq=b),
        grid_spec=grid_spec,
        out_shape=jax.ShapeDtypeStruct((b, N_HEADS, 1, HEAD_DIM), BF16),
        compiler_params=_cparams(("arbitrary", "arbitrary")),
        name="moba_sample_attn",
    )(page_table, idx, q16, cache, kv_new, kv_new)


def _log_sigmoid(z):
    return jnp.minimum(z, 0.0) - jnp.log(1.0 + jnp.exp(-jnp.abs(z)))


def _split2(x):
    hi = x.astype(BF16)
    return hi, (x - hi.astype(F32)).astype(BF16)


EXP_UNDERFLOW = -104.0


def _sb_prompt_kernel(q_ref, k_ref, v_ref, o_ref, k_s, v_s, *, tq):
    qi = pl.program_id(2)

    @pl.when(qi == 0)
    def _():
        k_s[...] = k_ref[...].astype(BF16)
        v_s[...] = v_ref[...].astype(BF16)

    q = q_ref[...].astype(BF16)
    below = _iota((tq, tq), 0) > _iota((tq, tq), 1)
    after = jnp.where(below, 1.0, 0.0).astype(BF16)

    def chunk(c, run, acc, diagonal):
        s0 = pl.multiple_of(c * tq, tq)
        z = _dot_nt(q, k_s[pl.ds(s0, tq), :]) * SCALE
        log_beta = _log_sigmoid(z)
        log_keep = log_beta - z
        if diagonal:
            log_keep = jnp.where(below, log_keep, 0.0)
        hi, lo = _split2(log_keep)
        within = _dot(hi, after) + _dot(lo, after)
        a = jnp.exp(log_beta + within + run)
        if diagonal:
            a = jnp.where(below, a, 0.0)
        acc = acc + _dot(a.astype(BF16), v_s[pl.ds(s0, tq), :])
        return run + within[:, 0:1] + log_keep[:, 0:1], acc

    run, acc = chunk(qi, jnp.zeros((tq, 1), F32), jnp.zeros((tq, HEAD_DIM), F32), True)

    def cond(carry):
        c, run, _ = carry
        return (c >= 0) & (jnp.max(run) >= EXP_UNDERFLOW)

    def body(carry):
        c, run, acc = carry
        run, acc = chunk(c, run, acc, False)
        return c - 1, run, acc

    _, _, acc = lax.while_loop(cond, body, (qi - 1, run, acc))
    o_ref[...] = acc.astype(o_ref.dtype)


def _sb_prompt_attn(qkv, b, t, tq=256):
    nq = t // tq
    return pl.pallas_call(
        functools.partial(_sb_prompt_kernel, tq=tq),
        grid=(b, N_HEADS, nq),
        in_specs=[
            pl.BlockSpec((tq, HEAD_DIM), lambda i, h, qi: (i * nq + qi, h)),
            pl.BlockSpec((t, HEAD_DIM), lambda i, h, qi: (i, N_HEADS + h)),
            pl.BlockSpec((t, HEAD_DIM), lambda i, h, qi: (i, 2 * N_HEADS + h)),
        ],
        out_specs=pl.BlockSpec((tq, HEAD_DIM), lambda i, h, qi: (i * nq + qi, h)),
        out_shape=jax.ShapeDtypeStruct((b * t, N_HEADS * HEAD_DIM), BF16),
        scratch_shapes=[pltpu.VMEM((t, HEAD_DIM), BF16)] * 2,
        compiler_params=_cparams(("parallel", "parallel", "arbitrary")),
        name="sb_prompt_attn",
    )(qkv, qkv, qkv)


def _sb_sample_kernel(pt_ref, qbd_ref, cache_ref, o_ref, buf, sems, acc_scr, *, layer, n_pages):
    b = pl.program_id(0)
    per_row = 2 * N_HEADS
    rows = buf.shape[0] // 2
    psz = rows // per_row

    def page_copy(p, slot):
        return pltpu.make_async_copy(cache_ref.at[layer, pt_ref[b, n_pages - 1 - p]],
                                     buf.at[pl.ds(slot * rows, rows)], sems.at[slot])

    acc_scr[...] = jnp.zeros_like(acc_scr)
    after = jnp.where(_iota((psz, psz), 1) > _iota((psz, psz), 0), 1.0, 0.0).astype(BF16)
    live = _iota((1, 128), 1) < N_HEADS
    page_copy(0, 0).start()

    def cond(carry):
        p, run = carry
        return (p < n_pages) & (jnp.max(jnp.where(live, run, NEG_INF)) >= EXP_UNDERFLOW)

    def body(carry):
        p, run = carry
        slot = p % 2
        page_copy(p, slot).wait()

        @pl.when(p + 1 < n_pages)
        def _():
            page_copy(p + 1, 1 - slot).start()

        base = slot * rows

        def heads_wide(first):
            return jnp.concatenate([buf[pl.ds(base + first + h, psz, stride=per_row), :].astype(BF16)
                                    for h in range(N_HEADS)], axis=1)

        z = _dot(heads_wide(0), qbd_ref[...]) * SCALE
        log_beta = _log_sigmoid(z)
        log_keep = log_beta - z
        hi, lo = _split2(log_keep)
        within = _dot(after, hi) + _dot(after, lo)
        a = jnp.exp(log_beta + within + run)
        acc_scr[...] += _dot(a.T.astype(BF16), heads_wide(N_HEADS))
        return p + 1, run + within[0:1, :] + log_keep[0:1, :]

    p_end, _ = lax.while_loop(cond, body, (0, jnp.zeros((1, 128), F32)))

    @pl.when(p_end < n_pages)
    def _():
        page_copy(p_end, p_end % 2).wait()

    acc = acc_scr[...]
    own = _iota(acc.shape, 0) == (_iota(acc.shape, 1) >> 7)
    o_ref[...] = jnp.sum(jnp.where(own, acc, 0.0), axis=0, keepdims=True)


def _sb_sample_attn(cache, layer, page_table, qbd):
    b, n_pages = page_table.shape
    aw = N_HEADS * HEAD_DIM
    grid_spec = pltpu.PrefetchScalarGridSpec(
        num_scalar_prefetch=1,
        grid=(b,),
        in_specs=[
            pl.BlockSpec((None, aw, 128), lambda i, pt: (i, 0, 0)),
            pl.BlockSpec(memory_space=pl.ANY),
        ],
        out_specs=pl.BlockSpec((None, 1, aw), lambda i, pt: (i, 0, 0)),
        scratch_shapes=[pltpu.VMEM((2 * cache.shape[2], HEAD_DIM), F32), pltpu.SemaphoreType.DMA((2,)),
                        pltpu.VMEM((128, aw), F32)],
    )
    return pl.pallas_call(
        functools.partial(_sb_sample_kernel, layer=layer, n_pages=n_pages),
        grid_spec=grid_spec,
        out_shape=jax.ShapeDtypeStruct((b, 1, aw), F32),
        compiler_params=_cparams(("arbitrary",)),
        name="sb_sample_attn",
    )(page_table, qbd, cache)


def _tile_rows(m):
    return 512 if m % 512 == 0 else m


def _nsa_weights(w_in, q_gain, ks_gain, kw_gain):
    d = w_in.shape[0]
    main = N_HEADS * HEAD_DIM + 6 * KV_HEADS * HEAD_DIM
    wg = w_in[:, main:].reshape(d, KV_HEADS, GROUP * 3)
    wg = jnp.pad(wg, ((0, 0), (0, 0), (0, HEAD_DIM - GROUP * 3))).reshape(d, KV_HEADS * HEAD_DIM)
    w = jnp.concatenate([w_in[:, :main], wg], axis=1).astype(BF16)
    tiles = [("rope", 0)] * 4 + [("plain", 1), ("plain", 1), ("rope", 1), ("plain", 1),
                                 ("rope", 2), ("plain", 2), ("sigmoid", 3)]
    ones = jnp.ones((HEAD_DIM,), F32)
    hg = jnp.stack([q_gain] * 4 + [ones, ones, ks_gain, ones, kw_gain, ones, ones])[:, None, :]
    outs = [(4, BF16), (4, F32), (2, F32), (1, F32)]
    return w, tiles, hg, outs


def _nsa_layer(xp, xs, b, t, cache, layer, page_table, win_all, norm_gain, w_in, w_out, q_gain, kc_gain,
               ks_gain, kw_gain, cmp_pos, cmp_w1, cmp_w2):
    bs = xs.shape[0]
    n_pages, psz = page_table.shape[1], cache.shape[2]
    past = n_pages * psz
    w, tiles, hg, outs = _nsa_weights(w_in, q_gain, ks_gain, kw_gain)
    w1, w2 = cmp_w1.astype(BF16), cmp_w2.astype(BF16)
    w_out = w_out.astype(BF16)

    cos_p, sin_p = _rope_tables(jnp.arange(t, dtype=jnp.int32))
    cos_pt, sin_pt = jnp.tile(cos_p, (b, 1)), jnp.tile(sin_p, (b, 1))
    q, rows, win, gates = _norm_proj(xp, norm_gain, w, hg, cos_pt, sin_pt, tiles, outs, _tile_rows(b * t))
    nc = t // CMP_BLOCK
    cos_c, sin_c = _rope_tables(jnp.arange(nc, dtype=jnp.int32) * CMP_BLOCK + (CMP_BLOCK - 1))
    k_cmp, v_cmp = _compress_prompt(rows, b, t, cmp_pos, w1, w2, kc_gain, cos_c, sin_c)
    o = _nsa_prompt_attn(q, rows, win, k_cmp, v_cmp, gates, b, t)
    xp = _out_proj(o, w_out, xp, _tile_rows(b * t))

    cos_s, sin_s = _rope_tables(jnp.full((bs,), past, jnp.int32))
    qs, rows_s, win_s, gates_s = _norm_proj(xs, norm_gain, w, hg, cos_s, sin_s, tiles, outs, bs)
    ncs = past // CMP_BLOCK
    cos_cs, sin_cs = _rope_tables(jnp.arange(ncs, dtype=jnp.int32) * CMP_BLOCK + (CMP_BLOCK - 1))
    n_layers, n_pool = cache.shape[:2]
    k_cmp_s, v_cmp_s = _compress_sample(cache.reshape(n_layers, n_pool, psz, 2, 2 * KV_HEADS, HEAD_DIM), layer,
                                        page_table, cmp_pos, w1, w2, kc_gain, cos_cs, sin_cs)
    q4 = qs.reshape(bs, KV_HEADS, GROUP, HEAD_DIM)
    o_cmp, idx = _nsa_sample_cmp(q4, k_cmp_s, v_cmp_s, past)
    idx = idx[:, :, 0, :N_SELECT - 1].reshape(bs, KV_HEADS * (N_SELECT - 1))
    halves = psz // CMP_BLOCK
    block_view = cache.reshape(n_layers, n_pool * halves, CMP_BLOCK * 4 * KV_HEADS, HEAD_DIM)
    nw = win_all.shape[2]
    win_view = win_all.reshape(win_all.shape[0], bs, nw * 2 * KV_HEADS, HEAD_DIM)
    o_s = _nsa_sample_slc(block_view, layer, page_table, idx, q4,
                          rows_s.reshape(bs, 4 * KV_HEADS, 1, HEAD_DIM), win_view, layer,
                          win_s.reshape(bs, 2 * KV_HEADS, 1, HEAD_DIM), o_cmp,
                          gates_s.reshape(bs, KV_HEADS, 1, HEAD_DIM), halves)
    xs = _out_proj(o_s.reshape(bs, N_HEADS * HEAD_DIM).astype(BF16), w_out, xs, bs)

    kv_p = rows.reshape(b, t, 4, KV_HEADS, HEAD_DIM)
    kv_s = rows_s.reshape(bs, 1, 4, KV_HEADS, HEAD_DIM)
    win_p = win.reshape(b, t, 2, KV_HEADS, HEAD_DIM)[:, -min(WINDOW, t):]
    win_new = jnp.concatenate([win_all[layer], win_s.reshape(bs, 1, 2, KV_HEADS, HEAD_DIM)], axis=1)[:, -nw:]
    return xp, xs, kv_p, kv_s, win_p, win_new


def _moba_layer(xp, xs, b, t, cache, layer, page_table, norm_gain, w_in, w_out, q_gain, k_gain):
    bs = xs.shape[0]
    n_pages, psz = page_table.shape[1], cache.shape[2]
    past = n_pages * psz
    w = w_in.astype(BF16)
    w_out = w_out.astype(BF16)
    tiles = [("rope", 0)] * 4 + [("rope", 1), ("plain", 1)]
    ones = jnp.ones((HEAD_DIM,), F32)
    hg = jnp.stack([q_gain] * 4 + [k_gain, ones])[:, None, :]
    outs = [(4, BF16), (2, F32)]

    cos_p, sin_p = _rope_tables(jnp.arange(t, dtype=jnp.int32))
    cos_pt, sin_pt = jnp.tile(cos_p, (b, 1)), jnp.tile(sin_p, (b, 1))
    q, kv = _norm_proj(xp, norm_gain, w, hg, cos_pt, sin_pt, tiles, outs, _tile_rows(b * t))
    o = _moba_prompt_attn(q, kv, b, t)
    xp = _out_proj(o, w_out, xp, _tile_rows(b * t))

    cos_s, sin_s = _rope_tables(jnp.full((bs,), past, jnp.int32))
    qs, kv_s = _norm_proj(xs, norm_gain, w, hg, cos_s, sin_s, tiles, outs, bs)
    cache2 = cache.reshape(cache.shape[0], cache.shape[1], psz * 2 * KV_HEADS, HEAD_DIM)
    idx = _moba_sample_gate(cache2, layer, page_table, qs.reshape(bs, KV_HEADS, GROUP, HEAD_DIM))
    idx = idx.reshape(bs, N_HEADS, 128)[:, :, :MOBA_TOPK].reshape(bs, N_HEADS * MOBA_TOPK)
    o_s = _moba_sample_attn(cache2, layer, page_table, idx, qs.reshape(bs, N_HEADS, 1, HEAD_DIM),
                            kv_s.reshape(bs, 2 * KV_HEADS, 1, HEAD_DIM))
    xs = _out_proj(o_s.reshape(bs, N_HEADS * HEAD_DIM), w_out, xs, bs)
    return (xp, xs, kv.reshape(b, t, 2, KV_HEADS, HEAD_DIM), kv_s.reshape(bs, 1, 2, KV_HEADS, HEAD_DIM))


def _sb_layer(xp, xs, b, t, cache, layer, page_table, norm_gain, w_in, w_out):
    bs = xs.shape[0]
    psz = cache.shape[2]
    aw = N_HEADS * HEAD_DIM
    w = w_in.astype(BF16)
    w_out = w_out.astype(BF16)
    n_tiles = w.shape[1] // 512
    tiles = [("plain", 0)] * n_tiles
    hg = jnp.ones((n_tiles, 1, HEAD_DIM), F32)
    outs = [(n_tiles, F32)]
    dummy = jnp.zeros((b * t, HEAD_DIM), F32)
    (qkv,) = _norm_proj(xp, norm_gain, w, hg, dummy, dummy, tiles, outs, _tile_rows(b * t))
    o = _sb_prompt_attn(qkv, b, t)
    xp = _out_proj(o, w_out, xp, _tile_rows(b * t))

    (qkv_s,) = _norm_proj(xs, norm_gain, w, hg, dummy[:bs], dummy[:bs], tiles, outs, bs)
    qs = qkv_s[:, :aw].astype(BF16).reshape(bs, N_HEADS, HEAD_DIM)
    eye = jnp.eye(N_HEADS, 128, dtype=BF16)
    qbd = (qs[:, :, :, None] * eye[None, :, None, :]).reshape(bs, aw, 128)
    cache2 = cache.reshape(cache.shape[0], cache.shape[1], psz * 2 * N_HEADS, HEAD_DIM)
    o_s = _sb_sample_attn(cache2, layer, page_table, qbd)
    xs = _out_proj(o_s.reshape(bs, aw).astype(BF16), w_out, xs, bs)
    return (xp, xs, qkv[:, aw:].reshape(b, t, 2, N_HEADS, HEAD_DIM), qkv_s[:, aw:].reshape(bs, 1, 2, N_HEADS, HEAD_DIM))


def kernel(x_prompt, x_sample, cache_nsa_kv, state_nsa_win, cache_moba_kv, cache_sb_kv, page_table, mix_norm, ffn_norm, nsa_w_in, nsa_w_out, nsa_q_gain, nsa_kc_gain, nsa_ks_gain, nsa_kw_gain, nsa_cmp_pos, nsa_cmp_w1, nsa_cmp_w2, moba_w_in, moba_w_out, moba_q_gain, moba_k_gain, sb_w_in, sb_w_out, ffn_w_gate, ffn_w_up, ffn_w_down):
    b, t, d = x_prompt.shape
    bs, ts, _ = x_sample.shape
    assert ts == 1
    depth = mix_norm.shape[0]
    xp = x_prompt.reshape(b * t, d)
    xs = x_sample.reshape(bs * ts, d)
    nsa_kv_p, nsa_kv_s, nsa_win_p, nsa_win_s = [], [], [], []
    moba_kv_p, moba_kv_s, sb_kv_p, sb_kv_s = [], [], [], []
    for i in range(depth):
        j = i // N_MIXERS
        if i % N_MIXERS == 0:
            xp, xs, kvp, kvs, wp, ws = _nsa_layer(
                xp, xs, b, t, cache_nsa_kv, j, page_table, state_nsa_win, mix_norm[i], nsa_w_in[j], nsa_w_out[j],
                nsa_q_gain[j], nsa_kc_gain[j], nsa_ks_gain[j], nsa_kw_gain[j], nsa_cmp_pos[j], nsa_cmp_w1[j],
                nsa_cmp_w2[j])
            nsa_kv_p.append(kvp)
            nsa_kv_s.append(kvs)
            nsa_win_p.append(wp)
            nsa_win_s.append(ws)
        elif i % N_MIXERS == 1:
            xp, xs, kvp, kvs = _moba_layer(xp, xs, b, t, cache_moba_kv, j, page_table, mix_norm[i], moba_w_in[j],
                                           moba_w_out[j], moba_q_gain[j], moba_k_gain[j])
            moba_kv_p.append(kvp)
            moba_kv_s.append(kvs)
        else:
            xp, xs, kvp, kvs = _sb_layer(xp, xs, b, t, cache_sb_kv, j, page_table, mix_norm[i], sb_w_in[j], sb_w_out[j])
            sb_kv_p.append(kvp)
            sb_kv_s.append(kvs)
        wg, wu, wd = ffn_w_gate[i].astype(BF16), ffn_w_up[i].astype(BF16), ffn_w_down[i].astype(BF16)
        xp = _ffn(xp, ffn_norm[i], wg, wu, wd, _tile_rows(b * t))
        xs = _ffn(xs, ffn_norm[i], wg, wu, wd, bs * ts)
    return (xp.reshape(b, t, d), xs.reshape(bs, ts, d), jnp.stack(nsa_kv_p), jnp.stack(nsa_kv_s),
            jnp.stack(nsa_win_p), jnp.stack(nsa_win_s), jnp.stack(moba_kv_p), jnp.stack(moba_kv_s),
            jnp.stack(sb_kv_p), jnp.stack(sb_kv_s))
```

```python
import functools
import math

import jax
import jax.numpy as jnp
from jax import lax
from jax.experimental import pallas as pl
from jax.experimental.pallas import tpu as pltpu

N_HEADS = 16
HEAD_DIM = 128
KV_HEADS = 4
GROUP = N_HEADS // KV_HEADS
CMP_BLOCK = 64
N_SELECT = 16
WINDOW = 512
MOBA_BLOCK = 256
MOBA_TOPK = 3
N_MIXERS = 3
ROPE_THETA = 10000.0
RMS_EPS = 1e-6
NEG_INF = -1e30
REMOVED = -3e38
SCALE = HEAD_DIM ** -0.5
LOG2_E = math.log2(math.e)

VMEM_LIMIT = 56 * 1024 * 1024
F32 = jnp.float32
BF16 = jnp.bfloat16


def _cparams(sem):
    return pltpu.CompilerParams(dimension_semantics=sem, vmem_limit_bytes=VMEM_LIMIT)


def _dot(a, b):
    return jnp.dot(a, b, preferred_element_type=F32)


def _dot_nt(a, b):
    return lax.dot_general(a, b, (((1,), (1,)), ((), ())), preferred_element_type=F32)


def _iota(shape, dim):
    return lax.broadcasted_iota(jnp.int32, shape, dim)


def _masked_softmax(s, mask):
    s = jnp.where(mask, s, NEG_INF)
    m = jnp.max(s, axis=-1, keepdims=True)
    e = jnp.exp(s - m)
    p = e * (1.0 / jnp.sum(e, axis=-1, keepdims=True))
    return jnp.where(mask, p, 0.0)


def _attend_group(q_all, k, v, mask):
    n = k.shape[0]
    m = mask.shape[-2]
    heads = q_all.shape[0] // m
    s = _dot_nt(q_all, k).reshape(heads, m, n) * (SCALE * LOG2_E)
    s = jnp.where(mask if mask.ndim == 3 else mask[None], s, NEG_INF)
    e = jnp.exp2(s - jnp.max(s, axis=-1, keepdims=True))
    inv = 1.0 / jnp.sum(e, axis=-1, keepdims=True)
    return _dot(e.reshape(heads * m, n).astype(BF16), v) * inv.reshape(heads * m, 1)


def _head_norm_rope(x, gain, cos, sin):
    y = x * lax.rsqrt(jnp.mean(x * x, axis=-1, keepdims=True) + RMS_EPS) * gain
    return y * cos + pltpu.roll(y, HEAD_DIM // 2, 1) * sin


def _rope_tables(pos):
    half = HEAD_DIM // 2
    inv_freq = ROPE_THETA ** (-jnp.arange(half, dtype=F32) / half)
    ang = pos.astype(F32)[:, None] * inv_freq[None, :]
    cos, sin = jnp.cos(ang), jnp.sin(ang)
    return jnp.concatenate([cos, cos], axis=-1), jnp.concatenate([-sin, sin], axis=-1)


def _rank_select(val, cand, k, axis):
    n = val.shape[axis]
    idx = _iota(val.shape, axis)
    v = jnp.where(cand, val, NEG_INF)
    rank = jnp.zeros(val.shape, jnp.int32)
    for j in range(n):
        vj = lax.slice_in_dim(v, j, j + 1, axis=axis)
        ahead = (vj > v) | ((vj == v) & (idx > j))
        rank = rank + jnp.where(ahead, 1, 0)
    return cand & (rank < k)


def _topk_indices(val, k):
    rows, n = val.shape
    lane = _iota((rows, n), 1).astype(F32)
    out_lane = _iota((rows, 128), 1)
    out = jnp.zeros((rows, 128), F32)
    for i in range(k):
        m = jnp.max(val, axis=-1, keepdims=True)
        pick = jnp.min(jnp.where(val == m, lane, float(n)), axis=-1, keepdims=True)
        out = jnp.where(out_lane == i, pick, out)
        val = jnp.where(lane == pick, REMOVED, val)
    return out.astype(jnp.int32)


def _double_buffer(step, n_steps, copies):
    slot = step % 2

    @pl.when(step == 0)
    def _():
        for c in copies(step, slot):
            c.start()

    @pl.when(step + 1 < n_steps)
    def _():
        for c in copies(step + 1, 1 - slot):
            c.start()

    for c in copies(step, slot):
        c.wait()
    return slot


def _proj_kernel(x_ref, g_ref, w_ref, hg_ref, cos_ref, sin_ref, *refs, tiles, n_out):
    out_refs, h_scr = refs[:n_out], refs[n_out]
    j = pl.program_id(1)

    @pl.when(j == 0)
    def _():
        x = x_ref[...]
        h = x * lax.rsqrt(jnp.mean(x * x, axis=-1, keepdims=True) + RMS_EPS) * g_ref[...]
        h_scr[...] = h.astype(BF16)

    acc = _dot(h_scr[...], w_ref[...])
    for jt, (mode, oi) in enumerate(tiles):
        @pl.when(j == jt)
        def _(mode=mode, oi=oi):
            o_ref = out_refs[oi]
            if mode == "rope":
                cos, sin, gain = cos_ref[...], sin_ref[...], hg_ref[0]
                for hh in range(acc.shape[1] // HEAD_DIM):
                    sl = slice(hh * HEAD_DIM, (hh + 1) * HEAD_DIM)
                    o_ref[:, sl] = _head_norm_rope(acc[:, sl], gain, cos, sin).astype(o_ref.dtype)
            elif mode == "sigmoid":
                o_ref[...] = (1.0 / (1.0 + jnp.exp(-acc))).astype(o_ref.dtype)
            else:
                o_ref[...] = acc.astype(o_ref.dtype)


def _norm_proj(x, gain, w, head_gains, cos, sin, tiles, outs, tm, tn=512):
    m, d = x.shape
    n_tiles = len(tiles)
    assert w.shape == (d, n_tiles * tn) and m % tm == 0
    starts = []
    for oi in range(len(outs)):
        starts.append(min(jt for jt, t in enumerate(tiles) if t[1] == oi))

    def out_map(oi):
        s, cnt = starts[oi], outs[oi][0]
        return lambda i, j: (i, jnp.clip(j - s, 0, cnt - 1))

    return pl.pallas_call(
        functools.partial(_proj_kernel, tiles=tuple(tiles), n_out=len(outs)),
        grid=(m // tm, n_tiles),
        in_specs=[
            pl.BlockSpec((tm, d), lambda i, j: (i, 0)),
            pl.BlockSpec((1, d), lambda i, j: (0, 0)),
            pl.BlockSpec((d, tn), lambda i, j: (0, j)),
            pl.BlockSpec((1, 1, HEAD_DIM), lambda i, j: (j, 0, 0)),
            pl.BlockSpec((tm, HEAD_DIM), lambda i, j: (i, 0)),
            pl.BlockSpec((tm, HEAD_DIM), lambda i, j: (i, 0)),
        ],
        out_specs=[pl.BlockSpec((tm, tn), out_map(oi)) for oi in range(len(outs))],
        out_shape=[jax.ShapeDtypeStruct((m, cnt * tn), dt) for cnt, dt in outs],
        scratch_shapes=[pltpu.VMEM((tm, d), BF16)],
        compiler_params=_cparams(("parallel", "arbitrary")),
        name="norm_proj",
    )(x, gain.reshape(1, d), w, head_gains, cos, sin)


def _out_proj_kernel(o_ref, w_ref, x_ref, y_ref):
    y_ref[...] = x_ref[...] + _dot(o_ref[...], w_ref[...])


def _out_proj(o, w, x, tm, tn=512):
    m, k = o.shape
    n = w.shape[1]
    return pl.pallas_call(
        _out_proj_kernel,
        grid=(m // tm, n // tn),
        in_specs=[
            pl.BlockSpec((tm, k), lambda i, j: (i, 0)),
            pl.BlockSpec((k, tn), lambda i, j: (0, j)),
            pl.BlockSpec((tm, tn), lambda i, j: (i, j)),
        ],
        out_specs=pl.BlockSpec((tm, tn), lambda i, j: (i, j)),
        out_shape=jax.ShapeDtypeStruct((m, n), F32),
        compiler_params=_cparams(("parallel", "arbitrary")),
        name="out_proj",
    )(o, w, x)


def _ffn_kernel(x_ref, g_ref, wg_ref, wu_ref, wd_ref, y_ref, h_scr, acc_scr):
    f = pl.program_id(1)

    @pl.when(f == 0)
    def _():
        x = x_ref[...]
        h = x * lax.rsqrt(jnp.mean(x * x, axis=-1, keepdims=True) + RMS_EPS) * g_ref[...]
        h_scr[...] = h.astype(BF16)
        acc_scr[...] = jnp.zeros_like(acc_scr)

    h = h_scr[...]
    a = _dot(h, wg_ref[...])
    u = _dot(h, wu_ref[...])
    act = (a * (1.0 / (1.0 + jnp.exp(-a))) * u).astype(BF16)
    acc_scr[...] += _dot(act, wd_ref[...])

    @pl.when(f == pl.num_programs(1) - 1)
    def _():
        y_ref[...] = x_ref[...] + acc_scr[...]


def _ffn(x, gain, wg, wu, wd, tm, tf=512):
    m, d = x.shape
    dff = wg.shape[1]
    assert dff % tf == 0 and m % tm == 0
    return pl.pallas_call(
        _ffn_kernel,
        grid=(m // tm, dff // tf),
        in_specs=[
            pl.BlockSpec((tm, d), lambda i, f: (i, 0)),
            pl.BlockSpec((1, d), lambda i, f: (0, 0)),
            pl.BlockSpec((d, tf), lambda i, f: (0, f)),
            pl.BlockSpec((d, tf), lambda i, f: (0, f)),
            pl.BlockSpec((tf, d), lambda i, f: (f, 0)),
        ],
        out_specs=pl.BlockSpec((tm, d), lambda i, f: (i, 0)),
        out_shape=jax.ShapeDtypeStruct((m, d), F32),
        scratch_shapes=[pltpu.VMEM((tm, d), BF16), pltpu.VMEM((tm, d), F32)],
        compiler_params=_cparams(("parallel", "arbitrary")),
        name="ffn",
    )(x, gain.reshape(1, d), wg, wu, wd)


def _gelu_tanh(x):
    return 0.5 * x * (1.0 + jnp.tanh(math.sqrt(2.0 / math.pi) * (x + 0.044715 * (x * x * x))))


def _compress_slab(load, pos_ref, w1_ref, w2_ref, gain_ref, cos_ref, sin_ref, ok_ref, ov_ref, nblk):
    jc = 8

    for s, o_ref in ((0, ok_ref), (1, ov_ref)):
        def body(jj, acc, s=s):
            pieces = []
            for g in range(KV_HEADS):
                cols = []
                for jo in range(jc):
                    j = jj * jc + jo
                    xj = load(s * KV_HEADS + g, j)
                    cols.append((xj + pos_ref[s, pl.ds(j, 1), :]).astype(BF16))
                pieces.append(jnp.concatenate(cols, axis=1))
            lhs = jnp.concatenate(pieces, axis=0)
            k0 = pl.multiple_of(jj * (jc * HEAD_DIM), jc * HEAD_DIM)
            return acc + _dot(lhs, w1_ref[s, pl.ds(k0, jc * HEAD_DIM), :])

        hidden = lax.fori_loop(0, CMP_BLOCK // jc, body, jnp.zeros((KV_HEADS * nblk, w1_ref.shape[2]), F32))
        y = _dot(_gelu_tanh(hidden).astype(BF16), w2_ref[s])
        for g in range(KV_HEADS):
            yg = y[g * nblk:(g + 1) * nblk]
            if s == 0:
                yg = _head_norm_rope(yg, gain_ref[...], cos_ref[...], sin_ref[...])
            o_ref[g] = yg.astype(o_ref.dtype)


def _compress_prompt_kernel(*refs, nblk):
    n_col = 2 * KV_HEADS
    load = lambda c, j: refs[c][pl.ds(j, nblk, stride=CMP_BLOCK), :]
    _compress_slab(load, *refs[n_col:], nblk)


def _compress_prompt(rows, b, t, pos_emb, w1, w2, kc_gain, cos_c, sin_c):
    nc = t // CMP_BLOCK
    n_col = 2 * KV_HEADS
    out = jax.ShapeDtypeStruct((b, KV_HEADS, nc, HEAD_DIM), BF16)
    return pl.pallas_call(
        functools.partial(_compress_prompt_kernel, nblk=nc),
        grid=(b,),
        in_specs=[pl.BlockSpec((t, HEAD_DIM), lambda i, c=c: (i, c)) for c in range(n_col)] + [
            pl.BlockSpec(pos_emb.shape, lambda i: (0, 0, 0)),
            pl.BlockSpec(w1.shape, lambda i: (0, 0, 0)),
            pl.BlockSpec(w2.shape, lambda i: (0, 0, 0)),
            pl.BlockSpec((1, HEAD_DIM), lambda i: (0, 0)),
            pl.BlockSpec((nc, HEAD_DIM), lambda i: (0, 0)),
            pl.BlockSpec((nc, HEAD_DIM), lambda i: (0, 0)),
        ],
        out_specs=[pl.BlockSpec((None, KV_HEADS, nc, HEAD_DIM), lambda i: (i, 0, 0, 0))] * 2,
        out_shape=[out, out],
        compiler_params=_cparams(("arbitrary",)),
        name="nsa_compress_prompt",
    )(*([rows] * n_col), pos_emb, w1, w2, kc_gain.reshape(1, HEAD_DIM), cos_c, sin_c)


def _compress_sample_kernel(pt_ref, cache_ref, pos_ref, w1_ref, w2_ref, gain_ref, cos_ref, sin_ref,
                            ok_ref, ov_ref, slab, sems, y_scr, *, layer, pages, n_seq, n_chunks):
    step = pl.program_id(0) * n_chunks + pl.program_id(1)
    n_col = slab.shape[1]
    psz = slab.shape[0] // (2 * pages)
    nblk = pages * psz // CMP_BLOCK
    m = nblk * n_col

    def copies(st, slot):
        b, c = st // n_chunks, st % n_chunks
        return [pltpu.make_async_copy(cache_ref.at[layer, pt_ref[b, c * pages + p], :, 0],
                                      slab.at[pl.ds((slot * pages + p) * psz, psz)], sems.at[slot, p])
                for p in range(pages)]

    slot = _double_buffer(step, n_seq * n_chunks, copies)
    base = slot * (pages * psz)
    jc = 8

    def hidden_for(s):
        def body(jj, acc):
            cols = []
            for jo in range(jc):
                j = jj * jc + jo
                xj = slab[pl.ds(base + j, nblk, stride=CMP_BLOCK)] + pos_ref[pl.ds(j, 1)]
                cols.append(xj.reshape(m, HEAD_DIM).astype(BF16))
            k0 = pl.multiple_of(jj * (jc * HEAD_DIM), jc * HEAD_DIM)
            return acc + _dot(jnp.concatenate(cols, axis=1), w1_ref[s, pl.ds(k0, jc * HEAD_DIM), :])
        return lax.fori_loop(0, CMP_BLOCK // jc, body, jnp.zeros((m, w1_ref.shape[2]), F32))

    is_key = (_iota((m, 1), 0) & (n_col - 1)) < KV_HEADS
    act = _gelu_tanh(jnp.where(is_key, hidden_for(0), hidden_for(1))).astype(BF16)
    y_scr[...] = jnp.where(is_key, _dot(act, w2_ref[0]), _dot(act, w2_ref[1]))
    for g in range(KV_HEADS):
        yk = y_scr[pl.ds(g, nblk, stride=n_col), :]
        ok_ref[g] = _head_norm_rope(yk, gain_ref[...], cos_ref[...], sin_ref[...]).astype(ok_ref.dtype)
        ov_ref[g] = y_scr[pl.ds(KV_HEADS + g, nblk, stride=n_col), :].astype(ov_ref.dtype)


def _compress_sample(cache, layer, page_table, pos_emb, w1, w2, kc_gain, cos_c, sin_c, pages=16):
    b, n_pages = page_table.shape
    psz = cache.shape[2]
    pages = min(pages, n_pages)
    assert n_pages % pages == 0 and psz % CMP_BLOCK == 0
    nblk = pages * psz // CMP_BLOCK
    nc = n_pages * psz // CMP_BLOCK
    n_col = 2 * KV_HEADS
    pos_rows = jnp.repeat(jnp.swapaxes(pos_emb, 0, 1), KV_HEADS, axis=1)
    out = jax.ShapeDtypeStruct((b, KV_HEADS, nc, HEAD_DIM), BF16)
    grid_spec = pltpu.PrefetchScalarGridSpec(
        num_scalar_prefetch=1,
        grid=(b, n_pages // pages),
        in_specs=[
            pl.BlockSpec(memory_space=pl.ANY),
            pl.BlockSpec(pos_rows.shape, lambda i, c, pt: (0, 0, 0)),
            pl.BlockSpec(w1.shape, lambda i, c, pt: (0, 0, 0)),
            pl.BlockSpec(w2.shape, lambda i, c, pt: (0, 0, 0)),
            pl.BlockSpec((1, HEAD_DIM), lambda i, c, pt: (0, 0)),
            pl.BlockSpec((nblk, HEAD_DIM), lambda i, c, pt: (c, 0)),
            pl.BlockSpec((nblk, HEAD_DIM), lambda i, c, pt: (c, 0)),
        ],
        out_specs=[pl.BlockSpec((None, KV_HEADS, nblk, HEAD_DIM), lambda i, c, pt: (i, 0, c, 0))] * 2,
        scratch_shapes=[pltpu.VMEM((2 * pages * psz, n_col, HEAD_DIM), F32), pltpu.SemaphoreType.DMA((2, pages)),
                        pltpu.VMEM((nblk * n_col, HEAD_DIM), F32)],
    )
    return pl.pallas_call(
        functools.partial(_compress_sample_kernel, layer=layer, pages=pages, n_seq=b, n_chunks=n_pages // pages),
        grid_spec=grid_spec,
        out_shape=[out, out],
        compiler_params=_cparams(("arbitrary", "arbitrary")),
        name="nsa_compress_sample",
    )(page_table, cache, pos_rows, w1, w2, kc_gain.reshape(1, HEAD_DIM), cos_c, sin_c)


def _nsa_prompt_kernel(q_ref, ks_ref, vs_ref, kw_ref, vw_ref, kc_ref, vc_ref, gt_ref, o_ref,
                       ks_s, vs_s, kw_s, vw_s, *, t_len, tq):
    qi = pl.program_id(2)

    @pl.when(qi == 0)
    def _():
        ks_s[...] = ks_ref[...].astype(BF16)
        vs_s[...] = vs_ref[...].astype(BF16)
        kw_s[...] = kw_ref[...].astype(BF16)
        vw_s[...] = vw_ref[...].astype(BF16)

    nc = kc_ref.shape[0]
    t0 = qi * tq
    tpos = t0 + _iota((tq, 1), 0)
    qb = q_ref[...]
    gt = gt_ref[...]
    kc, vc = kc_ref[...], vc_ref[...]
    q_all = jnp.concatenate([qb[:, r * HEAD_DIM:(r + 1) * HEAD_DIM] for r in range(GROUP)], axis=0)

    cmask = ((_iota((tq, nc), 1) * CMP_BLOCK + (CMP_BLOCK - 1)) <= tpos)[None]
    s_cmp = (_dot_nt(q_all, kc) * SCALE).reshape(GROUP, tq, nc)
    o_cmp = _dot(_masked_softmax(s_cmp, cmask).reshape(GROUP * tq, nc).astype(BF16), vc)
    tpos_t = t0 + _iota((nc, tq), 1)
    cblk_t = _iota((nc, tq), 0)
    cmask_t = (cblk_t * CMP_BLOCK + (CMP_BLOCK - 1)) <= tpos_t
    s_t_all = _dot_nt(kc, q_all) * SCALE
    imp_t = jnp.zeros((nc, tq), F32)
    for r in range(GROUP):
        s_t = jnp.where(cmask_t, s_t_all[:, r * tq:(r + 1) * tq], NEG_INF)
        e_t = jnp.exp(s_t - jnp.max(s_t, axis=0, keepdims=True))
        imp_t = imp_t + jnp.where(cmask_t, e_t * (1.0 / jnp.sum(e_t, axis=0, keepdims=True)), 0.0)

    cur_t = tpos_t >> 6
    sel_t = _rank_select(imp_t, cblk_t < cur_t, N_SELECT - 1, axis=0) | (cblk_t == cur_t)
    sel_t = jnp.where(sel_t, 1.0, 0.0).astype(BF16)

    wn = min(WINDOW + tq, t_len)
    w0 = pl.multiple_of(jnp.maximum(t0 + tq - wn, 0), tq)
    wd = tpos - (w0 + _iota((tq, wn), 1))
    wmask = (wd >= 0) & (wd < WINDOW)
    kw, vw = kw_s[pl.ds(w0, wn), :], vw_s[pl.ds(w0, wn), :]
    o_win = _attend_group(q_all, kw, vw, wmask)

    n_prefix = 4 if t_len % (4 * tq) == 0 else 1
    step = t_len // n_prefix
    for pi in range(n_prefix):
        @pl.when((t0 + tq - 1) // step == pi)
        def _(lk=(pi + 1) * step):
            expand = jnp.where((_iota((nc, lk), 1) >> 6) == _iota((nc, lk), 0), 1.0, 0.0).astype(BF16)
            picked = lax.dot_general(sel_t, expand, (((0,), (0,)), ((), ())), preferred_element_type=F32)
            smask = (picked > 0.5) & (_iota((tq, lk), 1) <= tpos)
            o_slc = _attend_group(q_all, ks_s[0:lk, :], vs_s[0:lk, :], smask)
            for r in range(GROUP):
                rows = slice(r * tq, (r + 1) * tq)
                o = (gt[:, 3 * r:3 * r + 1] * o_cmp[rows] + gt[:, 3 * r + 1:3 * r + 2] * o_slc[rows]
                     + gt[:, 3 * r + 2:3 * r + 3] * o_win[rows])
                o_ref[:, r * HEAD_DIM:(r + 1) * HEAD_DIM] = o.astype(o_ref.dtype)


def _nsa_prompt_attn(q, rows, win, k_cmp, v_cmp, gates, b, t, tq=128):
    nq = t // tq
    nc = t // CMP_BLOCK
    gw = GROUP * HEAD_DIM
    kv_spec = lambda col0: pl.BlockSpec((t, HEAD_DIM), lambda i, g, qi: (i, col0 + g))
    cmp_spec = pl.BlockSpec((None, None, nc, HEAD_DIM), lambda i, g, qi: (i, g, 0, 0))
    return pl.pallas_call(
        functools.partial(_nsa_prompt_kernel, t_len=t, tq=tq),
        grid=(b, KV_HEADS, nq),
        in_specs=[
            pl.BlockSpec((tq, gw), lambda i, g, qi: (i * nq + qi, g)),
            kv_spec(2 * KV_HEADS), kv_spec(3 * KV_HEADS), kv_spec(0), kv_spec(KV_HEADS),
            cmp_spec, cmp_spec,
            pl.BlockSpec((tq, HEAD_DIM), lambda i, g, qi: (i * nq + qi, g)),
        ],
        out_specs=pl.BlockSpec((tq, gw), lambda i, g, qi: (i * nq + qi, g)),
        out_shape=jax.ShapeDtypeStruct((b * t, N_HEADS * HEAD_DIM), BF16),
        scratch_shapes=[pltpu.VMEM((t, HEAD_DIM), BF16)] * 4,
        compiler_params=_cparams(("parallel", "parallel", "arbitrary")),
        name="nsa_prompt_attn",
    )(q, rows, rows, win, win, k_cmp, v_cmp, gates)


def _nsa_sample_cmp_kernel(q_ref, kc_ref, vc_ref, o_ref, idx_ref, *, pos):
    nc = kc_ref.shape[0]
    cblk = _iota((GROUP, nc), 1)
    cmask = (cblk * CMP_BLOCK + (CMP_BLOCK - 1)) <= pos
    p = _masked_softmax(_dot_nt(q_ref[...], kc_ref[...]) * SCALE, cmask)
    o_ref[...] = _dot(p.astype(BF16), vc_ref[...])
    imp = jnp.sum(p, axis=0, keepdims=True)
    cand = _iota((1, nc), 1) < (pos // CMP_BLOCK)
    idx_ref[...] = _topk_indices(jnp.where(cand, imp, NEG_INF), N_SELECT - 1)


def _nsa_sample_cmp(q4, k_cmp, v_cmp, pos):
    b = q4.shape[0]
    nc = k_cmp.shape[2]
    assert nc >= N_SELECT - 1 and pos // CMP_BLOCK >= N_SELECT - 1
    spec4 = pl.BlockSpec((None, None, GROUP, HEAD_DIM), lambda i, g: (i, g, 0, 0))
    cspec = pl.BlockSpec((None, None, nc, HEAD_DIM), lambda i, g: (i, g, 0, 0))
    return pl.pallas_call(
        functools.partial(_nsa_sample_cmp_kernel, pos=pos),
        grid=(b, KV_HEADS),
        in_specs=[spec4, cspec, cspec],
        out_specs=[spec4, pl.BlockSpec((None, None, 1, 128), lambda i, g: (i, g, 0, 0))],
        out_shape=[jax.ShapeDtypeStruct((b, KV_HEADS, GROUP, HEAD_DIM), F32),
                   jax.ShapeDtypeStruct((b, KV_HEADS, 1, 128), jnp.int32)],
        compiler_params=_cparams(("parallel", "parallel")),
        name="nsa_sample_cmp",
    )(q4, k_cmp, v_cmp)


def _softmax_with_extra(s, s_new):
    m = jnp.maximum(jnp.max(s, axis=-1, keepdims=True), s_new)
    e, e_new = jnp.exp(s - m), jnp.exp(s_new - m)
    inv = 1.0 / (jnp.sum(e, axis=-1, keepdims=True) + e_new)
    return e * inv, e_new * inv


def _row_dot(q, k_row):
    return jnp.sum(q.astype(F32) * k_row.astype(BF16).astype(F32), axis=-1, keepdims=True)


def _nsa_sample_slc_kernel(pt_ref, idx_ref, q_ref, cache_ref, knew_ref, vnew_ref,
                           win_ref, kwnew_ref, vwnew_ref, ocmp_ref, gt_ref, o_ref,
                           slab, sems, *, layer, n_sel, halves, n_seq):
    g = pl.program_id(1)
    step = pl.program_id(0) * KV_HEADS + g
    per_row = 4 * KV_HEADS
    kv_rows = 2 * KV_HEADS
    rows = slab.shape[0] // 2
    blk_rows = CMP_BLOCK * per_row

    def copies(st, slot):
        b, gg = st // KV_HEADS, st % KV_HEADS
        out = []
        for n in range(n_sel):
            blk = idx_ref[b, gg * n_sel + n]
            src = cache_ref.at[layer, pt_ref[b, blk // halves] * halves + blk % halves]
            out.append(pltpu.make_async_copy(src, slab.at[pl.ds((slot * n_sel + n) * blk_rows, blk_rows)],
                                             sems.at[slot, n]))
        return out

    slot = _double_buffer(step, n_seq * KV_HEADS, copies)
    base = slot * rows
    n_keys = n_sel * CMP_BLOCK
    q = q_ref[...]
    k_sel = slab[pl.ds(base + 2 * KV_HEADS + g, n_keys, stride=per_row), :].astype(BF16)
    v_sel = slab[pl.ds(base + 3 * KV_HEADS + g, n_keys, stride=per_row), :].astype(BF16)
    p, p_new = _softmax_with_extra(_dot_nt(q, k_sel) * SCALE, _row_dot(q, knew_ref[...]) * SCALE)
    o_slc = _dot(p.astype(BF16), v_sel) + p_new.astype(BF16).astype(F32) * vnew_ref[...].astype(BF16).astype(F32)

    nw = win_ref.shape[0] // kv_rows
    kwin = win_ref[pl.ds(g, nw, stride=kv_rows), :].astype(BF16)
    vwin = win_ref[pl.ds(KV_HEADS + g, nw, stride=kv_rows), :].astype(BF16)
    sw = jnp.where(nw - _iota((GROUP, nw), 1) < WINDOW, _dot_nt(q, kwin) * SCALE, NEG_INF)
    p, p_new = _softmax_with_extra(sw, _row_dot(q, kwnew_ref[...]) * SCALE)
    o_win = _dot(p.astype(BF16), vwin) + p_new.astype(BF16).astype(F32) * vwnew_ref[...].astype(BF16).astype(F32)

    gt = gt_ref[...]
    rowi = _iota((GROUP, 1), 0)
    gcol = [sum(jnp.where(rowi == r, gt[:, 3 * r + c:3 * r + c + 1], 0.0) for r in range(GROUP)) for c in range(3)]
    o_ref[...] = gcol[0] * ocmp_ref[...] + gcol[1] * o_slc + gcol[2] * o_win


def _nsa_sample_slc(cache, layer, page_table, idx, q4, rows_s, win_buf, win_layer, win_s, o_cmp, gates_s, halves):
    b = q4.shape[0]
    n_sel = idx.shape[1] // KV_HEADS
    kv_rows = 2 * KV_HEADS
    spec4 = pl.BlockSpec((None, None, GROUP, HEAD_DIM), lambda i, g, pt, ix: (i, g, 0, 0))
    new_spec = lambda col0: pl.BlockSpec((None, None, 1, HEAD_DIM), lambda i, g, pt, ix: (i, col0 + g, 0, 0))
    grid_spec = pltpu.PrefetchScalarGridSpec(
        num_scalar_prefetch=2,
        grid=(b, KV_HEADS),
        in_specs=[
            spec4,
            pl.BlockSpec(memory_space=pl.ANY),
            new_spec(2 * KV_HEADS), new_spec(3 * KV_HEADS),
            pl.BlockSpec((None, None) + win_buf.shape[2:], lambda i, g, pt, ix: (win_layer, i, 0, 0)),
            new_spec(0), new_spec(KV_HEADS),
            spec4,
            pl.BlockSpec((None, None, 1, HEAD_DIM), lambda i, g, pt, ix: (i, g, 0, 0)),
        ],
        out_specs=spec4,
        scratch_shapes=[pltpu.VMEM((2 * n_sel * cache.shape[2], HEAD_DIM), F32),
                        pltpu.SemaphoreType.DMA((2, n_sel))],
    )
    return pl.pallas_call(
        functools.partial(_nsa_sample_slc_kernel, layer=layer, n_sel=n_sel, halves=halves, n_seq=b),
        grid_spec=grid_spec,
        out_shape=jax.ShapeDtypeStruct((b, KV_HEADS, GROUP, HEAD_DIM), F32),
        compiler_params=_cparams(("arbitrary", "arbitrary")),
        name="nsa_sample_slc",
    )(page_table, idx, q4, cache, rows_s, rows_s, win_buf, win_s, win_s, o_cmp, gates_s)


def _moba_prompt_kernel(q_ref, k_ref, v_ref, o_ref, k_s, v_s, km_s, *, t_len, tq):
    qi = pl.program_id(2)
    nb = km_s.shape[0]

    @pl.when(qi == 0)
    def _():
        k = k_ref[...]
        k_s[...] = k.astype(BF16)
        v_s[...] = v_ref[...].astype(BF16)
        km_s[...] = (jnp.sum(k.reshape(nb, MOBA_BLOCK, HEAD_DIM), axis=1) * (1.0 / MOBA_BLOCK)).astype(BF16)

    t0 = qi * tq
    tpos = t0 + _iota((tq, 1), 0)
    cur_t = (t0 + _iota((nb, tq), 1)) >> 8
    blk_t = _iota((nb, tq), 0)
    qb = q_ref[...]
    km = km_s[...]
    q_all = jnp.concatenate([qb[:, r * HEAD_DIM:(r + 1) * HEAD_DIM] for r in range(GROUP)], axis=0)
    gate_t_all = _dot_nt(km, q_all)
    sel_t = []
    for r in range(GROUP):
        sel = _rank_select(gate_t_all[:, r * tq:(r + 1) * tq], blk_t < cur_t, MOBA_TOPK, axis=0) | (blk_t == cur_t)
        sel_t.append(jnp.where(sel, 1.0, 0.0).astype(BF16))

    n_prefix = 4 if t_len % (4 * MOBA_BLOCK) == 0 else 1
    step = t_len // n_prefix
    for pi in range(n_prefix):
        @pl.when((t0 + tq - 1) // step == pi)
        def _(lk=(pi + 1) * step):
            expand = jnp.where((_iota((nb, lk), 1) >> 8) == _iota((nb, lk), 0), 1.0, 0.0).astype(BF16)
            causal = _iota((tq, lk), 1) <= tpos
            picked = jnp.concatenate(
                [lax.dot_general(sel_t[r], expand, (((0,), (0,)), ((), ())), preferred_element_type=F32)
                 for r in range(GROUP)], axis=0).reshape(GROUP, tq, lk)
            o = _attend_group(q_all, k_s[0:lk, :], v_s[0:lk, :], (picked > 0.5) & causal[None])
            for r in range(GROUP):
                o_ref[:, r * HEAD_DIM:(r + 1) * HEAD_DIM] = o[r * tq:(r + 1) * tq].astype(o_ref.dtype)


def _moba_prompt_attn(q, kv, b, t, tq=128):
    assert t % MOBA_BLOCK == 0
    nq = t // tq
    gw = GROUP * HEAD_DIM
    kv_spec = lambda col0: pl.BlockSpec((t, HEAD_DIM), lambda i, g, qi: (i, col0 + g))
    return pl.pallas_call(
        functools.partial(_moba_prompt_kernel, t_len=t, tq=tq),
        grid=(b, KV_HEADS, nq),
        in_specs=[pl.BlockSpec((tq, gw), lambda i, g, qi: (i * nq + qi, g)), kv_spec(0), kv_spec(KV_HEADS)],
        out_specs=pl.BlockSpec((tq, gw), lambda i, g, qi: (i * nq + qi, g)),
        out_shape=jax.ShapeDtypeStruct((b * t, N_HEADS * HEAD_DIM), BF16),
        scratch_shapes=[pltpu.VMEM((t, HEAD_DIM), BF16), pltpu.VMEM((t, HEAD_DIM), BF16),
                        pltpu.VMEM((t // MOBA_BLOCK, HEAD_DIM), BF16)],
        compiler_params=_cparams(("parallel", "parallel", "arbitrary")),
        name="moba_prompt_attn",
    )(q, kv, kv)


def _moba_sample_gate_kernel(pt_ref, q_ref, cache_ref, idx_ref, slab, sems, ksum, *, layer, pages, ppb, n_seq, n_chunks):
    c = pl.program_id(1)
    step = pl.program_id(0) * n_chunks + c
    per_row = 2 * KV_HEADS
    rows = slab.shape[0] // 2
    page_rows = rows // pages

    def copies(st, slot):
        b, cc = st // n_chunks, st % n_chunks
        return [pltpu.make_async_copy(cache_ref.at[layer, pt_ref[b, cc * pages + p]],
                                      slab.at[pl.ds((slot * pages + p) * page_rows, page_rows)], sems.at[slot, p])
                for p in range(pages)]

    slot = _double_buffer(step, n_seq * n_chunks, copies)
    x = slab[pl.ds(pl.multiple_of(slot * rows, rows), rows), :]
    n_blk = pages // ppb
    sums = jnp.sum(x.reshape(n_blk, rows // (n_blk * per_row), per_row, HEAD_DIM), axis=1)
    ksum[pl.ds(pl.multiple_of(c * (n_blk * per_row), n_blk * per_row), n_blk * per_row), :] = sums.reshape(
        n_blk * per_row, HEAD_DIM)

    @pl.when(c == n_chunks - 1)
    def _():
        nb = ksum.shape[0] // per_row
        for g in range(KV_HEADS):
            km = (ksum[pl.ds(g, nb, stride=per_row), :] * (1.0 / MOBA_BLOCK)).astype(BF16)
            idx_ref[g] = _topk_indices(_dot_nt(q_ref[g], km), MOBA_TOPK)


def _moba_sample_gate(cache, layer, page_table, q4, pages=16):
    b, n_pages = page_table.shape
    per_row = 2 * KV_HEADS
    page_rows = cache.shape[2]
    psz = page_rows // per_row
    ppb = MOBA_BLOCK // psz
    nb = n_pages // ppb
    pages = min(pages, n_pages)
    assert MOBA_BLOCK % psz == 0 and n_pages % pages == 0 and pages % ppb == 0 and nb >= MOBA_TOPK
    grid_spec = pltpu.PrefetchScalarGridSpec(
        num_scalar_prefetch=1,
        grid=(b, n_pages // pages),
        in_specs=[
            pl.BlockSpec((None, KV_HEADS, GROUP, HEAD_DIM), lambda i, c, pt: (i, 0, 0, 0)),
            pl.BlockSpec(memory_space=pl.ANY),
        ],
        out_specs=pl.BlockSpec((None, KV_HEADS, GROUP, 128), lambda i, c, pt: (i, 0, 0, 0)),
        scratch_shapes=[pltpu.VMEM((2 * pages * page_rows, HEAD_DIM), F32), pltpu.SemaphoreType.DMA((2, pages)),
                        pltpu.VMEM((nb * per_row, HEAD_DIM), F32)],
    )
    return pl.pallas_call(
        functools.partial(_moba_sample_gate_kernel, layer=layer, pages=pages, ppb=ppb, n_seq=b,
                          n_chunks=n_pages // pages),
        grid_spec=grid_spec,
        out_shape=jax.ShapeDtypeStruct((b, KV_HEADS, GROUP, 128), jnp.int32),
        compiler_params=_cparams(("arbitrary", "arbitrary")),
        name="moba_sample_gate",
    )(page_table, q4, cache)


def _moba_sample_attn_kernel(pt_ref, idx_ref, q_ref, cache_ref, knew_ref, vnew_ref, o_ref, slab, sems,
                             *, layer, n_sel, ppb, n_seq):
    h = pl.program_id(1)
    step = pl.program_id(0) * N_HEADS + h
    per_row = 2 * KV_HEADS
    n_pg = n_sel * ppb
    rows = slab.shape[0] // 2
    page_rows = rows // n_pg

    def copies(st, slot):
        b, hh = st // N_HEADS, st % N_HEADS
        out = []
        for n in range(n_pg):
            page = pt_ref[b, idx_ref[b, hh * n_sel + n // ppb] * ppb + n % ppb]
            out.append(pltpu.make_async_copy(cache_ref.at[layer, page],
                                             slab.at[pl.ds((slot * n_pg + n) * page_rows, page_rows)],
                                             sems.at[slot, n]))
        return out

    slot = _double_buffer(step, n_seq * N_HEADS, copies)
    base = slot * rows
    g = h // GROUP
    n_keys = rows // per_row
    q = q_ref[...]
    k_sel = slab[pl.ds(base + g, n_keys, stride=per_row), :].astype(BF16)
    v_sel = slab[pl.ds(base + KV_HEADS + g, n_keys, stride=per_row), :].astype(BF16)
    p, p_new = _softmax_with_extra(_dot_nt(q, k_sel) * SCALE, _row_dot(q, knew_ref[...]) * SCALE)
    o = _dot(p.astype(BF16), v_sel) + p_new.astype(BF16).astype(F32) * vnew_ref[...].astype(BF16).astype(F32)
    o_ref[...] = o.astype(o_ref.dtype)


def _moba_sample_attn(cache, layer, page_table, idx, q16, kv_new):
    b = q16.shape[0]
    page_rows = cache.shape[2]
    ppb = MOBA_BLOCK // (page_rows // (2 * KV_HEADS))
    n_sel = idx.shape[1] // N_HEADS
    n_pg = n_sel * ppb
    hspec = pl.BlockSpec((None, None, 1, HEAD_DIM), lambda i, h, pt, ix: (i, h, 0, 0))
    new_spec = lambda col0: pl.BlockSpec((None, None, 1, HEAD_DIM), lambda i, h, pt, ix: (i, col0 + h // GROUP, 0, 0))
    grid_spec = pltpu.PrefetchScalarGridSpec(
        num_scalar_prefetch=2,
        grid=(b, N_HEADS),
        in_specs=[hspec, pl.BlockSpec(memory_space=pl.ANY), new_spec(0), new_spec(KV_HEADS)],
        out_specs=hspec,
        scratch_shapes=[pltpu.VMEM((2 * n_pg * page_rows, HEAD_DIM), F32), pltpu.SemaphoreType.DMA((2, n_pg))],
    )
    return pl.pallas_call(
        functools.partial(_moba_sample_attn_kernel, layer=layer, n_sel=n_sel, ppb=ppb, n_seq=b),
        grid_spec=grid_spec,
        out_shape=jax.ShapeDtypeStruct((b, N_HEADS, 1, HEAD_DIM), BF16),
        compiler_params=_cparams(("arbitrary", "arbitrary")),
        name="moba_sample_attn",
    )(page_table, idx, q16, cache, kv_new, kv_new)


def _log_sigmoid(z):
    return jnp.minimum(z, 0.0) - jnp.log(1.0 + jnp.exp(-jnp.abs(z)))


def _split2(x):
    hi = x.astype(BF16)
    return hi, (x - hi.astype(F32)).astype(BF16)


EXP_UNDERFLOW = -104.0


def _sb_prompt_kernel(q_ref, k_ref, v_ref, o_ref, k_s, v_s, acc_scr, *, tq):
    qi = pl.program_id(2)

    @pl.when(qi == 0)
    def _():
        k_s[...] = k_ref[...].astype(BF16)
        v_s[...] = v_ref[...].astype(BF16)

    hp = acc_scr.shape[0]
    lanes = lambda h: slice(h * HEAD_DIM, (h + 1) * HEAD_DIM)
    qs = [q_ref[:, lanes(h)].astype(BF16) for h in range(hp)]
    after = jnp.where(_iota((tq, tq), 0) > _iota((tq, tq), 1), 1.0, 0.0).astype(BF16)
    below = (_iota((hp * tq, tq), 0) & (tq - 1)) > _iota((hp * tq, tq), 1)
    acc_scr[...] = jnp.zeros_like(acc_scr)

    def chunk(c, run, diagonal):
        s0 = pl.multiple_of(c * tq, tq)
        z = jnp.concatenate([_dot_nt(qs[h], k_s[pl.ds(s0, tq), lanes(h)]) for h in range(hp)], axis=0) * SCALE
        log_beta = _log_sigmoid(z)
        log_keep = log_beta - z
        if diagonal:
            log_keep = jnp.where(below, log_keep, 0.0)
        both = _dot(jnp.concatenate(_split2(log_keep), axis=0), after)
        within = both[:hp * tq] + both[hp * tq:]
        a = jnp.exp(log_beta + within + run)
        if diagonal:
            a = jnp.where(below, a, 0.0)
        a = a.astype(BF16)
        for h in range(hp):
            acc_scr[h] += _dot(a[h * tq:(h + 1) * tq], v_s[pl.ds(s0, tq), lanes(h)])
        return run + within[:, 0:1] + log_keep[:, 0:1]

    run = chunk(qi, jnp.zeros((hp * tq, 1), F32), True)

    def cond(carry):
        c, run = carry
        return (c >= 0) & (jnp.max(run) >= EXP_UNDERFLOW)

    def body(carry):
        c, run = carry
        return c - 1, chunk(c, run, False)

    lax.while_loop(cond, body, (qi - 1, run))
    for h in range(hp):
        o_ref[:, lanes(h)] = acc_scr[h].astype(o_ref.dtype)


def _sb_prompt_attn(qkv, b, t, tq=256, hp=2):
    nq = t // tq
    n_hp = N_HEADS // hp
    w = hp * HEAD_DIM
    assert tq & (tq - 1) == 0
    return pl.pallas_call(
        functools.partial(_sb_prompt_kernel, tq=tq),
        grid=(b, n_hp, nq),
        in_specs=[
            pl.BlockSpec((tq, w), lambda i, h, qi: (i * nq + qi, h)),
            pl.BlockSpec((t, w), lambda i, h, qi: (i, n_hp + h)),
            pl.BlockSpec((t, w), lambda i, h, qi: (i, 2 * n_hp + h)),
        ],
        out_specs=pl.BlockSpec((tq, w), lambda i, h, qi: (i * nq + qi, h)),
        out_shape=jax.ShapeDtypeStruct((b * t, N_HEADS * HEAD_DIM), BF16),
        scratch_shapes=[pltpu.VMEM((t, w), BF16)] * 2 + [pltpu.VMEM((hp, tq, HEAD_DIM), F32)],
        compiler_params=_cparams(("parallel", "parallel", "arbitrary")),
        name="sb_prompt_attn",
    )(qkv, qkv, qkv)


def _sb_sample_kernel(pt_ref, qbd_ref, cache_ref, o_ref, buf, sems, acc_scr, *, layer, n_pages):
    b = pl.program_id(0)
    per_row = 2 * N_HEADS
    rows = buf.shape[0] // 2
    psz = rows // per_row

    def page_copy(p, slot):
        return pltpu.make_async_copy(cache_ref.at[layer, pt_ref[b, n_pages - 1 - p]],
                                     buf.at[pl.ds(slot * rows, rows)], sems.at[slot])

    acc_scr[...] = jnp.zeros_like(acc_scr)
    after = jnp.where(_iota((psz, psz), 1) > _iota((psz, psz), 0), 1.0, 0.0).astype(BF16)
    live = _iota((1, 128), 1) < N_HEADS
    page_copy(0, 0).start()

    def cond(carry):
        p, run = carry
        return (p < n_pages) & (jnp.max(jnp.where(live, run, NEG_INF)) >= EXP_UNDERFLOW)

    def body(carry):
        p, run = carry
        slot = p % 2
        page_copy(p, slot).wait()

        @pl.when(p + 1 < n_pages)
        def _():
            page_copy(p + 1, 1 - slot).start()

        base = slot * rows

        def heads_wide(first):
            return jnp.concatenate([buf[pl.ds(base + first + h, psz, stride=per_row), :].astype(BF16)
                                    for h in range(N_HEADS)], axis=1)

        z = _dot(heads_wide(0), qbd_ref[...]) * SCALE
        log_beta = _log_sigmoid(z)
        log_keep = log_beta - z
        hi, lo = _split2(log_keep)
        within = _dot(after, hi) + _dot(after, lo)
        a = jnp.exp(log_beta + within + run)
        acc_scr[...] += _dot(a.T.astype(BF16), heads_wide(N_HEADS))
        return p + 1, run + within[0:1, :] + log_keep[0:1, :]

    p_end, _ = lax.while_loop(cond, body, (0, jnp.zeros((1, 128), F32)))

    @pl.when(p_end < n_pages)
    def _():
        page_copy(p_end, p_end % 2).wait()

    acc = acc_scr[...]
    own = _iota(acc.shape, 0) == (_iota(acc.shape, 1) >> 7)
    o_ref[...] = jnp.sum(jnp.where(own, acc, 0.0), axis=0, keepdims=True)


def _sb_sample_attn(cache, layer, page_table, qbd):
    b, n_pages = page_table.shape
    aw = N_HEADS * HEAD_DIM
    grid_spec = pltpu.PrefetchScalarGridSpec(
        num_scalar_prefetch=1,
        grid=(b,),
        in_specs=[
            pl.BlockSpec((None, aw, 128), lambda i, pt: (i, 0, 0)),
            pl.BlockSpec(memory_space=pl.ANY),
        ],
        out_specs=pl.BlockSpec((None, 1, aw), lambda i, pt: (i, 0, 0)),
        scratch_shapes=[pltpu.VMEM((2 * cache.shape[2], HEAD_DIM), F32), pltpu.SemaphoreType.DMA((2,)),
                        pltpu.VMEM((128, aw), F32)],
    )
    return pl.pallas_call(
        functools.partial(_sb_sample_kernel, layer=layer, n_pages=n_pages),
        grid_spec=grid_spec,
        out_shape=jax.ShapeDtypeStruct((b, 1, aw), F32),
        compiler_params=_cparams(("arbitrary",)),
        name="sb_sample_attn",
    )(page_table, qbd, cache)


def _tile_rows(m, tile=512):
    return tile if m % tile == 0 else m


PROJ_ROWS = 1024


def _nsa_weights(w_in, q_gain, ks_gain, kw_gain):
    d = w_in.shape[0]
    main = N_HEADS * HEAD_DIM + 6 * KV_HEADS * HEAD_DIM
    wg = w_in[:, main:].reshape(d, KV_HEADS, GROUP * 3)
    wg = jnp.pad(wg, ((0, 0), (0, 0), (0, HEAD_DIM - GROUP * 3))).reshape(d, KV_HEADS * HEAD_DIM)
    w = jnp.concatenate([w_in[:, :main], wg], axis=1).astype(BF16)
    tiles = [("rope", 0)] * 4 + [("plain", 1), ("plain", 1), ("rope", 1), ("plain", 1),
                                 ("rope", 2), ("plain", 2), ("sigmoid", 3)]
    ones = jnp.ones((HEAD_DIM,), F32)
    hg = jnp.stack([q_gain] * 4 + [ones, ones, ks_gain, ones, kw_gain, ones, ones])[:, None, :]
    outs = [(4, BF16), (4, F32), (2, F32), (1, F32)]
    return w, tiles, hg, outs


def _nsa_layer(xp, xs, b, t, cache, layer, page_table, win_all, norm_gain, w_in, w_out, q_gain, kc_gain,
               ks_gain, kw_gain, cmp_pos, cmp_w1, cmp_w2):
    bs = xs.shape[0]
    n_pages, psz = page_table.shape[1], cache.shape[2]
    past = n_pages * psz
    w, tiles, hg, outs = _nsa_weights(w_in, q_gain, ks_gain, kw_gain)
    w1, w2 = cmp_w1.astype(BF16), cmp_w2.astype(BF16)
    w_out = w_out.astype(BF16)

    cos_p, sin_p = _rope_tables(jnp.arange(t, dtype=jnp.int32))
    cos_pt, sin_pt = jnp.tile(cos_p, (b, 1)), jnp.tile(sin_p, (b, 1))
    q, rows, win, gates = _norm_proj(xp, norm_gain, w, hg, cos_pt, sin_pt, tiles, outs, _tile_rows(b * t, PROJ_ROWS))
    nc = t // CMP_BLOCK
    cos_c, sin_c = _rope_tables(jnp.arange(nc, dtype=jnp.int32) * CMP_BLOCK + (CMP_BLOCK - 1))
    k_cmp, v_cmp = _compress_prompt(rows, b, t, cmp_pos, w1, w2, kc_gain, cos_c, sin_c)
    o = _nsa_prompt_attn(q, rows, win, k_cmp, v_cmp, gates, b, t)
    xp = _out_proj(o, w_out, xp, _tile_rows(b * t, PROJ_ROWS))

    cos_s, sin_s = _rope_tables(jnp.full((bs,), past, jnp.int32))
    qs, rows_s, win_s, gates_s = _norm_proj(xs, norm_gain, w, hg, cos_s, sin_s, tiles, outs, bs)
    ncs = past // CMP_BLOCK
    cos_cs, sin_cs = _rope_tables(jnp.arange(ncs, dtype=jnp.int32) * CMP_BLOCK + (CMP_BLOCK - 1))
    n_layers, n_pool = cache.shape[:2]
    k_cmp_s, v_cmp_s = _compress_sample(cache.reshape(n_layers, n_pool, psz, 2, 2 * KV_HEADS, HEAD_DIM), layer,
                                        page_table, cmp_pos, w1, w2, kc_gain, cos_cs, sin_cs)
    q4 = qs.reshape(bs, KV_HEADS, GROUP, HEAD_DIM)
    o_cmp, idx = _nsa_sample_cmp(q4, k_cmp_s, v_cmp_s, past)
    idx = idx[:, :, 0, :N_SELECT - 1].reshape(bs, KV_HEADS * (N_SELECT - 1))
    halves = psz // CMP_BLOCK
    block_view = cache.reshape(n_layers, n_pool * halves, CMP_BLOCK * 4 * KV_HEADS, HEAD_DIM)
    nw = win_all.shape[2]
    win_view = win_all.reshape(win_all.shape[0], bs, nw * 2 * KV_HEADS, HEAD_DIM)
    o_s = _nsa_sample_slc(block_view, layer, page_table, idx, q4,
                          rows_s.reshape(bs, 4 * KV_HEADS, 1, HEAD_DIM), win_view, layer,
                          win_s.reshape(bs, 2 * KV_HEADS, 1, HEAD_DIM), o_cmp,
                          gates_s.reshape(bs, KV_HEADS, 1, HEAD_DIM), halves)
    xs = _out_proj(o_s.reshape(bs, N_HEADS * HEAD_DIM).astype(BF16), w_out, xs, bs)

    kv_p = rows.reshape(b, t, 4, KV_HEADS, HEAD_DIM)
    kv_s = rows_s.reshape(bs, 1, 4, KV_HEADS, HEAD_DIM)
    win_p = win.reshape(b, t, 2, KV_HEADS, HEAD_DIM)[:, -min(WINDOW, t):]
    win_new = jnp.concatenate([win_all[layer], win_s.reshape(bs, 1, 2, KV_HEADS, HEAD_DIM)], axis=1)[:, -nw:]
    return xp, xs, kv_p, kv_s, win_p, win_new


def _moba_layer(xp, xs, b, t, cache, layer, page_table, norm_gain, w_in, w_out, q_gain, k_gain):
    bs = xs.shape[0]
    n_pages, psz = page_table.shape[1], cache.shape[2]
    past = n_pages * psz
    w = w_in.astype(BF16)
    w_out = w_out.astype(BF16)
    tiles = [("rope", 0)] * 4 + [("rope", 1), ("plain", 1)]
    ones = jnp.ones((HEAD_DIM,), F32)
    hg = jnp.stack([q_gain] * 4 + [k_gain, ones])[:, None, :]
    outs = [(4, BF16), (2, F32)]

    cos_p, sin_p = _rope_tables(jnp.arange(t, dtype=jnp.int32))
    cos_pt, sin_pt = jnp.tile(cos_p, (b, 1)), jnp.tile(sin_p, (b, 1))
    q, kv = _norm_proj(xp, norm_gain, w, hg, cos_pt, sin_pt, tiles, outs, _tile_rows(b * t, PROJ_ROWS))
    o = _moba_prompt_attn(q, kv, b, t)
    xp = _out_proj(o, w_out, xp, _tile_rows(b * t, PROJ_ROWS))

    cos_s, sin_s = _rope_tables(jnp.full((bs,), past, jnp.int32))
    qs, kv_s = _norm_proj(xs, norm_gain, w, hg, cos_s, sin_s, tiles, outs, bs)
    cache2 = cache.reshape(cache.shape[0], cache.shape[1], psz * 2 * KV_HEADS, HEAD_DIM)
    idx = _moba_sample_gate(cache2, layer, page_table, qs.reshape(bs, KV_HEADS, GROUP, HEAD_DIM))
    idx = idx.reshape(bs, N_HEADS, 128)[:, :, :MOBA_TOPK].reshape(bs, N_HEADS * MOBA_TOPK)
    o_s = _moba_sample_attn(cache2, layer, page_table, idx, qs.reshape(bs, N_HEADS, 1, HEAD_DIM),
                            kv_s.reshape(bs, 2 * KV_HEADS, 1, HEAD_DIM))
    xs = _out_proj(o_s.reshape(bs, N_HEADS * HEAD_DIM), w_out, xs, bs)
    return (xp, xs, kv.reshape(b, t, 2, KV_HEADS, HEAD_DIM), kv_s.reshape(bs, 1, 2, KV_HEADS, HEAD_DIM))


def _sb_layer(xp, xs, b, t, cache, layer, page_table, norm_gain, w_in, w_out):
    bs = xs.shape[0]
    psz = cache.shape[2]
    aw = N_HEADS * HEAD_DIM
    w = w_in.astype(BF16)
    w_out = w_out.astype(BF16)
    n_tiles = w.shape[1] // 512
    tiles = [("plain", 0)] * n_tiles
    hg = jnp.ones((n_tiles, 1, HEAD_DIM), F32)
    outs = [(n_tiles, F32)]
    dummy = jnp.zeros((b * t, HEAD_DIM), F32)
    (qkv,) = _norm_proj(xp, norm_gain, w, hg, dummy, dummy, tiles, outs, _tile_rows(b * t, PROJ_ROWS))
    o = _sb_prompt_attn(qkv, b, t)
    xp = _out_proj(o, w_out, xp, _tile_rows(b * t, PROJ_ROWS))

    (qkv_s,) = _norm_proj(xs, norm_gain, w, hg, dummy[:bs], dummy[:bs], tiles, outs, bs)
    qs = qkv_s[:, :aw].astype(BF16).reshape(bs, N_HEADS, HEAD_DIM)
    eye = jnp.eye(N_HEADS, 128, dtype=BF16)
    qbd = (qs[:, :, :, None] * eye[None, :, None, :]).reshape(bs, aw, 128)
    cache2 = cache.reshape(cache.shape[0], cache.shape[1], psz * 2 * N_HEADS, HEAD_DIM)
    o_s = _sb_sample_attn(cache2, layer, page_table, qbd)
    xs = _out_proj(o_s.reshape(bs, aw).astype(BF16), w_out, xs, bs)
    return (xp, xs, qkv[:, aw:].reshape(b, t, 2, N_HEADS, HEAD_DIM), qkv_s[:, aw:].reshape(bs, 1, 2, N_HEADS, HEAD_DIM))


def kernel(x_prompt, x_sample, cache_nsa_kv, state_nsa_win, cache_moba_kv, cache_sb_kv, page_table, mix_norm, ffn_norm, nsa_w_in, nsa_w_out, nsa_q_gain, nsa_kc_gain, nsa_ks_gain, nsa_kw_gain, nsa_cmp_pos, nsa_cmp_w1, nsa_cmp_w2, moba_w_in, moba_w_out, moba_q_gain, moba_k_gain, sb_w_in, sb_w_out, ffn_w_gate, ffn_w_up, ffn_w_down):
    b, t, d = x_prompt.shape
    bs, ts, _ = x_sample.shape
    assert ts == 1
    depth = mix_norm.shape[0]
    xp = x_prompt.reshape(b * t, d)
    xs = x_sample.reshape(bs * ts, d)
    nsa_kv_p, nsa_kv_s, nsa_win_p, nsa_win_s = [], [], [], []
    moba_kv_p, moba_kv_s, sb_kv_p, sb_kv_s = [], [], [], []
    for i in range(depth):
        j = i // N_MIXERS
        if i % N_MIXERS == 0:
            xp, xs, kvp, kvs, wp, ws = _nsa_layer(
                xp, xs, b, t, cache_nsa_kv, j, page_table, state_nsa_win, mix_norm[i], nsa_w_in[j], nsa_w_out[j],
                nsa_q_gain[j], nsa_kc_gain[j], nsa_ks_gain[j], nsa_kw_gain[j], nsa_cmp_pos[j], nsa_cmp_w1[j],
                nsa_cmp_w2[j])
            nsa_kv_p.append(kvp)
            nsa_kv_s.append(kvs)
            nsa_win_p.append(wp)
            nsa_win_s.append(ws)
        elif i % N_MIXERS == 1:
            xp, xs, kvp, kvs = _moba_layer(xp, xs, b, t, cache_moba_kv, j, page_table, mix_norm[i], moba_w_in[j],
                                           moba_w_out[j], moba_q_gain[j], moba_k_gain[j])
            moba_kv_p.append(kvp)
            moba_kv_s.append(kvs)
        else:
            xp, xs, kvp, kvs = _sb_layer(xp, xs, b, t, cache_sb_kv, j, page_table, mix_norm[i], sb_w_in[j], sb_w_out[j])
            sb_kv_p.append(kvp)
            sb_kv_s.append(kvs)
        wg, wu, wd = ffn_w_gate[i].astype(BF16), ffn_w_up[i].astype(BF16), ffn_w_down[i].astype(BF16)
        xp = _ffn(xp, ffn_norm[i], wg, wu, wd, _tile_rows(b * t))
        xs = _ffn(xs, ffn_norm[i], wg, wu, wd, bs * ts)
    return (xp.reshape(b, t, d), xs.reshape(bs, ts, d), jnp.stack(nsa_kv_p), jnp.stack(nsa_kv_s),
            jnp.stack(nsa_win_p), jnp.stack(nsa_win_s), jnp.stack(moba_kv_p), jnp.stack(moba_kv_s),
            jnp.stack(sb_kv_p), jnp.stack(sb_kv_s))
```

```python
import functools
import math

import jax
import jax.numpy as jnp
from jax import lax
from jax.experimental import pallas as pl
from jax.experimental.pallas import tpu as pltpu

N_HEADS = 16
HEAD_DIM = 128
KV_HEADS = 4
GROUP = N_HEADS // KV_HEADS
CMP_BLOCK = 64
N_SELECT = 16
WINDOW = 512
MOBA_BLOCK = 256
MOBA_TOPK = 3
N_MIXERS = 3
ROPE_THETA = 10000.0
RMS_EPS = 1e-6
NEG_INF = -1e30
REMOVED = -3e38
SCALE = HEAD_DIM ** -0.5
LOG2_E = math.log2(math.e)

VMEM_LIMIT = 56 * 1024 * 1024
F32 = jnp.float32
BF16 = jnp.bfloat16


def _cparams(sem):
    return pltpu.CompilerParams(dimension_semantics=sem, vmem_limit_bytes=VMEM_LIMIT)


def _dot(a, b):
    return jnp.dot(a, b, preferred_element_type=F32)


def _dot_nt(a, b):
    return lax.dot_general(a, b, (((1,), (1,)), ((), ())), preferred_element_type=F32)


def _iota(shape, dim):
    return lax.broadcasted_iota(jnp.int32, shape, dim)


def _masked_softmax(s, mask):
    s = jnp.where(mask, s, NEG_INF)
    m = jnp.max(s, axis=-1, keepdims=True)
    e = jnp.exp(s - m)
    p = e * (1.0 / jnp.sum(e, axis=-1, keepdims=True))
    return jnp.where(mask, p, 0.0)


def _attend_group(q_all, k, v, mask):
    n = k.shape[0]
    m = mask.shape[-2]
    heads = q_all.shape[0] // m
    s = _dot_nt(q_all, k).reshape(heads, m, n) * (SCALE * LOG2_E)
    s = jnp.where(mask if mask.ndim == 3 else mask[None], s, NEG_INF)
    e = jnp.exp2(s - jnp.max(s, axis=-1, keepdims=True))
    inv = 1.0 / jnp.sum(e, axis=-1, keepdims=True)
    return _dot(e.reshape(heads * m, n).astype(BF16), v) * inv.reshape(heads * m, 1)


def _head_norm_rope(x, gain, cos, sin):
    y = x * lax.rsqrt(jnp.mean(x * x, axis=-1, keepdims=True) + RMS_EPS) * gain
    return y * cos + pltpu.roll(y, HEAD_DIM // 2, 1) * sin


def _rope_tables(pos):
    half = HEAD_DIM // 2
    inv_freq = ROPE_THETA ** (-jnp.arange(half, dtype=F32) / half)
    ang = pos.astype(F32)[:, None] * inv_freq[None, :]
    cos, sin = jnp.cos(ang), jnp.sin(ang)
    return jnp.concatenate([cos, cos], axis=-1), jnp.concatenate([-sin, sin], axis=-1)


def _rank_select(val, cand, k, axis):
    n = val.shape[axis]
    idx = _iota(val.shape, axis)
    v = jnp.where(cand, val, NEG_INF)
    rank = jnp.zeros(val.shape, jnp.int32)
    for j in range(n):
        vj = lax.slice_in_dim(v, j, j + 1, axis=axis)
        ahead = (vj > v) | ((vj == v) & (idx > j))
        rank = rank + jnp.where(ahead, 1, 0)
    return cand & (rank < k)


def _topk_indices(val, k):
    rows, n = val.shape
    lane = _iota((rows, n), 1).astype(F32)
    out_lane = _iota((rows, 128), 1)
    out = jnp.zeros((rows, 128), F32)
    for i in range(k):
        m = jnp.max(val, axis=-1, keepdims=True)
        pick = jnp.min(jnp.where(val == m, lane, float(n)), axis=-1, keepdims=True)
        out = jnp.where(out_lane == i, pick, out)
        val = jnp.where(lane == pick, REMOVED, val)
    return out.astype(jnp.int32)


def _double_buffer(step, n_steps, copies):
    slot = step % 2

    @pl.when(step == 0)
    def _():
        for c in copies(step, slot):
            c.start()

    @pl.when(step + 1 < n_steps)
    def _():
        for c in copies(step + 1, 1 - slot):
            c.start()

    for c in copies(step, slot):
        c.wait()
    return slot


def _proj_kernel(x_ref, g_ref, w_ref, hg_ref, cos_ref, sin_ref, *refs, tiles, n_out, cache_rows):
    out_refs, cache_refs, h_scr = refs[:n_out], refs[n_out:n_out + len(cache_rows)], refs[-1]
    j = pl.program_id(1)
    tm = x_ref.shape[0]

    @pl.when(j == 0)
    def _():
        x = x_ref[...]
        h = x * lax.rsqrt(jnp.mean(x * x, axis=-1, keepdims=True) + RMS_EPS) * g_ref[...]
        h_scr[...] = h.astype(BF16)

    acc = _dot(h_scr[...], w_ref[...])
    for jt, (mode, oi, cache_at) in enumerate(tiles):
        @pl.when(j == jt)
        def _(mode=mode, oi=oi, cache_at=cache_at):
            o_ref = out_refs[oi]
            for hh in range(acc.shape[1] // HEAD_DIM):
                sl = slice(hh * HEAD_DIM, (hh + 1) * HEAD_DIM)
                y = acc[:, sl]
                if mode == "rope":
                    y = _head_norm_rope(y, hg_ref[0], cos_ref[...], sin_ref[...])
                elif mode == "sigmoid":
                    y = 1.0 / (1.0 + jnp.exp(-y))
                o_ref[:, sl] = y.astype(o_ref.dtype)
                if cache_at is not None:
                    ci, first = cache_at
                    cache_refs[ci][pl.ds(first + hh, tm, stride=cache_rows[ci]), :] = y


def _norm_proj(x, gain, w, head_gains, cos, sin, tiles, outs, tm, tn=512, cache_rows=()):
    m, d = x.shape
    n_tiles = len(tiles)
    assert w.shape == (d, n_tiles * tn) and m % tm == 0
    starts = []
    for oi in range(len(outs)):
        starts.append(min(jt for jt, t in enumerate(tiles) if t[1] == oi))

    def out_map(oi):
        s, cnt = starts[oi], outs[oi][0]
        return lambda i, j: (i, jnp.clip(j - s, 0, cnt - 1))

    return pl.pallas_call(
        functools.partial(_proj_kernel, tiles=tuple(tiles), n_out=len(outs), cache_rows=tuple(cache_rows)),
        grid=(m // tm, n_tiles),
        in_specs=[
            pl.BlockSpec((tm, d), lambda i, j: (i, 0)),
            pl.BlockSpec((1, d), lambda i, j: (0, 0)),
            pl.BlockSpec((d, tn), lambda i, j: (0, j)),
            pl.BlockSpec((1, 1, HEAD_DIM), lambda i, j: (j, 0, 0)),
            pl.BlockSpec((tm, HEAD_DIM), lambda i, j: (i, 0)),
            pl.BlockSpec((tm, HEAD_DIM), lambda i, j: (i, 0)),
        ],
        out_specs=[pl.BlockSpec((tm, tn), out_map(oi)) for oi in range(len(outs))] + [
            pl.BlockSpec((tm * r, HEAD_DIM), lambda i, j: (i, 0), pipeline_mode=pl.Buffered(1)) for r in cache_rows],
        out_shape=[jax.ShapeDtypeStruct((m, cnt * tn), dt) for cnt, dt in outs] + [
            jax.ShapeDtypeStruct((m * r, HEAD_DIM), F32) for r in cache_rows],
        scratch_shapes=[pltpu.VMEM((tm, d), BF16)],
        compiler_params=_cparams(("parallel", "arbitrary")),
        name="norm_proj",
    )(x, gain.reshape(1, d), w, head_gains, cos, sin)


def _out_proj_kernel(o_ref, w_ref, x_ref, y_ref):
    y_ref[...] = x_ref[...] + _dot(o_ref[...], w_ref[...])


def _out_proj(o, w, x, tm, tn=512):
    m, k = o.shape
    n = w.shape[1]
    return pl.pallas_call(
        _out_proj_kernel,
        grid=(m // tm, n // tn),
        in_specs=[
            pl.BlockSpec((tm, k), lambda i, j: (i, 0)),
            pl.BlockSpec((k, tn), lambda i, j: (0, j)),
            pl.BlockSpec((tm, tn), lambda i, j: (i, j)),
        ],
        out_specs=pl.BlockSpec((tm, tn), lambda i, j: (i, j)),
        out_shape=jax.ShapeDtypeStruct((m, n), F32),
        compiler_params=_cparams(("parallel", "arbitrary")),
        name="out_proj",
    )(o, w, x)


def _ffn_kernel(x_ref, g_ref, wg_ref, wu_ref, wd_ref, y_ref, h_scr, acc_scr):
    f = pl.program_id(1)

    @pl.when(f == 0)
    def _():
        x = x_ref[...]
        h = x * lax.rsqrt(jnp.mean(x * x, axis=-1, keepdims=True) + RMS_EPS) * g_ref[...]
        h_scr[...] = h.astype(BF16)
        acc_scr[...] = jnp.zeros_like(acc_scr)

    h = h_scr[...]
    a = _dot(h, wg_ref[...])
    u = _dot(h, wu_ref[...])
    act = (a * (1.0 / (1.0 + jnp.exp(-a))) * u).astype(BF16)
    acc_scr[...] += _dot(act, wd_ref[...])

    @pl.when(f == pl.num_programs(1) - 1)
    def _():
        y_ref[...] = x_ref[...] + acc_scr[...]


def _ffn(x, gain, wg, wu, wd, tm, tf=512):
    m, d = x.shape
    dff = wg.shape[1]
    assert dff % tf == 0 and m % tm == 0
    return pl.pallas_call(
        _ffn_kernel,
        grid=(m // tm, dff // tf),
        in_specs=[
            pl.BlockSpec((tm, d), lambda i, f: (i, 0)),
            pl.BlockSpec((1, d), lambda i, f: (0, 0)),
            pl.BlockSpec((d, tf), lambda i, f: (0, f)),
            pl.BlockSpec((d, tf), lambda i, f: (0, f)),
            pl.BlockSpec((tf, d), lambda i, f: (f, 0)),
        ],
        out_specs=pl.BlockSpec((tm, d), lambda i, f: (i, 0)),
        out_shape=jax.ShapeDtypeStruct((m, d), F32),
        scratch_shapes=[pltpu.VMEM((tm, d), BF16), pltpu.VMEM((tm, d), F32)],
        compiler_params=_cparams(("parallel", "arbitrary")),
        name="ffn",
    )(x, gain.reshape(1, d), wg, wu, wd)


def _gelu_tanh(x):
    return 0.5 * x * (1.0 + jnp.tanh(math.sqrt(2.0 / math.pi) * (x + 0.044715 * (x * x * x))))


def _compress_slab(load, pos_ref, w1_ref, w2_ref, gain_ref, cos_ref, sin_ref, ok_ref, ov_ref, nblk):
    jc = 8

    for s, o_ref in ((0, ok_ref), (1, ov_ref)):
        def body(jj, acc, s=s):
            pieces = []
            for g in range(KV_HEADS):
                cols = []
                for jo in range(jc):
                    j = jj * jc + jo
                    xj = load(s * KV_HEADS + g, j)
                    cols.append((xj + pos_ref[s, pl.ds(j, 1), :]).astype(BF16))
                pieces.append(jnp.concatenate(cols, axis=1))
            lhs = jnp.concatenate(pieces, axis=0)
            k0 = pl.multiple_of(jj * (jc * HEAD_DIM), jc * HEAD_DIM)
            return acc + _dot(lhs, w1_ref[s, pl.ds(k0, jc * HEAD_DIM), :])

        hidden = lax.fori_loop(0, CMP_BLOCK // jc, body, jnp.zeros((KV_HEADS * nblk, w1_ref.shape[2]), F32))
        y = _dot(_gelu_tanh(hidden).astype(BF16), w2_ref[s])
        for g in range(KV_HEADS):
            yg = y[g * nblk:(g + 1) * nblk]
            if s == 0:
                yg = _head_norm_rope(yg, gain_ref[...], cos_ref[...], sin_ref[...])
            o_ref[g] = yg.astype(o_ref.dtype)


def _compress_prompt_kernel(*refs, nblk):
    n_col = 2 * KV_HEADS
    load = lambda c, j: refs[c][pl.ds(j, nblk, stride=CMP_BLOCK), :]
    _compress_slab(load, *refs[n_col:], nblk)


def _compress_prompt(rows, b, t, pos_emb, w1, w2, kc_gain, cos_c, sin_c):
    nc = t // CMP_BLOCK
    n_col = 2 * KV_HEADS
    out = jax.ShapeDtypeStruct((b, KV_HEADS, nc, HEAD_DIM), BF16)
    return pl.pallas_call(
        functools.partial(_compress_prompt_kernel, nblk=nc),
        grid=(b,),
        in_specs=[pl.BlockSpec((t, HEAD_DIM), lambda i, c=c: (i, c)) for c in range(n_col)] + [
            pl.BlockSpec(pos_emb.shape, lambda i: (0, 0, 0)),
            pl.BlockSpec(w1.shape, lambda i: (0, 0, 0)),
            pl.BlockSpec(w2.shape, lambda i: (0, 0, 0)),
            pl.BlockSpec((1, HEAD_DIM), lambda i: (0, 0)),
            pl.BlockSpec((nc, HEAD_DIM), lambda i: (0, 0)),
            pl.BlockSpec((nc, HEAD_DIM), lambda i: (0, 0)),
        ],
        out_specs=[pl.BlockSpec((None, KV_HEADS, nc, HEAD_DIM), lambda i: (i, 0, 0, 0))] * 2,
        out_shape=[out, out],
        compiler_params=_cparams(("arbitrary",)),
        name="nsa_compress_prompt",
    )(*([rows] * n_col), pos_emb, w1, w2, kc_gain.reshape(1, HEAD_DIM), cos_c, sin_c)


def _compress_sample_kernel(pt_ref, cache_ref, pos_ref, w1_ref, w2_ref, gain_ref, cos_ref, sin_ref,
                            ok_ref, ov_ref, slab, sems, y_scr, *, layer, pages, n_seq, n_chunks):
    step = pl.program_id(0) * n_chunks + pl.program_id(1)
    n_col = slab.shape[1]
    psz = slab.shape[0] // (2 * pages)
    nblk = pages * psz // CMP_BLOCK
    m = nblk * n_col

    def copies(st, slot):
        b, c = st // n_chunks, st % n_chunks
        return [pltpu.make_async_copy(cache_ref.at[layer, pt_ref[b, c * pages + p], :, 0],
                                      slab.at[pl.ds((slot * pages + p) * psz, psz)], sems.at[slot, p])
                for p in range(pages)]

    slot = _double_buffer(step, n_seq * n_chunks, copies)
    base = slot * (pages * psz)
    jc = 8

    def hidden_for(s):
        def body(jj, acc):
            cols = []
            for jo in range(jc):
                j = jj * jc + jo
                xj = slab[pl.ds(base + j, nblk, stride=CMP_BLOCK)] + pos_ref[pl.ds(j, 1)]
                cols.append(xj.reshape(m, HEAD_DIM).astype(BF16))
            k0 = pl.multiple_of(jj * (jc * HEAD_DIM), jc * HEAD_DIM)
            return acc + _dot(jnp.concatenate(cols, axis=1), w1_ref[s, pl.ds(k0, jc * HEAD_DIM), :])
        return lax.fori_loop(0, CMP_BLOCK // jc, body, jnp.zeros((m, w1_ref.shape[2]), F32))

    is_key = (_iota((m, 1), 0) & (n_col - 1)) < KV_HEADS
    act = _gelu_tanh(jnp.where(is_key, hidden_for(0), hidden_for(1))).astype(BF16)
    y_scr[...] = jnp.where(is_key, _dot(act, w2_ref[0]), _dot(act, w2_ref[1]))
    for g in range(KV_HEADS):
        yk = y_scr[pl.ds(g, nblk, stride=n_col), :]
        ok_ref[g] = _head_norm_rope(yk, gain_ref[...], cos_ref[...], sin_ref[...]).astype(ok_ref.dtype)
        ov_ref[g] = y_scr[pl.ds(KV_HEADS + g, nblk, stride=n_col), :].astype(ov_ref.dtype)


def _compress_sample(cache, layer, page_table, pos_emb, w1, w2, kc_gain, cos_c, sin_c, pages=16):
    b, n_pages = page_table.shape
    psz = cache.shape[2]
    pages = min(pages, n_pages)
    assert n_pages % pages == 0 and psz % CMP_BLOCK == 0
    nblk = pages * psz // CMP_BLOCK
    nc = n_pages * psz // CMP_BLOCK
    n_col = 2 * KV_HEADS
    pos_rows = jnp.repeat(jnp.swapaxes(pos_emb, 0, 1), KV_HEADS, axis=1)
    out = jax.ShapeDtypeStruct((b, KV_HEADS, nc, HEAD_DIM), BF16)
    grid_spec = pltpu.PrefetchScalarGridSpec(
        num_scalar_prefetch=1,
        grid=(b, n_pages // pages),
        in_specs=[
            pl.BlockSpec(memory_space=pl.ANY),
            pl.BlockSpec(pos_rows.shape, lambda i, c, pt: (0, 0, 0)),
            pl.BlockSpec(w1.shape, lambda i, c, pt: (0, 0, 0)),
            pl.BlockSpec(w2.shape, lambda i, c, pt: (0, 0, 0)),
            pl.BlockSpec((1, HEAD_DIM), lambda i, c, pt: (0, 0)),
            pl.BlockSpec((nblk, HEAD_DIM), lambda i, c, pt: (c, 0)),
            pl.BlockSpec((nblk, HEAD_DIM), lambda i, c, pt: (c, 0)),
        ],
        out_specs=[pl.BlockSpec((None, KV_HEADS, nblk, HEAD_DIM), lambda i, c, pt: (i, 0, c, 0))] * 2,
        scratch_shapes=[pltpu.VMEM((2 * pages * psz, n_col, HEAD_DIM), F32), pltpu.SemaphoreType.DMA((2, pages)),
                        pltpu.VMEM((nblk * n_col, HEAD_DIM), F32)],
    )
    return pl.pallas_call(
        functools.partial(_compress_sample_kernel, layer=layer, pages=pages, n_seq=b, n_chunks=n_pages // pages),
        grid_spec=grid_spec,
        out_shape=[out, out],
        compiler_params=_cparams(("arbitrary", "arbitrary")),
        name="nsa_compress_sample",
    )(page_table, cache, pos_rows, w1, w2, kc_gain.reshape(1, HEAD_DIM), cos_c, sin_c)


def _nsa_prompt_kernel(q_ref, ks_ref, vs_ref, kw_ref, vw_ref, kc_ref, vc_ref, gt_ref, o_ref,
                       ks_s, vs_s, kw_s, vw_s, *, t_len, tq):
    qi = pl.program_id(2)

    @pl.when(qi == 0)
    def _():
        ks_s[...] = ks_ref[...].astype(BF16)
        vs_s[...] = vs_ref[...].astype(BF16)
        kw_s[...] = kw_ref[...].astype(BF16)
        vw_s[...] = vw_ref[...].astype(BF16)

    nc = kc_ref.shape[0]
    t0 = qi * tq
    tpos = t0 + _iota((tq, 1), 0)
    qb = q_ref[...]
    gt = gt_ref[...]
    kc, vc = kc_ref[...], vc_ref[...]
    q_all = jnp.concatenate([qb[:, r * HEAD_DIM:(r + 1) * HEAD_DIM] for r in range(GROUP)], axis=0)

    cmask = ((_iota((tq, nc), 1) * CMP_BLOCK + (CMP_BLOCK - 1)) <= tpos)[None]
    s_cmp = (_dot_nt(q_all, kc) * SCALE).reshape(GROUP, tq, nc)
    o_cmp = _dot(_masked_softmax(s_cmp, cmask).reshape(GROUP * tq, nc).astype(BF16), vc)
    tpos_t = t0 + _iota((nc, tq), 1)
    cblk_t = _iota((nc, tq), 0)
    cmask_t = (cblk_t * CMP_BLOCK + (CMP_BLOCK - 1)) <= tpos_t
    s_t_all = _dot_nt(kc, q_all) * SCALE
    imp_t = jnp.zeros((nc, tq), F32)
    for r in range(GROUP):
        s_t = jnp.where(cmask_t, s_t_all[:, r * tq:(r + 1) * tq], NEG_INF)
        e_t = jnp.exp(s_t - jnp.max(s_t, axis=0, keepdims=True))
        imp_t = imp_t + jnp.where(cmask_t, e_t * (1.0 / jnp.sum(e_t, axis=0, keepdims=True)), 0.0)

    cur_t = tpos_t >> 6
    sel_t = _rank_select(imp_t, cblk_t < cur_t, N_SELECT - 1, axis=0) | (cblk_t == cur_t)
    sel_t = jnp.where(sel_t, 1.0, 0.0).astype(BF16)

    wn = min(WINDOW + tq, t_len)
    w0 = pl.multiple_of(jnp.maximum(t0 + tq - wn, 0), tq)
    wd = tpos - (w0 + _iota((tq, wn), 1))
    wmask = (wd >= 0) & (wd < WINDOW)
    kw, vw = kw_s[pl.ds(w0, wn), :], vw_s[pl.ds(w0, wn), :]
    o_win = _attend_group(q_all, kw, vw, wmask)

    n_prefix = 4 if t_len % (4 * tq) == 0 else 1
    step = t_len // n_prefix
    for pi in range(n_prefix):
        @pl.when((t0 + tq - 1) // step == pi)
        def _(lk=(pi + 1) * step):
            expand = jnp.where((_iota((nc, lk), 1) >> 6) == _iota((nc, lk), 0), 1.0, 0.0).astype(BF16)
            picked = lax.dot_general(sel_t, expand, (((0,), (0,)), ((), ())), preferred_element_type=F32)
            smask = (picked > 0.5) & (_iota((tq, lk), 1) <= tpos)
            o_slc = _attend_group(q_all, ks_s[0:lk, :], vs_s[0:lk, :], smask)
            for r in range(GROUP):
                rows = slice(r * tq, (r + 1) * tq)
                o = (gt[:, 3 * r:3 * r + 1] * o_cmp[rows] + gt[:, 3 * r + 1:3 * r + 2] * o_slc[rows]
                     + gt[:, 3 * r + 2:3 * r + 3] * o_win[rows])
                o_ref[:, r * HEAD_DIM:(r + 1) * HEAD_DIM] = o.astype(o_ref.dtype)


def _nsa_prompt_attn(q, rows, win, k_cmp, v_cmp, gates, b, t, tq=256):
    nq = t // tq
    nc = t // CMP_BLOCK
    gw = GROUP * HEAD_DIM
    kv_spec = lambda col0: pl.BlockSpec((t, HEAD_DIM), lambda i, g, qi: (i, col0 + g))
    cmp_spec = pl.BlockSpec((None, None, nc, HEAD_DIM), lambda i, g, qi: (i, g, 0, 0))
    return pl.pallas_call(
        functools.partial(_nsa_prompt_kernel, t_len=t, tq=tq),
        grid=(b, KV_HEADS, nq),
        in_specs=[
            pl.BlockSpec((tq, gw), lambda i, g, qi: (i * nq + qi, g)),
            kv_spec(2 * KV_HEADS), kv_spec(3 * KV_HEADS), kv_spec(0), kv_spec(KV_HEADS),
            cmp_spec, cmp_spec,
            pl.BlockSpec((tq, HEAD_DIM), lambda i, g, qi: (i * nq + qi, g)),
        ],
        out_specs=pl.BlockSpec((tq, gw), lambda i, g, qi: (i * nq + qi, g)),
        out_shape=jax.ShapeDtypeStruct((b * t, N_HEADS * HEAD_DIM), BF16),
        scratch_shapes=[pltpu.VMEM((t, HEAD_DIM), BF16)] * 4,
        compiler_params=_cparams(("parallel", "parallel", "arbitrary")),
        name="nsa_prompt_attn",
    )(q, rows, rows, win, win, k_cmp, v_cmp, gates)


def _nsa_sample_cmp_kernel(q_ref, kc_ref, vc_ref, o_ref, idx_ref, *, pos):
    nc = kc_ref.shape[0]
    cblk = _iota((GROUP, nc), 1)
    cmask = (cblk * CMP_BLOCK + (CMP_BLOCK - 1)) <= pos
    p = _masked_softmax(_dot_nt(q_ref[...], kc_ref[...]) * SCALE, cmask)
    o_ref[...] = _dot(p.astype(BF16), vc_ref[...])
    imp = jnp.sum(p, axis=0, keepdims=True)
    cand = _iota((1, nc), 1) < (pos // CMP_BLOCK)
    idx_ref[...] = _topk_indices(jnp.where(cand, imp, NEG_INF), N_SELECT - 1)


def _nsa_sample_cmp(q4, k_cmp, v_cmp, pos):
    b = q4.shape[0]
    nc = k_cmp.shape[2]
    assert nc >= N_SELECT - 1 and pos // CMP_BLOCK >= N_SELECT - 1
    spec4 = pl.BlockSpec((None, None, GROUP, HEAD_DIM), lambda i, g: (i, g, 0, 0))
    cspec = pl.BlockSpec((None, None, nc, HEAD_DIM), lambda i, g: (i, g, 0, 0))
    return pl.pallas_call(
        functools.partial(_nsa_sample_cmp_kernel, pos=pos),
        grid=(b, KV_HEADS),
        in_specs=[spec4, cspec, cspec],
        out_specs=[spec4, pl.BlockSpec((None, None, 1, 128), lambda i, g: (i, g, 0, 0))],
        out_shape=[jax.ShapeDtypeStruct((b, KV_HEADS, GROUP, HEAD_DIM), F32),
                   jax.ShapeDtypeStruct((b, KV_HEADS, 1, 128), jnp.int32)],
        compiler_params=_cparams(("parallel", "parallel")),
        name="nsa_sample_cmp",
    )(q4, k_cmp, v_cmp)


def _softmax_with_extra(s, s_new):
    m = jnp.maximum(jnp.max(s, axis=-1, keepdims=True), s_new)
    e, e_new = jnp.exp(s - m), jnp.exp(s_new - m)
    inv = 1.0 / (jnp.sum(e, axis=-1, keepdims=True) + e_new)
    return e * inv, e_new * inv


def _row_dot(q, k_row):
    return jnp.sum(q.astype(F32) * k_row.astype(BF16).astype(F32), axis=-1, keepdims=True)


def _nsa_sample_slc_kernel(pt_ref, idx_ref, q_ref, cache_ref, knew_ref, vnew_ref,
                           win_ref, kwnew_ref, vwnew_ref, ocmp_ref, gt_ref, o_ref,
                           slab, sems, *, layer, n_sel, halves, n_seq):
    g = pl.program_id(1)
    step = pl.program_id(0) * KV_HEADS + g
    per_row = 4 * KV_HEADS
    kv_rows = 2 * KV_HEADS
    rows = slab.shape[0] // 2
    blk_rows = CMP_BLOCK * per_row

    def copies(st, slot):
        b, gg = st // KV_HEADS, st % KV_HEADS
        out = []
        for n in range(n_sel):
            blk = idx_ref[b, gg * n_sel + n]
            src = cache_ref.at[layer, pt_ref[b, blk // halves] * halves + blk % halves]
            out.append(pltpu.make_async_copy(src, slab.at[pl.ds((slot * n_sel + n) * blk_rows, blk_rows)],
                                             sems.at[slot, n]))
        return out

    slot = _double_buffer(step, n_seq * KV_HEADS, copies)
    base = slot * rows
    n_keys = n_sel * CMP_BLOCK
    q = q_ref[...]
    k_sel = slab[pl.ds(base + 2 * KV_HEADS + g, n_keys, stride=per_row), :].astype(BF16)
    v_sel = slab[pl.ds(base + 3 * KV_HEADS + g, n_keys, stride=per_row), :].astype(BF16)
    p, p_new = _softmax_with_extra(_dot_nt(q, k_sel) * SCALE, _row_dot(q, knew_ref[...]) * SCALE)
    o_slc = _dot(p.astype(BF16), v_sel) + p_new.astype(BF16).astype(F32) * vnew_ref[...].astype(BF16).astype(F32)

    nw = win_ref.shape[0] // kv_rows
    kwin = win_ref[pl.ds(g, nw, stride=kv_rows), :].astype(BF16)
    vwin = win_ref[pl.ds(KV_HEADS + g, nw, stride=kv_rows), :].astype(BF16)
    sw = jnp.where(nw - _iota((GROUP, nw), 1) < WINDOW, _dot_nt(q, kwin) * SCALE, NEG_INF)
    p, p_new = _softmax_with_extra(sw, _row_dot(q, kwnew_ref[...]) * SCALE)
    o_win = _dot(p.astype(BF16), vwin) + p_new.astype(BF16).astype(F32) * vwnew_ref[...].astype(BF16).astype(F32)

    gt = gt_ref[...]
    rowi = _iota((GROUP, 1), 0)
    gcol = [sum(jnp.where(rowi == r, gt[:, 3 * r + c:3 * r + c + 1], 0.0) for r in range(GROUP)) for c in range(3)]
    o_ref[...] = gcol[0] * ocmp_ref[...] + gcol[1] * o_slc + gcol[2] * o_win


def _nsa_sample_slc(cache, layer, page_table, idx, q4, rows_s, win_buf, win_layer, win_s, o_cmp, gates_s, halves):
    b = q4.shape[0]
    n_sel = idx.shape[1] // KV_HEADS
    kv_rows = 2 * KV_HEADS
    spec4 = pl.BlockSpec((None, None, GROUP, HEAD_DIM), lambda i, g, pt, ix: (i, g, 0, 0))
    new_spec = lambda col0: pl.BlockSpec((None, None, 1, HEAD_DIM), lambda i, g, pt, ix: (i, col0 + g, 0, 0))
    grid_spec = pltpu.PrefetchScalarGridSpec(
        num_scalar_prefetch=2,
        grid=(b, KV_HEADS),
        in_specs=[
            spec4,
            pl.BlockSpec(memory_space=pl.ANY),
            new_spec(2 * KV_HEADS), new_spec(3 * KV_HEADS),
            pl.BlockSpec((None, None) + win_buf.shape[2:], lambda i, g, pt, ix: (win_layer, i, 0, 0)),
            new_spec(0), new_spec(KV_HEADS),
            spec4,
            pl.BlockSpec((None, None, 1, HEAD_DIM), lambda i, g, pt, ix: (i, g, 0, 0)),
        ],
        out_specs=spec4,
        scratch_shapes=[pltpu.VMEM((2 * n_sel * cache.shape[2], HEAD_DIM), F32),
                        pltpu.SemaphoreType.DMA((2, n_sel))],
    )
    return pl.pallas_call(
        functools.partial(_nsa_sample_slc_kernel, layer=layer, n_sel=n_sel, halves=halves, n_seq=b),
        grid_spec=grid_spec,
        out_shape=jax.ShapeDtypeStruct((b, KV_HEADS, GROUP, HEAD_DIM), F32),
        compiler_params=_cparams(("arbitrary", "arbitrary")),
        name="nsa_sample_slc",
    )(page_table, idx, q4, cache, rows_s, rows_s, win_buf, win_s, win_s, o_cmp, gates_s)


def _moba_prompt_kernel(q_ref, k_ref, v_ref, o_ref, k_s, v_s, km_s, *, t_len, tq):
    qi = pl.program_id(2)
    nb = km_s.shape[0]

    @pl.when(qi == 0)
    def _():
        k = k_ref[...]
        k_s[...] = k.astype(BF16)
        v_s[...] = v_ref[...].astype(BF16)
        km_s[...] = (jnp.sum(k.reshape(nb, MOBA_BLOCK, HEAD_DIM), axis=1) * (1.0 / MOBA_BLOCK)).astype(BF16)

    t0 = qi * tq
    tpos = t0 + _iota((tq, 1), 0)
    cur_t = (t0 + _iota((nb, tq), 1)) >> 8
    blk_t = _iota((nb, tq), 0)
    qb = q_ref[...]
    km = km_s[...]
    q_all = jnp.concatenate([qb[:, r * HEAD_DIM:(r + 1) * HEAD_DIM] for r in range(GROUP)], axis=0)
    gate_t_all = _dot_nt(km, q_all)
    sel_t = []
    for r in range(GROUP):
        sel = _rank_select(gate_t_all[:, r * tq:(r + 1) * tq], blk_t < cur_t, MOBA_TOPK, axis=0) | (blk_t == cur_t)
        sel_t.append(jnp.where(sel, 1.0, 0.0).astype(BF16))

    n_prefix = 4 if t_len % (4 * MOBA_BLOCK) == 0 else 1
    step = t_len // n_prefix
    for pi in range(n_prefix):
        @pl.when((t0 + tq - 1) // step == pi)
        def _(lk=(pi + 1) * step):
            expand = jnp.where((_iota((nb, lk), 1) >> 8) == _iota((nb, lk), 0), 1.0, 0.0).astype(BF16)
            causal = _iota((tq, lk), 1) <= tpos
            picked = jnp.concatenate(
                [lax.dot_general(sel_t[r], expand, (((0,), (0,)), ((), ())), preferred_element_type=F32)
                 for r in range(GROUP)], axis=0).reshape(GROUP, tq, lk)
            o = _attend_group(q_all, k_s[0:lk, :], v_s[0:lk, :], (picked > 0.5) & causal[None])
            for r in range(GROUP):
                o_ref[:, r * HEAD_DIM:(r + 1) * HEAD_DIM] = o[r * tq:(r + 1) * tq].astype(o_ref.dtype)


def _moba_prompt_attn(q, kv, b, t, tq=256):
    assert t % MOBA_BLOCK == 0
    nq = t // tq
    gw = GROUP * HEAD_DIM
    kv_spec = lambda col0: pl.BlockSpec((t, HEAD_DIM), lambda i, g, qi: (i, col0 + g))
    return pl.pallas_call(
        functools.partial(_moba_prompt_kernel, t_len=t, tq=tq),
        grid=(b, KV_HEADS, nq),
        in_specs=[pl.BlockSpec((tq, gw), lambda i, g, qi: (i * nq + qi, g)), kv_spec(0), kv_spec(KV_HEADS)],
        out_specs=pl.BlockSpec((tq, gw), lambda i, g, qi: (i * nq + qi, g)),
        out_shape=jax.ShapeDtypeStruct((b * t, N_HEADS * HEAD_DIM), BF16),
        scratch_shapes=[pltpu.VMEM((t, HEAD_DIM), BF16), pltpu.VMEM((t, HEAD_DIM), BF16),
                        pltpu.VMEM((t // MOBA_BLOCK, HEAD_DIM), BF16)],
        compiler_params=_cparams(("parallel", "parallel", "arbitrary")),
        name="moba_prompt_attn",
    )(q, kv, kv)


def _moba_sample_gate_kernel(pt_ref, q_ref, cache_ref, idx_ref, slab, sems, ksum, *, layer, pages, ppb, n_seq, n_chunks):
    c = pl.program_id(1)
    step = pl.program_id(0) * n_chunks + c
    per_row = 2 * KV_HEADS
    rows = slab.shape[0] // 2
    page_rows = rows // pages

    def copies(st, slot):
        b, cc = st // n_chunks, st % n_chunks
        return [pltpu.make_async_copy(cache_ref.at[layer, pt_ref[b, cc * pages + p]],
                                      slab.at[pl.ds((slot * pages + p) * page_rows, page_rows)], sems.at[slot, p])
                for p in range(pages)]

    slot = _double_buffer(step, n_seq * n_chunks, copies)
    x = slab[pl.ds(pl.multiple_of(slot * rows, rows), rows), :]
    n_blk = pages // ppb
    sums = jnp.sum(x.reshape(n_blk, rows // (n_blk * per_row), per_row, HEAD_DIM), axis=1)
    ksum[pl.ds(pl.multiple_of(c * (n_blk * per_row), n_blk * per_row), n_blk * per_row), :] = sums.reshape(
        n_blk * per_row, HEAD_DIM)

    @pl.when(c == n_chunks - 1)
    def _():
        nb = ksum.shape[0] // per_row
        for g in range(KV_HEADS):
            km = (ksum[pl.ds(g, nb, stride=per_row), :] * (1.0 / MOBA_BLOCK)).astype(BF16)
            idx_ref[g] = _topk_indices(_dot_nt(q_ref[g], km), MOBA_TOPK)


def _moba_sample_gate(cache, layer, page_table, q4, pages=16):
    b, n_pages = page_table.shape
    per_row = 2 * KV_HEADS
    page_rows = cache.shape[2]
    psz = page_rows // per_row
    ppb = MOBA_BLOCK // psz
    nb = n_pages // ppb
    pages = min(pages, n_pages)
    assert MOBA_BLOCK % psz == 0 and n_pages % pages == 0 and pages % ppb == 0 and nb >= MOBA_TOPK
    grid_spec = pltpu.PrefetchScalarGridSpec(
        num_scalar_prefetch=1,
        grid=(b, n_pages // pages),
        in_specs=[
            pl.BlockSpec((None, KV_HEADS, GROUP, HEAD_DIM), lambda i, c, pt: (i, 0, 0, 0)),
            pl.BlockSpec(memory_space=pl.ANY),
        ],
        out_specs=pl.BlockSpec((None, KV_HEADS, GROUP, 128), lambda i, c, pt: (i, 0, 0, 0)),
        scratch_shapes=[pltpu.VMEM((2 * pages * page_rows, HEAD_DIM), F32), pltpu.SemaphoreType.DMA((2, pages)),
                        pltpu.VMEM((nb * per_row, HEAD_DIM), F32)],
    )
    return pl.pallas_call(
        functools.partial(_moba_sample_gate_kernel, layer=layer, pages=pages, ppb=ppb, n_seq=b,
                          n_chunks=n_pages // pages),
        grid_spec=grid_spec,
        out_shape=jax.ShapeDtypeStruct((b, KV_HEADS, GROUP, 128), jnp.int32),
        compiler_params=_cparams(("arbitrary", "arbitrary")),
        name="moba_sample_gate",
    )(page_table, q4, cache)


def _moba_sample_attn_kernel(pt_ref, idx_ref, q_ref, cache_ref, knew_ref, vnew_ref, o_ref, slab, sems,
                             *, layer, n_sel, ppb, n_seq):
    h = pl.program_id(1)
    step = pl.program_id(0) * N_HEADS + h
    per_row = 2 * KV_HEADS
    n_pg = n_sel * ppb
    rows = slab.shape[0] // 2
    page_rows = rows // n_pg

    def copies(st, slot):
        b, hh = st // N_HEADS, st % N_HEADS
        out = []
        for n in range(n_pg):
            page = pt_ref[b, idx_ref[b, hh * n_sel + n // ppb] * ppb + n % ppb]
            out.append(pltpu.make_async_copy(cache_ref.at[layer, page],
                                             slab.at[pl.ds((slot * n_pg + n) * page_rows, page_rows)],
                                             sems.at[slot, n]))
        return out

    slot = _double_buffer(step, n_seq * N_HEADS, copies)
    base = slot * rows
    g = h // GROUP
    n_keys = rows // per_row
    q = q_ref[...]
    k_sel = slab[pl.ds(base + g, n_keys, stride=per_row), :].astype(BF16)
    v_sel = slab[pl.ds(base + KV_HEADS + g, n_keys, stride=per_row), :].astype(BF16)
    p, p_new = _softmax_with_extra(_dot_nt(q, k_sel) * SCALE, _row_dot(q, knew_ref[...]) * SCALE)
    o = _dot(p.astype(BF16), v_sel) + p_new.astype(BF16).astype(F32) * vnew_ref[...].astype(BF16).astype(F32)
    o_ref[...] = o.astype(o_ref.dtype)


def _moba_sample_attn(cache, layer, page_table, idx, q16, kv_new):
    b = q16.shape[0]
    page_rows = cache.shape[2]
    ppb = MOBA_BLOCK // (page_rows // (2 * KV_HEADS))
    n_sel = idx.shape[1] // N_HEADS
    n_pg = n_sel * ppb
    hspec = pl.BlockSpec((None, None, 1, HEAD_DIM), lambda i, h, pt, ix: (i, h, 0, 0))
    new_spec = lambda col0: pl.BlockSpec((None, None, 1, HEAD_DIM), lambda i, h, pt, ix: (i, col0 + h // GROUP, 0, 0))
    grid_spec = pltpu.PrefetchScalarGridSpec(
        num_scalar_prefetch=2,
        grid=(b, N_HEADS),
        in_specs=[hspec, pl.BlockSpec(memory_space=pl.ANY), new_spec(0), new_spec(KV_HEADS)],
        out_specs=hspec,
        scratch_shapes=[pltpu.VMEM((2 * n_pg * page_rows, HEAD_DIM), F32), pltpu.SemaphoreType.DMA((2, n_pg))],
    )
    return pl.pallas_call(
        functools.partial(_moba_sample_attn_kernel, layer=layer, n_sel=n_sel, ppb=ppb, n_seq=b),
        grid_spec=grid_spec,
        out_shape=jax.ShapeDtypeStruct((b, N_HEADS, 1, HEAD_DIM), BF16),
        compiler_params=_cparams(("arbitrary", "arbitrary")),
        name="moba_sample_attn",
    )(page_table, idx, q16, cache, kv_new, kv_new)


def _log_sigmoid(z):
    return jnp.minimum(z, 0.0) - jnp.log(1.0 + jnp.exp(-jnp.abs(z)))


def _split2(x):
    hi = x.astype(BF16)
    return hi, (x - hi.astype(F32)).astype(BF16)


EXP_UNDERFLOW = -104.0


def _sb_prompt_kernel(q_ref, k_ref, v_ref, o_ref, k_s, v_s, acc_scr, *, tq):
    qi = pl.program_id(2)

    @pl.when(qi == 0)
    def _():
        k_s[...] = k_ref[...].astype(BF16)
        v_s[...] = v_ref[...].astype(BF16)

    hp = acc_scr.shape[0]
    lanes = lambda h: slice(h * HEAD_DIM, (h + 1) * HEAD_DIM)
    qs = [q_ref[:, lanes(h)].astype(BF16) for h in range(hp)]
    after = jnp.where(_iota((tq, tq), 0) > _iota((tq, tq), 1), 1.0, 0.0).astype(BF16)
    below = (_iota((hp * tq, tq), 0) & (tq - 1)) > _iota((hp * tq, tq), 1)
    acc_scr[...] = jnp.zeros_like(acc_scr)

    def chunk(c, run, diagonal):
        s0 = pl.multiple_of(c * tq, tq)
        z = jnp.concatenate([_dot_nt(qs[h], k_s[pl.ds(s0, tq), lanes(h)]) for h in range(hp)], axis=0) * SCALE
        log_beta = _log_sigmoid(z)
        log_keep = log_beta - z
        if diagonal:
            log_keep = jnp.where(below, log_keep, 0.0)
        both = _dot(jnp.concatenate(_split2(log_keep), axis=0), after)
        within = both[:hp * tq] + both[hp * tq:]
        a = jnp.exp(log_beta + within + run)
        if diagonal:
            a = jnp.where(below, a, 0.0)
        a = a.astype(BF16)
        for h in range(hp):
            acc_scr[h] += _dot(a[h * tq:(h + 1) * tq], v_s[pl.ds(s0, tq), lanes(h)])
        return run + within[:, 0:1] + log_keep[:, 0:1]

    run = chunk(qi, jnp.zeros((hp * tq, 1), F32), True)

    def cond(carry):
        c, run = carry
        return (c >= 0) & (jnp.max(run) >= EXP_UNDERFLOW)

    def body(carry):
        c, run = carry
        return c - 1, chunk(c, run, False)

    lax.while_loop(cond, body, (qi - 1, run))
    for h in range(hp):
        o_ref[:, lanes(h)] = acc_scr[h].astype(o_ref.dtype)


def _sb_prompt_attn(qkv, b, t, tq=256, hp=2):
    nq = t // tq
    n_hp = N_HEADS // hp
    w = hp * HEAD_DIM
    assert tq & (tq - 1) == 0
    return pl.pallas_call(
        functools.partial(_sb_prompt_kernel, tq=tq),
        grid=(b, n_hp, nq),
        in_specs=[
            pl.BlockSpec((tq, w), lambda i, h, qi: (i * nq + qi, h)),
            pl.BlockSpec((t, w), lambda i, h, qi: (i, n_hp + h)),
            pl.BlockSpec((t, w), lambda i, h, qi: (i, 2 * n_hp + h)),
        ],
        out_specs=pl.BlockSpec((tq, w), lambda i, h, qi: (i * nq + qi, h)),
        out_shape=jax.ShapeDtypeStruct((b * t, N_HEADS * HEAD_DIM), BF16),
        scratch_shapes=[pltpu.VMEM((t, w), BF16)] * 2 + [pltpu.VMEM((hp, tq, HEAD_DIM), F32)],
        compiler_params=_cparams(("parallel", "parallel", "arbitrary")),
        name="sb_prompt_attn",
    )(qkv, qkv, qkv)


def _sb_sample_kernel(pt_ref, qbd_ref, cache_ref, o_ref, buf, sems, acc_scr, *, layer, n_pages):
    b = pl.program_id(0)
    per_row = 2 * N_HEADS
    rows = buf.shape[0] // 2
    psz = rows // per_row

    def page_copy(p, slot):
        return pltpu.make_async_copy(cache_ref.at[layer, pt_ref[b, n_pages - 1 - p]],
                                     buf.at[pl.ds(slot * rows, rows)], sems.at[slot])

    acc_scr[...] = jnp.zeros_like(acc_scr)
    after = jnp.where(_iota((psz, psz), 1) > _iota((psz, psz), 0), 1.0, 0.0).astype(BF16)
    live = _iota((1, 128), 1) < N_HEADS
    page_copy(0, 0).start()

    def cond(carry):
        p, run = carry
        return (p < n_pages) & (jnp.max(jnp.where(live, run, NEG_INF)) >= EXP_UNDERFLOW)

    def body(carry):
        p, run = carry
        slot = p % 2
        page_copy(p, slot).wait()

        @pl.when(p + 1 < n_pages)
        def _():
            page_copy(p + 1, 1 - slot).start()

        base = slot * rows

        def heads_wide(first):
            return jnp.concatenate([buf[pl.ds(base + first + h, psz, stride=per_row), :].astype(BF16)
                                    for h in range(N_HEADS)], axis=1)

        z = _dot(heads_wide(0), qbd_ref[...]) * SCALE
        log_beta = _log_sigmoid(z)
        log_keep = log_beta - z
        hi, lo = _split2(log_keep)
        within = _dot(after, hi) + _dot(after, lo)
        a = jnp.exp(log_beta + within + run)
        acc_scr[...] += _dot(a.T.astype(BF16), heads_wide(N_HEADS))
        return p + 1, run + within[0:1, :] + log_keep[0:1, :]

    p_end, _ = lax.while_loop(cond, body, (0, jnp.zeros((1, 128), F32)))

    @pl.when(p_end < n_pages)
    def _():
        page_copy(p_end, p_end % 2).wait()

    acc = acc_scr[...]
    own = _iota(acc.shape, 0) == (_iota(acc.shape, 1) >> 7)
    o_ref[...] = jnp.sum(jnp.where(own, acc, 0.0), axis=0, keepdims=True)


def _sb_sample_attn(cache, layer, page_table, qbd):
    b, n_pages = page_table.shape
    aw = N_HEADS * HEAD_DIM
    grid_spec = pltpu.PrefetchScalarGridSpec(
        num_scalar_prefetch=1,
        grid=(b,),
        in_specs=[
            pl.BlockSpec((None, aw, 128), lambda i, pt: (i, 0, 0)),
            pl.BlockSpec(memory_space=pl.ANY),
        ],
        out_specs=pl.BlockSpec((None, 1, aw), lambda i, pt: (i, 0, 0)),
        scratch_shapes=[pltpu.VMEM((2 * cache.shape[2], HEAD_DIM), F32), pltpu.SemaphoreType.DMA((2,)),
                        pltpu.VMEM((128, aw), F32)],
    )
    return pl.pallas_call(
        functools.partial(_sb_sample_kernel, layer=layer, n_pages=n_pages),
        grid_spec=grid_spec,
        out_shape=jax.ShapeDtypeStruct((b, 1, aw), F32),
        compiler_params=_cparams(("arbitrary",)),
        name="sb_sample_attn",
    )(page_table, qbd, cache)


def _tile_rows(m, tile=512):
    return tile if m % tile == 0 else m


PROJ_ROWS = 1024


def _nsa_weights(w_in, q_gain, ks_gain, kw_gain):
    d = w_in.shape[0]
    main = N_HEADS * HEAD_DIM + 6 * KV_HEADS * HEAD_DIM
    wg = w_in[:, main:].reshape(d, KV_HEADS, GROUP * 3)
    wg = jnp.pad(wg, ((0, 0), (0, 0), (0, HEAD_DIM - GROUP * 3))).reshape(d, KV_HEADS * HEAD_DIM)
    w = jnp.concatenate([w_in[:, :main], wg], axis=1).astype(BF16)
    kvh = KV_HEADS
    tiles = [("rope", 0, None)] * 4 + [
        ("plain", 1, (0, 0)), ("plain", 1, (0, kvh)), ("rope", 1, (0, 2 * kvh)), ("plain", 1, (0, 3 * kvh)),
        ("rope", 2, (1, 0)), ("plain", 2, (1, kvh)), ("sigmoid", 3, None)]
    ones = jnp.ones((HEAD_DIM,), F32)
    hg = jnp.stack([q_gain] * 4 + [ones, ones, ks_gain, ones, kw_gain, ones, ones])[:, None, :]
    outs = [(4, BF16), (4, F32), (2, F32), (1, F32)]
    return w, tiles, hg, outs


def _nsa_layer(xp, xs, b, t, cache, layer, page_table, win_all, norm_gain, w_in, w_out, q_gain, kc_gain,
               ks_gain, kw_gain, cmp_pos, cmp_w1, cmp_w2):
    bs = xs.shape[0]
    n_pages, psz = page_table.shape[1], cache.shape[2]
    past = n_pages * psz
    w, tiles, hg, outs = _nsa_weights(w_in, q_gain, ks_gain, kw_gain)
    w1, w2 = cmp_w1.astype(BF16), cmp_w2.astype(BF16)
    w_out = w_out.astype(BF16)

    cos_p, sin_p = _rope_tables(jnp.arange(t, dtype=jnp.int32))
    cos_pt, sin_pt = jnp.tile(cos_p, (b, 1)), jnp.tile(sin_p, (b, 1))
    cache_rows = (4 * KV_HEADS, 2 * KV_HEADS)
    q, rows, win, gates, rows_c, win_c = _norm_proj(xp, norm_gain, w, hg, cos_pt, sin_pt, tiles, outs,
                                                    _tile_rows(b * t, PROJ_ROWS), cache_rows=cache_rows)
    nc = t // CMP_BLOCK
    cos_c, sin_c = _rope_tables(jnp.arange(nc, dtype=jnp.int32) * CMP_BLOCK + (CMP_BLOCK - 1))
    k_cmp, v_cmp = _compress_prompt(rows, b, t, cmp_pos, w1, w2, kc_gain, cos_c, sin_c)
    o = _nsa_prompt_attn(q, rows, win, k_cmp, v_cmp, gates, b, t)
    xp = _out_proj(o, w_out, xp, _tile_rows(b * t, PROJ_ROWS))

    cos_s, sin_s = _rope_tables(jnp.full((bs,), past, jnp.int32))
    qs, _, _, gates_s, rows_s, win_s = _norm_proj(xs, norm_gain, w, hg, cos_s, sin_s, tiles, outs, bs,
                                                  cache_rows=cache_rows)
    ncs = past // CMP_BLOCK
    cos_cs, sin_cs = _rope_tables(jnp.arange(ncs, dtype=jnp.int32) * CMP_BLOCK + (CMP_BLOCK - 1))
    n_layers, n_pool = cache.shape[:2]
    k_cmp_s, v_cmp_s = _compress_sample(cache.reshape(n_layers, n_pool, psz, 2, 2 * KV_HEADS, HEAD_DIM), layer,
                                        page_table, cmp_pos, w1, w2, kc_gain, cos_cs, sin_cs)
    q4 = qs.reshape(bs, KV_HEADS, GROUP, HEAD_DIM)
    o_cmp, idx = _nsa_sample_cmp(q4, k_cmp_s, v_cmp_s, past)
    idx = idx[:, :, 0, :N_SELECT - 1].reshape(bs, KV_HEADS * (N_SELECT - 1))
    halves = psz // CMP_BLOCK
    block_view = cache.reshape(n_layers, n_pool * halves, CMP_BLOCK * 4 * KV_HEADS, HEAD_DIM)
    nw = win_all.shape[2]
    win_view = win_all.reshape(win_all.shape[0], bs, nw * 2 * KV_HEADS, HEAD_DIM)
    o_s = _nsa_sample_slc(block_view, layer, page_table, idx, q4,
                          rows_s.reshape(bs, 4 * KV_HEADS, 1, HEAD_DIM), win_view, layer,
                          win_s.reshape(bs, 2 * KV_HEADS, 1, HEAD_DIM), o_cmp,
                          gates_s.reshape(bs, KV_HEADS, 1, HEAD_DIM), halves)
    xs = _out_proj(o_s.reshape(bs, N_HEADS * HEAD_DIM).astype(BF16), w_out, xs, bs)

    kv_p = rows_c.reshape(b, t, 4, KV_HEADS, HEAD_DIM)
    kv_s = rows_s.reshape(bs, 1, 4, KV_HEADS, HEAD_DIM)
    win_p = win_c.reshape(b, t, 2, KV_HEADS, HEAD_DIM)[:, -min(WINDOW, t):]
    win_new = jnp.concatenate([win_all[layer], win_s.reshape(bs, 1, 2, KV_HEADS, HEAD_DIM)], axis=1)[:, -nw:]
    return xp, xs, kv_p, kv_s, win_p, win_new


def _moba_layer(xp, xs, b, t, cache, layer, page_table, norm_gain, w_in, w_out, q_gain, k_gain):
    bs = xs.shape[0]
    n_pages, psz = page_table.shape[1], cache.shape[2]
    past = n_pages * psz
    w = w_in.astype(BF16)
    w_out = w_out.astype(BF16)
    tiles = [("rope", 0, None)] * 4 + [("rope", 1, (0, 0)), ("plain", 1, (0, KV_HEADS))]
    cache_rows = (2 * KV_HEADS,)
    ones = jnp.ones((HEAD_DIM,), F32)
    hg = jnp.stack([q_gain] * 4 + [k_gain, ones])[:, None, :]
    outs = [(4, BF16), (2, F32)]

    cos_p, sin_p = _rope_tables(jnp.arange(t, dtype=jnp.int32))
    cos_pt, sin_pt = jnp.tile(cos_p, (b, 1)), jnp.tile(sin_p, (b, 1))
    q, kv, kv_c = _norm_proj(xp, norm_gain, w, hg, cos_pt, sin_pt, tiles, outs, _tile_rows(b * t, PROJ_ROWS),
                             cache_rows=cache_rows)
    o = _moba_prompt_attn(q, kv, b, t)
    xp = _out_proj(o, w_out, xp, _tile_rows(b * t, PROJ_ROWS))

    cos_s, sin_s = _rope_tables(jnp.full((bs,), past, jnp.int32))
    qs, _, kv_s = _norm_proj(xs, norm_gain, w, hg, cos_s, sin_s, tiles, outs, bs, cache_rows=cache_rows)
    cache2 = cache.reshape(cache.shape[0], cache.shape[1], psz * 2 * KV_HEADS, HEAD_DIM)
    idx = _moba_sample_gate(cache2, layer, page_table, qs.reshape(bs, KV_HEADS, GROUP, HEAD_DIM))
    idx = idx.reshape(bs, N_HEADS, 128)[:, :, :MOBA_TOPK].reshape(bs, N_HEADS * MOBA_TOPK)
    o_s = _moba_sample_attn(cache2, layer, page_table, idx, qs.reshape(bs, N_HEADS, 1, HEAD_DIM),
                            kv_s.reshape(bs, 2 * KV_HEADS, 1, HEAD_DIM))
    xs = _out_proj(o_s.reshape(bs, N_HEADS * HEAD_DIM), w_out, xs, bs)
    return (xp, xs, kv_c.reshape(b, t, 2, KV_HEADS, HEAD_DIM), kv_s.reshape(bs, 1, 2, KV_HEADS, HEAD_DIM))


def _sb_layer(xp, xs, b, t, cache, layer, page_table, norm_gain, w_in, w_out):
    bs = xs.shape[0]
    psz = cache.shape[2]
    aw = N_HEADS * HEAD_DIM
    w = w_in.astype(BF16)
    w_out = w_out.astype(BF16)
    n_tiles = w.shape[1] // 512
    heads_per_tile = 512 // HEAD_DIM
    q_tiles = aw // 512
    tiles = [("plain", 0, None)] * q_tiles + [("plain", 0, (0, (jt - q_tiles) * heads_per_tile))
                                              for jt in range(q_tiles, n_tiles)]
    cache_rows = (2 * N_HEADS,)
    hg = jnp.ones((n_tiles, 1, HEAD_DIM), F32)
    outs = [(n_tiles, F32)]
    dummy = jnp.zeros((b * t, HEAD_DIM), F32)
    qkv, kv_c = _norm_proj(xp, norm_gain, w, hg, dummy, dummy, tiles, outs, _tile_rows(b * t, PROJ_ROWS),
                           cache_rows=cache_rows)
    o = _sb_prompt_attn(qkv, b, t)
    xp = _out_proj(o, w_out, xp, _tile_rows(b * t, PROJ_ROWS))

    qkv_s, kv_s = _norm_proj(xs, norm_gain, w, hg, dummy[:bs], dummy[:bs], tiles, outs, bs, cache_rows=cache_rows)
    qs = qkv_s[:, :aw].astype(BF16).reshape(bs, N_HEADS, HEAD_DIM)
    eye = jnp.eye(N_HEADS, 128, dtype=BF16)
    qbd = (qs[:, :, :, None] * eye[None, :, None, :]).reshape(bs, aw, 128)
    cache2 = cache.reshape(cache.shape[0], cache.shape[1], psz * 2 * N_HEADS, HEAD_DIM)
    o_s = _sb_sample_attn(cache2, layer, page_table, qbd)
    xs = _out_proj(o_s.reshape(bs, aw).astype(BF16), w_out, xs, bs)
    return (xp, xs, kv_c.reshape(b, t, 2, N_HEADS, HEAD_DIM), kv_s.reshape(bs, 1, 2, N_HEADS, HEAD_DIM))


def kernel(x_prompt, x_sample, cache_nsa_kv, state_nsa_win, cache_moba_kv, cache_sb_kv, page_table, mix_norm, ffn_norm, nsa_w_in, nsa_w_out, nsa_q_gain, nsa_kc_gain, nsa_ks_gain, nsa_kw_gain, nsa_cmp_pos, nsa_cmp_w1, nsa_cmp_w2, moba_w_in, moba_w_out, moba_q_gain, moba_k_gain, sb_w_in, sb_w_out, ffn_w_gate, ffn_w_up, ffn_w_down):
    b, t, d = x_prompt.shape
    bs, ts, _ = x_sample.shape
    assert ts == 1
    depth = mix_norm.shape[0]
    xp = x_prompt.reshape(b * t, d)
    xs = x_sample.reshape(bs * ts, d)
    nsa_kv_p, nsa_kv_s, nsa_win_p, nsa_win_s = [], [], [], []
    moba_kv_p, moba_kv_s, sb_kv_p, sb_kv_s = [], [], [], []
    for i in range(depth):
        j = i // N_MIXERS
        if i % N_MIXERS == 0:
            xp, xs, kvp, kvs, wp, ws = _nsa_layer(
                xp, xs, b, t, cache_nsa_kv, j, page_table, state_nsa_win, mix_norm[i], nsa_w_in[j], nsa_w_out[j],
                nsa_q_gain[j], nsa_kc_gain[j], nsa_ks_gain[j], nsa_kw_gain[j], nsa_cmp_pos[j], nsa_cmp_w1[j],
                nsa_cmp_w2[j])
            nsa_kv_p.append(kvp)
            nsa_kv_s.append(kvs)
            nsa_win_p.append(wp)
            nsa_win_s.append(ws)
        elif i % N_MIXERS == 1:
            xp, xs, kvp, kvs = _moba_layer(xp, xs, b, t, cache_moba_kv, j, page_table, mix_norm[i], moba_w_in[j],
                                           moba_w_out[j], moba_q_gain[j], moba_k_gain[j])
            moba_kv_p.append(kvp)
            moba_kv_s.append(kvs)
        else:
            xp, xs, kvp, kvs = _sb_layer(xp, xs, b, t, cache_sb_kv, j, page_table, mix_norm[i], sb_w_in[j], sb_w_out[j])
            sb_kv_p.append(kvp)
            sb_kv_s.append(kvs)
        wg, wu, wd = ffn_w_gate[i].astype(BF16), ffn_w_up[i].astype(BF16), ffn_w_down[i].astype(BF16)
        xp = _ffn(xp, ffn_norm[i], wg, wu, wd, _tile_rows(b * t))
        xs = _ffn(xs, ffn_norm[i], wg, wu, wd, bs * ts)
    return (xp.reshape(b, t, d), xs.reshape(bs, ts, d), jnp.stack(nsa_kv_p), jnp.stack(nsa_kv_s),
            jnp.stack(nsa_win_p), jnp.stack(nsa_win_s), jnp.stack(moba_kv_p), jnp.stack(moba_kv_s),
            jnp.stack(sb_kv_p), jnp.stack(sb_kv_s))
```

```python
import functools
import math

import jax
import jax.numpy as jnp
from jax import lax
from jax.experimental import pallas as pl
from jax.experimental.pallas import tpu as pltpu

N_HEADS = 16
HEAD_DIM = 128
KV_HEADS = 4
GROUP = N_HEADS // KV_HEADS
CMP_BLOCK = 64
N_SELECT = 16
WINDOW = 512
MOBA_BLOCK = 256
MOBA_TOPK = 3
N_MIXERS = 3
ROPE_THETA = 10000.0
RMS_EPS = 1e-6
NEG_INF = -1e30
REMOVED = -3e38
SCALE = HEAD_DIM ** -0.5
LOG2_E = math.log2(math.e)

VMEM_LIMIT = 56 * 1024 * 1024
F32 = jnp.float32
BF16 = jnp.bfloat16


def _cparams(sem):
    return pltpu.CompilerParams(dimension_semantics=sem, vmem_limit_bytes=VMEM_LIMIT)


def _dot(a, b):
    return jnp.dot(a, b, preferred_element_type=F32)


def _dot_nt(a, b):
    return lax.dot_general(a, b, (((1,), (1,)), ((), ())), preferred_element_type=F32)


def _iota(shape, dim):
    return lax.broadcasted_iota(jnp.int32, shape, dim)


def _masked_softmax(s, mask):
    s = jnp.where(mask, s, NEG_INF)
    m = jnp.max(s, axis=-1, keepdims=True)
    e = jnp.exp(s - m)
    p = e * (1.0 / jnp.sum(e, axis=-1, keepdims=True))
    return jnp.where(mask, p, 0.0)


def _attend_group(q_all, k, v, mask):
    n = k.shape[0]
    m = mask.shape[-2]
    heads = q_all.shape[0] // m
    s = _dot_nt(q_all, k).reshape(heads, m, n) * (SCALE * LOG2_E)
    s = jnp.where(mask if mask.ndim == 3 else mask[None], s, NEG_INF)
    e = jnp.exp2(s - jnp.max(s, axis=-1, keepdims=True))
    inv = 1.0 / jnp.sum(e, axis=-1, keepdims=True)
    return _dot(e.reshape(heads * m, n).astype(BF16), v) * inv.reshape(heads * m, 1)


def _head_norm_rope(x, gain, cos, sin):
    y = x * lax.rsqrt(jnp.mean(x * x, axis=-1, keepdims=True) + RMS_EPS) * gain
    return y * cos + pltpu.roll(y, HEAD_DIM // 2, 1) * sin


def _rope_tables(pos):
    half = HEAD_DIM // 2
    inv_freq = ROPE_THETA ** (-jnp.arange(half, dtype=F32) / half)
    ang = pos.astype(F32)[:, None] * inv_freq[None, :]
    cos, sin = jnp.cos(ang), jnp.sin(ang)
    return jnp.concatenate([cos, cos], axis=-1), jnp.concatenate([-sin, sin], axis=-1)


def _rank_select(val, cand, k, axis):
    n = val.shape[axis]
    idx = _iota(val.shape, axis)
    v = jnp.where(cand, val, NEG_INF)
    rank = jnp.zeros(val.shape, jnp.int32)
    for j in range(n):
        vj = lax.slice_in_dim(v, j, j + 1, axis=axis)
        ahead = (vj > v) | ((vj == v) & (idx > j))
        rank = rank + jnp.where(ahead, 1, 0)
    return cand & (rank < k)


def _topk_indices(val, k):
    rows, n = val.shape
    lane = _iota((rows, n), 1).astype(F32)
    out_lane = _iota((rows, 128), 1)
    out = jnp.zeros((rows, 128), F32)
    for i in range(k):
        m = jnp.max(val, axis=-1, keepdims=True)
        pick = jnp.min(jnp.where(val == m, lane, float(n)), axis=-1, keepdims=True)
        out = jnp.where(out_lane == i, pick, out)
        val = jnp.where(lane == pick, REMOVED, val)
    return out.astype(jnp.int32)


def _double_buffer(step, n_steps, copies):
    slot = step % 2

    @pl.when(step == 0)
    def _():
        for c in copies(step, slot):
            c.start()

    @pl.when(step + 1 < n_steps)
    def _():
        for c in copies(step + 1, 1 - slot):
            c.start()

    for c in copies(step, slot):
        c.wait()
    return slot


def _proj_kernel(x_ref, g_ref, w_ref, hg_ref, cos_ref, sin_ref, *refs, tiles, n_out, cache_rows):
    out_refs, cache_refs, h_scr = refs[:n_out], refs[n_out:n_out + len(cache_rows)], refs[-1]
    j = pl.program_id(1)
    tm = x_ref.shape[0]

    @pl.when(j == 0)
    def _():
        x = x_ref[...]
        h = x * lax.rsqrt(jnp.mean(x * x, axis=-1, keepdims=True) + RMS_EPS) * g_ref[...]
        h_scr[...] = h.astype(BF16)

    acc = _dot(h_scr[...], w_ref[...])
    for jt, (mode, oi, cache_at) in enumerate(tiles):
        @pl.when(j == jt)
        def _(mode=mode, oi=oi, cache_at=cache_at):
            o_ref = out_refs[oi]
            for hh in range(acc.shape[1] // HEAD_DIM):
                sl = slice(hh * HEAD_DIM, (hh + 1) * HEAD_DIM)
                y = acc[:, sl]
                if mode == "rope":
                    y = _head_norm_rope(y, hg_ref[0], cos_ref[...], sin_ref[...])
                elif mode == "sigmoid":
                    y = 1.0 / (1.0 + jnp.exp(-y))
                o_ref[:, sl] = y.astype(o_ref.dtype)
                if cache_at is not None:
                    ci, first = cache_at
                    cache_refs[ci][pl.ds(first + hh, tm, stride=cache_rows[ci]), :] = y


def _norm_proj(x, gain, w, head_gains, cos, sin, tiles, outs, tm, tn=512, cache_rows=()):
    m, d = x.shape
    n_tiles = len(tiles)
    assert w.shape == (d, n_tiles * tn) and m % tm == 0
    starts = []
    for oi in range(len(outs)):
        starts.append(min(jt for jt, t in enumerate(tiles) if t[1] == oi))

    def out_map(oi):
        s, cnt = starts[oi], outs[oi][0]
        return lambda i, j: (i, jnp.clip(j - s, 0, cnt - 1))

    return pl.pallas_call(
        functools.partial(_proj_kernel, tiles=tuple(tiles), n_out=len(outs), cache_rows=tuple(cache_rows)),
        grid=(m // tm, n_tiles),
        in_specs=[
            pl.BlockSpec((tm, d), lambda i, j: (i, 0)),
            pl.BlockSpec((1, d), lambda i, j: (0, 0)),
            pl.BlockSpec((d, tn), lambda i, j: (0, j)),
            pl.BlockSpec((1, 1, HEAD_DIM), lambda i, j: (j, 0, 0)),
            pl.BlockSpec((tm, HEAD_DIM), lambda i, j: (i, 0)),
            pl.BlockSpec((tm, HEAD_DIM), lambda i, j: (i, 0)),
        ],
        out_specs=[pl.BlockSpec((tm, tn), out_map(oi)) for oi in range(len(outs))] + [
            pl.BlockSpec((tm * r, HEAD_DIM), lambda i, j: (i, 0), pipeline_mode=pl.Buffered(1)) for r in cache_rows],
        out_shape=[jax.ShapeDtypeStruct((m, cnt * tn), dt) for cnt, dt in outs] + [
            jax.ShapeDtypeStruct((m * r, HEAD_DIM), F32) for r in cache_rows],
        scratch_shapes=[pltpu.VMEM((tm, d), BF16)],
        compiler_params=_cparams(("parallel", "arbitrary")),
        name="norm_proj",
    )(x, gain.reshape(1, d), w, head_gains, cos, sin)


def _out_proj_kernel(o_ref, w_ref, x_ref, y_ref):
    y_ref[...] = x_ref[...] + _dot(o_ref[...], w_ref[...])


def _out_proj(o, w, x, tm, tn=512):
    m, k = o.shape
    n = w.shape[1]
    return pl.pallas_call(
        _out_proj_kernel,
        grid=(m // tm, n // tn),
        in_specs=[
            pl.BlockSpec((tm, k), lambda i, j: (i, 0)),
            pl.BlockSpec((k, tn), lambda i, j: (0, j)),
            pl.BlockSpec((tm, tn), lambda i, j: (i, j)),
        ],
        out_specs=pl.BlockSpec((tm, tn), lambda i, j: (i, j)),
        out_shape=jax.ShapeDtypeStruct((m, n), F32),
        compiler_params=_cparams(("parallel", "arbitrary")),
        name="out_proj",
    )(o, w, x)


def _ffn_kernel(x_ref, g_ref, wg_ref, wu_ref, wd_ref, y_ref, h_scr, acc_scr):
    f = pl.program_id(1)

    @pl.when(f == 0)
    def _():
        x = x_ref[...]
        h = x * lax.rsqrt(jnp.mean(x * x, axis=-1, keepdims=True) + RMS_EPS) * g_ref[...]
        h_scr[...] = h.astype(BF16)
        acc_scr[...] = jnp.zeros_like(acc_scr)

    h = h_scr[...]
    a = _dot(h, wg_ref[...])
    u = _dot(h, wu_ref[...])
    act = (a * (1.0 / (1.0 + jnp.exp(-a))) * u).astype(BF16)
    acc_scr[...] += _dot(act, wd_ref[...])

    @pl.when(f == pl.num_programs(1) - 1)
    def _():
        y_ref[...] = x_ref[...] + acc_scr[...]


def _ffn(x, gain, wg, wu, wd, layer, tm, tf=512):
    m, d = x.shape
    dff = wg.shape[2]
    assert dff % tf == 0 and m % tm == 0
    return pl.pallas_call(
        _ffn_kernel,
        grid=(m // tm, dff // tf),
        in_specs=[
            pl.BlockSpec((tm, d), lambda i, f: (i, 0)),
            pl.BlockSpec((1, d), lambda i, f: (0, 0)),
            pl.BlockSpec((None, d, tf), lambda i, f: (layer, 0, f)),
            pl.BlockSpec((None, d, tf), lambda i, f: (layer, 0, f)),
            pl.BlockSpec((None, tf, d), lambda i, f: (layer, f, 0)),
        ],
        out_specs=pl.BlockSpec((tm, d), lambda i, f: (i, 0)),
        out_shape=jax.ShapeDtypeStruct((m, d), F32),
        scratch_shapes=[pltpu.VMEM((tm, d), BF16), pltpu.VMEM((tm, d), F32)],
        compiler_params=_cparams(("parallel", "arbitrary")),
        name="ffn",
    )(x, gain.reshape(1, d), wg, wu, wd)


def _gelu_tanh(x):
    return 0.5 * x * (1.0 + jnp.tanh(math.sqrt(2.0 / math.pi) * (x + 0.044715 * (x * x * x))))


def _compress_slab(load, pos_ref, w1_ref, w2_ref, gain_ref, cos_ref, sin_ref, ok_ref, ov_ref, nblk):
    jc = 8

    for s, o_ref in ((0, ok_ref), (1, ov_ref)):
        def body(jj, acc, s=s):
            pieces = []
            for g in range(KV_HEADS):
                cols = []
                for jo in range(jc):
                    j = jj * jc + jo
                    xj = load(s * KV_HEADS + g, j)
                    cols.append((xj + pos_ref[s, pl.ds(j, 1), :]).astype(BF16))
                pieces.append(jnp.concatenate(cols, axis=1))
            lhs = jnp.concatenate(pieces, axis=0)
            k0 = pl.multiple_of(jj * (jc * HEAD_DIM), jc * HEAD_DIM)
            return acc + _dot(lhs, w1_ref[s, pl.ds(k0, jc * HEAD_DIM), :])

        hidden = lax.fori_loop(0, CMP_BLOCK // jc, body, jnp.zeros((KV_HEADS * nblk, w1_ref.shape[2]), F32))
        y = _dot(_gelu_tanh(hidden).astype(BF16), w2_ref[s])
        for g in range(KV_HEADS):
            yg = y[g * nblk:(g + 1) * nblk]
            if s == 0:
                yg = _head_norm_rope(yg, gain_ref[...], cos_ref[...], sin_ref[...])
            o_ref[g] = yg.astype(o_ref.dtype)


def _compress_prompt_kernel(*refs, nblk):
    n_col = 2 * KV_HEADS
    load = lambda c, j: refs[c][pl.ds(j, nblk, stride=CMP_BLOCK), :]
    _compress_slab(load, *refs[n_col:], nblk)


def _compress_prompt(rows, b, t, pos_emb, w1, w2, kc_gain, cos_c, sin_c):
    nc = t // CMP_BLOCK
    n_col = 2 * KV_HEADS
    out = jax.ShapeDtypeStruct((b, KV_HEADS, nc, HEAD_DIM), BF16)
    return pl.pallas_call(
        functools.partial(_compress_prompt_kernel, nblk=nc),
        grid=(b,),
        in_specs=[pl.BlockSpec((t, HEAD_DIM), lambda i, c=c: (i, c)) for c in range(n_col)] + [
            pl.BlockSpec(pos_emb.shape, lambda i: (0, 0, 0)),
            pl.BlockSpec(w1.shape, lambda i: (0, 0, 0)),
            pl.BlockSpec(w2.shape, lambda i: (0, 0, 0)),
            pl.BlockSpec((1, HEAD_DIM), lambda i: (0, 0)),
            pl.BlockSpec((nc, HEAD_DIM), lambda i: (0, 0)),
            pl.BlockSpec((nc, HEAD_DIM), lambda i: (0, 0)),
        ],
        out_specs=[pl.BlockSpec((None, KV_HEADS, nc, HEAD_DIM), lambda i: (i, 0, 0, 0))] * 2,
        out_shape=[out, out],
        compiler_params=_cparams(("arbitrary",)),
        name="nsa_compress_prompt",
    )(*([rows] * n_col), pos_emb, w1, w2, kc_gain.reshape(1, HEAD_DIM), cos_c, sin_c)


def _compress_sample_kernel(pt_ref, cache_ref, pos_ref, w1_ref, w2_ref, gain_ref, cos_ref, sin_ref,
                            ok_ref, ov_ref, slab, sems, y_scr, *, layer, pages, n_seq, n_chunks):
    step = pl.program_id(0) * n_chunks + pl.program_id(1)
    n_col = slab.shape[1]
    psz = slab.shape[0] // (2 * pages)
    nblk = pages * psz // CMP_BLOCK
    m = nblk * n_col

    def copies(st, slot):
        b, c = st // n_chunks, st % n_chunks
        return [pltpu.make_async_copy(cache_ref.at[layer, pt_ref[b, c * pages + p], :, 0],
                                      slab.at[pl.ds((slot * pages + p) * psz, psz)], sems.at[slot, p])
                for p in range(pages)]

    slot = _double_buffer(step, n_seq * n_chunks, copies)
    base = slot * (pages * psz)
    jc = 8

    hidden = [jnp.zeros((m, w1_ref.shape[2]), F32)] * 2
    for jj in range(CMP_BLOCK // jc):
        cols = []
        for j in range(jj * jc, (jj + 1) * jc):
            xj = slab[pl.ds(base + j, nblk, stride=CMP_BLOCK)] + pos_ref[j:j + 1]
            cols.append(xj.reshape(m, HEAD_DIM).astype(BF16))
        lhs = jnp.concatenate(cols, axis=1)
        rows = slice(jj * jc * HEAD_DIM, (jj + 1) * jc * HEAD_DIM)
        hidden = [hidden[s] + _dot(lhs, w1_ref[s, rows, :]) for s in range(2)]

    is_key = (_iota((m, 1), 0) & (n_col - 1)) < KV_HEADS
    act = _gelu_tanh(jnp.where(is_key, hidden[0], hidden[1])).astype(BF16)
    y_scr[...] = jnp.where(is_key, _dot(act, w2_ref[0]), _dot(act, w2_ref[1]))
    for g in range(KV_HEADS):
        yk = y_scr[pl.ds(g, nblk, stride=n_col), :]
        ok_ref[g] = _head_norm_rope(yk, gain_ref[...], cos_ref[...], sin_ref[...]).astype(ok_ref.dtype)
        ov_ref[g] = y_scr[pl.ds(KV_HEADS + g, nblk, stride=n_col), :].astype(ov_ref.dtype)


def _compress_sample(cache, layer, page_table, pos_emb, w1, w2, kc_gain, cos_c, sin_c, pages=16):
    b, n_pages = page_table.shape
    psz = cache.shape[2]
    pages = min(pages, n_pages)
    assert n_pages % pages == 0 and psz % CMP_BLOCK == 0
    nblk = pages * psz // CMP_BLOCK
    nc = n_pages * psz // CMP_BLOCK
    n_col = 2 * KV_HEADS
    pos_rows = jnp.repeat(jnp.swapaxes(pos_emb, 0, 1), KV_HEADS, axis=1)
    out = jax.ShapeDtypeStruct((b, KV_HEADS, nc, HEAD_DIM), BF16)
    grid_spec = pltpu.PrefetchScalarGridSpec(
        num_scalar_prefetch=1,
        grid=(b, n_pages // pages),
        in_specs=[
            pl.BlockSpec(memory_space=pl.ANY),
            pl.BlockSpec(pos_rows.shape, lambda i, c, pt: (0, 0, 0)),
            pl.BlockSpec(w1.shape, lambda i, c, pt: (0, 0, 0)),
            pl.BlockSpec(w2.shape, lambda i, c, pt: (0, 0, 0)),
            pl.BlockSpec((1, HEAD_DIM), lambda i, c, pt: (0, 0)),
            pl.BlockSpec((nblk, HEAD_DIM), lambda i, c, pt: (c, 0)),
            pl.BlockSpec((nblk, HEAD_DIM), lambda i, c, pt: (c, 0)),
        ],
        out_specs=[pl.BlockSpec((None, KV_HEADS, nblk, HEAD_DIM), lambda i, c, pt: (i, 0, c, 0))] * 2,
        scratch_shapes=[pltpu.VMEM((2 * pages * psz, n_col, HEAD_DIM), F32), pltpu.SemaphoreType.DMA((2, pages)),
                        pltpu.VMEM((nblk * n_col, HEAD_DIM), F32)],
    )
    return pl.pallas_call(
        functools.partial(_compress_sample_kernel, layer=layer, pages=pages, n_seq=b, n_chunks=n_pages // pages),
        grid_spec=grid_spec,
        out_shape=[out, out],
        compiler_params=_cparams(("arbitrary", "arbitrary")),
        name="nsa_compress_sample",
    )(page_table, cache, pos_rows, w1, w2, kc_gain.reshape(1, HEAD_DIM), cos_c, sin_c)


def _nsa_prompt_kernel(q_ref, ks_ref, vs_ref, kw_ref, vw_ref, kc_ref, vc_ref, gt_ref, o_ref,
                       ks_s, vs_s, kw_s, vw_s, *, t_len, tq):
    qi = pl.program_id(2)

    @pl.when(qi == 0)
    def _():
        ks_s[...] = ks_ref[...].astype(BF16)
        vs_s[...] = vs_ref[...].astype(BF16)
        kw_s[...] = kw_ref[...].astype(BF16)
        vw_s[...] = vw_ref[...].astype(BF16)

    nc = kc_ref.shape[0]
    t0 = qi * tq
    tpos = t0 + _iota((tq, 1), 0)
    qb = q_ref[...]
    gt = gt_ref[...]
    kc, vc = kc_ref[...], vc_ref[...]
    q_all = jnp.concatenate([qb[:, r * HEAD_DIM:(r + 1) * HEAD_DIM] for r in range(GROUP)], axis=0)

    cmask = ((_iota((tq, nc), 1) * CMP_BLOCK + (CMP_BLOCK - 1)) <= tpos)[None]
    s_cmp = (_dot_nt(q_all, kc) * SCALE).reshape(GROUP, tq, nc)
    o_cmp = _dot(_masked_softmax(s_cmp, cmask).reshape(GROUP * tq, nc).astype(BF16), vc)
    tpos_t = t0 + _iota((nc, tq), 1)
    cblk_t = _iota((nc, tq), 0)
    cmask_t = (cblk_t * CMP_BLOCK + (CMP_BLOCK - 1)) <= tpos_t
    s_t_all = _dot_nt(kc, q_all) * SCALE
    imp_t = jnp.zeros((nc, tq), F32)
    for r in range(GROUP):
        s_t = jnp.where(cmask_t, s_t_all[:, r * tq:(r + 1) * tq], NEG_INF)
        e_t = jnp.exp(s_t - jnp.max(s_t, axis=0, keepdims=True))
        imp_t = imp_t + jnp.where(cmask_t, e_t * (1.0 / jnp.sum(e_t, axis=0, keepdims=True)), 0.0)

    cur_t = tpos_t >> 6
    sel_t = _rank_select(imp_t, cblk_t < cur_t, N_SELECT - 1, axis=0) | (cblk_t == cur_t)
    sel_t = jnp.where(sel_t, 1.0, 0.0).astype(BF16)

    wn = min(WINDOW + tq, t_len)
    w0 = pl.multiple_of(jnp.maximum(t0 + tq - wn, 0), tq)
    wd = tpos - (w0 + _iota((tq, wn), 1))
    wmask = (wd >= 0) & (wd < WINDOW)
    kw, vw = kw_s[pl.ds(w0, wn), :], vw_s[pl.ds(w0, wn), :]
    o_win = _attend_group(q_all, kw, vw, wmask)

    n_prefix = 4 if t_len % (4 * tq) == 0 else 1
    step = t_len // n_prefix
    for pi in range(n_prefix):
        @pl.when((t0 + tq - 1) // step == pi)
        def _(lk=(pi + 1) * step):
            expand = jnp.where((_iota((nc, lk), 1) >> 6) == _iota((nc, lk), 0), 1.0, 0.0).astype(BF16)
            picked = lax.dot_general(sel_t, expand, (((0,), (0,)), ((), ())), preferred_element_type=F32)
            smask = (picked > 0.5) & (_iota((tq, lk), 1) <= tpos)
            o_slc = _attend_group(q_all, ks_s[0:lk, :], vs_s[0:lk, :], smask)
            for r in range(GROUP):
                rows = slice(r * tq, (r + 1) * tq)
                o = (gt[:, 3 * r:3 * r + 1] * o_cmp[rows] + gt[:, 3 * r + 1:3 * r + 2] * o_slc[rows]
                     + gt[:, 3 * r + 2:3 * r + 3] * o_win[rows])
                o_ref[:, r * HEAD_DIM:(r + 1) * HEAD_DIM] = o.astype(o_ref.dtype)


def _nsa_prompt_attn(q, rows, win, k_cmp, v_cmp, gates, b, t, tq=256):
    nq = t // tq
    nc = t // CMP_BLOCK
    gw = GROUP * HEAD_DIM
    kv_spec = lambda col0: pl.BlockSpec((t, HEAD_DIM), lambda i, g, qi: (i, col0 + g))
    cmp_spec = pl.BlockSpec((None, None, nc, HEAD_DIM), lambda i, g, qi: (i, g, 0, 0))
    return pl.pallas_call(
        functools.partial(_nsa_prompt_kernel, t_len=t, tq=tq),
        grid=(b, KV_HEADS, nq),
        in_specs=[
            pl.BlockSpec((tq, gw), lambda i, g, qi: (i * nq + qi, g)),
            kv_spec(2 * KV_HEADS), kv_spec(3 * KV_HEADS), kv_spec(0), kv_spec(KV_HEADS),
            cmp_spec, cmp_spec,
            pl.BlockSpec((tq, HEAD_DIM), lambda i, g, qi: (i * nq + qi, g)),
        ],
        out_specs=pl.BlockSpec((tq, gw), lambda i, g, qi: (i * nq + qi, g)),
        out_shape=jax.ShapeDtypeStruct((b * t, N_HEADS * HEAD_DIM), BF16),
        scratch_shapes=[pltpu.VMEM((t, HEAD_DIM), BF16)] * 4,
        compiler_params=_cparams(("parallel", "parallel", "arbitrary")),
        name="nsa_prompt_attn",
    )(q, rows, rows, win, win, k_cmp, v_cmp, gates)


def _nsa_sample_cmp_kernel(q_ref, kc_ref, vc_ref, o_ref, idx_ref, *, pos):
    nc = kc_ref.shape[1]
    cblk = _iota((GROUP, nc), 1)
    cmask = (cblk * CMP_BLOCK + (CMP_BLOCK - 1)) <= pos
    group_row = _iota((KV_HEADS, nc), 0)
    imp = jnp.zeros((KV_HEADS, nc), F32)
    for g in range(KV_HEADS):
        p = _masked_softmax(_dot_nt(q_ref[g], kc_ref[g]) * SCALE, cmask)
        o_ref[g] = _dot(p.astype(BF16), vc_ref[g])
        imp = jnp.where(group_row == g, jnp.sum(p, axis=0, keepdims=True), imp)
    cand = _iota((KV_HEADS, nc), 1) < (pos // CMP_BLOCK)
    idx_ref[...] = _topk_indices(jnp.where(cand, imp, NEG_INF), N_SELECT - 1)


def _nsa_sample_cmp(q4, k_cmp, v_cmp, pos):
    b = q4.shape[0]
    nc = k_cmp.shape[2]
    assert nc >= N_SELECT - 1 and pos // CMP_BLOCK >= N_SELECT - 1
    spec4 = pl.BlockSpec((None, KV_HEADS, GROUP, HEAD_DIM), lambda i: (i, 0, 0, 0))
    cspec = pl.BlockSpec((None, KV_HEADS, nc, HEAD_DIM), lambda i: (i, 0, 0, 0))
    return pl.pallas_call(
        functools.partial(_nsa_sample_cmp_kernel, pos=pos),
        grid=(b,),
        in_specs=[spec4, cspec, cspec],
        out_specs=[spec4, pl.BlockSpec((None, KV_HEADS, 128), lambda i: (i, 0, 0))],
        out_shape=[jax.ShapeDtypeStruct((b, KV_HEADS, GROUP, HEAD_DIM), F32),
                   jax.ShapeDtypeStruct((b, KV_HEADS, 128), jnp.int32)],
        compiler_params=_cparams(("parallel",)),
        name="nsa_sample_cmp",
    )(q4, k_cmp, v_cmp)


def _softmax_with_extra(s, s_new):
    m = jnp.maximum(jnp.max(s, axis=-1, keepdims=True), s_new)
    e, e_new = jnp.exp(s - m), jnp.exp(s_new - m)
    inv = 1.0 / (jnp.sum(e, axis=-1, keepdims=True) + e_new)
    return e * inv, e_new * inv


def _row_dot(q, k_row):
    return jnp.sum(q.astype(F32) * k_row.astype(BF16).astype(F32), axis=-1, keepdims=True)


def _nsa_sample_slc_kernel(pt_ref, idx_ref, q_ref, cache_ref, knew_ref, vnew_ref,
                           win_ref, kwnew_ref, vwnew_ref, ocmp_ref, gt_ref, o_ref,
                           slab, sems, *, layer, n_sel, halves, n_seq):
    g = pl.program_id(1)
    step = pl.program_id(0) * KV_HEADS + g
    per_row = 4 * KV_HEADS
    kv_rows = 2 * KV_HEADS
    rows = slab.shape[0] // 2
    blk_rows = CMP_BLOCK * per_row

    def copies(st, slot):
        b, gg = st // KV_HEADS, st % KV_HEADS
        out = []
        for n in range(n_sel):
            blk = idx_ref[b, gg * n_sel + n]
            src = cache_ref.at[layer, pt_ref[b, blk // halves] * halves + blk % halves]
            out.append(pltpu.make_async_copy(src, slab.at[pl.ds((slot * n_sel + n) * blk_rows, blk_rows)],
                                             sems.at[slot, n]))
        return out

    slot = _double_buffer(step, n_seq * KV_HEADS, copies)
    base = slot * rows
    n_keys = n_sel * CMP_BLOCK
    q = q_ref[...]
    k_sel = slab[pl.ds(base + 2 * KV_HEADS + g, n_keys, stride=per_row), :].astype(BF16)
    v_sel = slab[pl.ds(base + 3 * KV_HEADS + g, n_keys, stride=per_row), :].astype(BF16)
    p, p_new = _softmax_with_extra(_dot_nt(q, k_sel) * SCALE, _row_dot(q, knew_ref[...]) * SCALE)
    o_slc = _dot(p.astype(BF16), v_sel) + p_new.astype(BF16).astype(F32) * vnew_ref[...].astype(BF16).astype(F32)

    nw = win_ref.shape[0] // kv_rows
    kwin = win_ref[pl.ds(g, nw, stride=kv_rows), :].astype(BF16)
    vwin = win_ref[pl.ds(KV_HEADS + g, nw, stride=kv_rows), :].astype(BF16)
    sw = jnp.where(nw - _iota((GROUP, nw), 1) < WINDOW, _dot_nt(q, kwin) * SCALE, NEG_INF)
    p, p_new = _softmax_with_extra(sw, _row_dot(q, kwnew_ref[...]) * SCALE)
    o_win = _dot(p.astype(BF16), vwin) + p_new.astype(BF16).astype(F32) * vwnew_ref[...].astype(BF16).astype(F32)

    gt = gt_ref[...]
    rowi = _iota((GROUP, 1), 0)
    gcol = [sum(jnp.where(rowi == r, gt[:, 3 * r + c:3 * r + c + 1], 0.0) for r in range(GROUP)) for c in range(3)]
    o_ref[...] = gcol[0] * ocmp_ref[...] + gcol[1] * o_slc + gcol[2] * o_win


def _nsa_sample_slc(cache, layer, page_table, idx, q4, rows_s, win_buf, win_layer, win_s, o_cmp, gates_s, halves):
    b = q4.shape[0]
    n_sel = idx.shape[1] // KV_HEADS
    kv_rows = 2 * KV_HEADS
    spec4 = pl.BlockSpec((None, None, GROUP, HEAD_DIM), lambda i, g, pt, ix: (i, g, 0, 0))
    new_spec = lambda col0: pl.BlockSpec((None, None, 1, HEAD_DIM), lambda i, g, pt, ix: (i, col0 + g, 0, 0))
    grid_spec = pltpu.PrefetchScalarGridSpec(
        num_scalar_prefetch=2,
        grid=(b, KV_HEADS),
        in_specs=[
            spec4,
            pl.BlockSpec(memory_space=pl.ANY),
            new_spec(2 * KV_HEADS), new_spec(3 * KV_HEADS),
            pl.BlockSpec((None, None) + win_buf.shape[2:], lambda i, g, pt, ix: (win_layer, i, 0, 0)),
            new_spec(0), new_spec(KV_HEADS),
            spec4,
            pl.BlockSpec((None, None, 1, HEAD_DIM), lambda i, g, pt, ix: (i, g, 0, 0)),
        ],
        out_specs=spec4,
        scratch_shapes=[pltpu.VMEM((2 * n_sel * cache.shape[2], HEAD_DIM), F32),
                        pltpu.SemaphoreType.DMA((2, n_sel))],
    )
    return pl.pallas_call(
        functools.partial(_nsa_sample_slc_kernel, layer=layer, n_sel=n_sel, halves=halves, n_seq=b),
        grid_spec=grid_spec,
        out_shape=jax.ShapeDtypeStruct((b, KV_HEADS, GROUP, HEAD_DIM), F32),
        compiler_params=_cparams(("arbitrary", "arbitrary")),
        name="nsa_sample_slc",
    )(page_table, idx, q4, cache, rows_s, rows_s, win_buf, win_s, win_s, o_cmp, gates_s)


def _moba_prompt_kernel(q_ref, k_ref, v_ref, o_ref, k_s, v_s, km_s, *, t_len, tq):
    qi = pl.program_id(2)
    nb = km_s.shape[0]

    @pl.when(qi == 0)
    def _():
        k = k_ref[...]
        k_s[...] = k.astype(BF16)
        v_s[...] = v_ref[...].astype(BF16)
        km_s[...] = (jnp.sum(k.reshape(nb, MOBA_BLOCK, HEAD_DIM), axis=1) * (1.0 / MOBA_BLOCK)).astype(BF16)

    t0 = qi * tq
    tpos = t0 + _iota((tq, 1), 0)
    cur_t = (t0 + _iota((nb, tq), 1)) >> 8
    blk_t = _iota((nb, tq), 0)
    qb = q_ref[...]
    km = km_s[...]
    q_all = jnp.concatenate([qb[:, r * HEAD_DIM:(r + 1) * HEAD_DIM] for r in range(GROUP)], axis=0)
    gate_t_all = _dot_nt(km, q_all)
    sel_t = []
    for r in range(GROUP):
        sel = _rank_select(gate_t_all[:, r * tq:(r + 1) * tq], blk_t < cur_t, MOBA_TOPK, axis=0) | (blk_t == cur_t)
        sel_t.append(jnp.where(sel, 1.0, 0.0).astype(BF16))

    n_prefix = 4 if t_len % (4 * MOBA_BLOCK) == 0 else 1
    step = t_len // n_prefix
    for pi in range(n_prefix):
        @pl.when((t0 + tq - 1) // step == pi)
        def _(lk=(pi + 1) * step):
            expand = jnp.where((_iota((nb, lk), 1) >> 8) == _iota((nb, lk), 0), 1.0, 0.0).astype(BF16)
            causal = _iota((tq, lk), 1) <= tpos
            picked = jnp.concatenate(
                [lax.dot_general(sel_t[r], expand, (((0,), (0,)), ((), ())), preferred_element_type=F32)
                 for r in range(GROUP)], axis=0).reshape(GROUP, tq, lk)
            o = _attend_group(q_all, k_s[0:lk, :], v_s[0:lk, :], (picked > 0.5) & causal[None])
            for r in range(GROUP):
                o_ref[:, r * HEAD_DIM:(r + 1) * HEAD_DIM] = o[r * tq:(r + 1) * tq].astype(o_ref.dtype)


def _moba_prompt_attn(q, kv, b, t, tq=256):
    assert t % MOBA_BLOCK == 0
    nq = t // tq
    gw = GROUP * HEAD_DIM
    kv_spec = lambda col0: pl.BlockSpec((t, HEAD_DIM), lambda i, g, qi: (i, col0 + g))
    return pl.pallas_call(
        functools.partial(_moba_prompt_kernel, t_len=t, tq=tq),
        grid=(b, KV_HEADS, nq),
        in_specs=[pl.BlockSpec((tq, gw), lambda i, g, qi: (i * nq + qi, g)), kv_spec(0), kv_spec(KV_HEADS)],
        out_specs=pl.BlockSpec((tq, gw), lambda i, g, qi: (i * nq + qi, g)),
        out_shape=jax.ShapeDtypeStruct((b * t, N_HEADS * HEAD_DIM), BF16),
        scratch_shapes=[pltpu.VMEM((t, HEAD_DIM), BF16), pltpu.VMEM((t, HEAD_DIM), BF16),
                        pltpu.VMEM((t // MOBA_BLOCK, HEAD_DIM), BF16)],
        compiler_params=_cparams(("parallel", "parallel", "arbitrary")),
        name="moba_prompt_attn",
    )(q, kv, kv)


def _moba_sample_gate_kernel(pt_ref, q_ref, cache_ref, idx_ref, slab, sems, ksum, *, layer, pages, ppb, n_seq, n_chunks):
    c = pl.program_id(1)
    step = pl.program_id(0) * n_chunks + c
    per_row = 2 * KV_HEADS
    rows = slab.shape[0] // 2
    page_rows = rows // pages

    def copies(st, slot):
        b, cc = st // n_chunks, st % n_chunks
        return [pltpu.make_async_copy(cache_ref.at[layer, pt_ref[b, cc * pages + p]],
                                      slab.at[pl.ds((slot * pages + p) * page_rows, page_rows)], sems.at[slot, p])
                for p in range(pages)]

    slot = _double_buffer(step, n_seq * n_chunks, copies)
    x = slab[pl.ds(pl.multiple_of(slot * rows, rows), rows), :]
    n_blk = pages // ppb
    sums = jnp.sum(x.reshape(n_blk, rows // (n_blk * per_row), per_row, HEAD_DIM), axis=1)
    ksum[pl.ds(pl.multiple_of(c * (n_blk * per_row), n_blk * per_row), n_blk * per_row), :] = sums.reshape(
        n_blk * per_row, HEAD_DIM)

    @pl.when(c == n_chunks - 1)
    def _():
        nb = ksum.shape[0] // per_row
        for g in range(KV_HEADS):
            km = (ksum[pl.ds(g, nb, stride=per_row), :] * (1.0 / MOBA_BLOCK)).astype(BF16)
            idx_ref[g] = _topk_indices(_dot_nt(q_ref[g], km), MOBA_TOPK)


def _moba_sample_gate(cache, layer, page_table, q4, pages=16):
    b, n_pages = page_table.shape
    per_row = 2 * KV_HEADS
    page_rows = cache.shape[2]
    psz = page_rows // per_row
    ppb = MOBA_BLOCK // psz
    nb = n_pages // ppb
    pages = min(pages, n_pages)
    assert MOBA_BLOCK % psz == 0 and n_pages % pages == 0 and pages % ppb == 0 and nb >= MOBA_TOPK
    grid_spec = pltpu.PrefetchScalarGridSpec(
        num_scalar_prefetch=1,
        grid=(b, n_pages // pages),
        in_specs=[
            pl.BlockSpec((None, KV_HEADS, GROUP, HEAD_DIM), lambda i, c, pt: (i, 0, 0, 0)),
            pl.BlockSpec(memory_space=pl.ANY),
        ],
        out_specs=pl.BlockSpec((None, KV_HEADS, GROUP, 128), lambda i, c, pt: (i, 0, 0, 0)),
        scratch_shapes=[pltpu.VMEM((2 * pages * page_rows, HEAD_DIM), F32), pltpu.SemaphoreType.DMA((2, pages)),
                        pltpu.VMEM((nb * per_row, HEAD_DIM), F32)],
    )
    return pl.pallas_call(
        functools.partial(_moba_sample_gate_kernel, layer=layer, pages=pages, ppb=ppb, n_seq=b,
                          n_chunks=n_pages // pages),
        grid_spec=grid_spec,
        out_shape=jax.ShapeDtypeStruct((b, KV_HEADS, GROUP, 128), jnp.int32),
        compiler_params=_cparams(("arbitrary", "arbitrary")),
        name="moba_sample_gate",
    )(page_table, q4, cache)


def _moba_sample_attn_kernel(pt_ref, idx_ref, q_ref, cache_ref, knew_ref, vnew_ref, o_ref, slab, sems,
                             *, layer, n_sel, ppb, n_seq):
    h = pl.program_id(1)
    step = pl.program_id(0) * N_HEADS + h
    per_row = 2 * KV_HEADS
    n_pg = n_sel * ppb
    rows = slab.shape[0] // 2
    page_rows = rows // n_pg

    def copies(st, slot):
        b, hh = st // N_HEADS, st % N_HEADS
        out = []
        for n in range(n_pg):
            page = pt_ref[b, idx_ref[b, hh * n_sel + n // ppb] * ppb + n % ppb]
            out.append(pltpu.make_async_copy(cache_ref.at[layer, page],
                                             slab.at[pl.ds((slot * n_pg + n) * page_rows, page_rows)],
                                             sems.at[slot, n]))
        return out

    slot = _double_buffer(step, n_seq * N_HEADS, copies)
    base = slot * rows
    g = h // GROUP
    n_keys = rows // per_row
    q = q_ref[...]
    k_sel = slab[pl.ds(base + g, n_keys, stride=per_row), :].astype(BF16)
    v_sel = slab[pl.ds(base + KV_HEADS + g, n_keys, stride=per_row), :].astype(BF16)
    p, p_new = _softmax_with_extra(_dot_nt(q, k_sel) * SCALE, _row_dot(q, knew_ref[...]) * SCALE)
    o = _dot(p.astype(BF16), v_sel) + p_new.astype(BF16).astype(F32) * vnew_ref[...].astype(BF16).astype(F32)
    o_ref[...] = o.astype(o_ref.dtype)


def _moba_sample_attn(cache, layer, page_table, idx, q16, kv_new):
    b = q16.shape[0]
    page_rows = cache.shape[2]
    ppb = MOBA_BLOCK // (page_rows // (2 * KV_HEADS))
    n_sel = idx.shape[1] // N_HEADS
    n_pg = n_sel * ppb
    hspec = pl.BlockSpec((None, None, 1, HEAD_DIM), lambda i, h, pt, ix: (i, h, 0, 0))
    new_spec = lambda col0: pl.BlockSpec((None, None, 1, HEAD_DIM), lambda i, h, pt, ix: (i, col0 + h // GROUP, 0, 0))
    grid_spec = pltpu.PrefetchScalarGridSpec(
        num_scalar_prefetch=2,
        grid=(b, N_HEADS),
        in_specs=[hspec, pl.BlockSpec(memory_space=pl.ANY), new_spec(0), new_spec(KV_HEADS)],
        out_specs=hspec,
        scratch_shapes=[pltpu.VMEM((2 * n_pg * page_rows, HEAD_DIM), F32), pltpu.SemaphoreType.DMA((2, n_pg))],
    )
    return pl.pallas_call(
        functools.partial(_moba_sample_attn_kernel, layer=layer, n_sel=n_sel, ppb=ppb, n_seq=b),
        grid_spec=grid_spec,
        out_shape=jax.ShapeDtypeStruct((b, N_HEADS, 1, HEAD_DIM), BF16),
        compiler_params=_cparams(("arbitrary", "arbitrary")),
        name="moba_sample_attn",
    )(page_table, idx, q16, cache, kv_new, kv_new)


def _log_sigmoid(z):
    return jnp.minimum(z, 0.0) - jnp.log(1.0 + jnp.exp(-jnp.abs(z)))


def _split2(x):
    hi = x.astype(BF16)
    return hi, (x - hi.astype(F32)).astype(BF16)


EXP_UNDERFLOW = -104.0


def _sb_prompt_kernel(q_ref, k_ref, v_ref, o_ref, k_s, v_s, acc_scr, *, tq):
    qi = pl.program_id(2)

    @pl.when(qi == 0)
    def _():
        k_s[...] = k_ref[...].astype(BF16)
        v_s[...] = v_ref[...].astype(BF16)

    hp = acc_scr.shape[0]
    lanes = lambda h: slice(h * HEAD_DIM, (h + 1) * HEAD_DIM)
    qs = [q_ref[:, lanes(h)].astype(BF16) for h in range(hp)]
    after = jnp.where(_iota((tq, tq), 0) > _iota((tq, tq), 1), 1.0, 0.0).astype(BF16)
    below = (_iota((hp * tq, tq), 0) & (tq - 1)) > _iota((hp * tq, tq), 1)
    acc_scr[...] = jnp.zeros_like(acc_scr)

    def chunk(c, run, diagonal):
        s0 = pl.multiple_of(c * tq, tq)
        z = jnp.concatenate([_dot_nt(qs[h], k_s[pl.ds(s0, tq), lanes(h)]) for h in range(hp)], axis=0) * SCALE
        log_beta = _log_sigmoid(z)
        log_keep = log_beta - z
        if diagonal:
            log_keep = jnp.where(below, log_keep, 0.0)
        both = _dot(jnp.concatenate(_split2(log_keep), axis=0), after)
        within = both[:hp * tq] + both[hp * tq:]
        a = jnp.exp(log_beta + within + run)
        if diagonal:
            a = jnp.where(below, a, 0.0)
        a = a.astype(BF16)
        for h in range(hp):
            acc_scr[h] += _dot(a[h * tq:(h + 1) * tq], v_s[pl.ds(s0, tq), lanes(h)])
        return run + within[:, 0:1] + log_keep[:, 0:1]

    run = chunk(qi, jnp.zeros((hp * tq, 1), F32), True)

    def cond(carry):
        c, run = carry
        return (c >= 0) & (jnp.max(run) >= EXP_UNDERFLOW)

    def body(carry):
        c, run = carry
        return c - 1, chunk(c, run, False)

    lax.while_loop(cond, body, (qi - 1, run))
    for h in range(hp):
        o_ref[:, lanes(h)] = acc_scr[h].astype(o_ref.dtype)


def _sb_prompt_attn(qkv, b, t, tq=256, hp=2):
    nq = t // tq
    n_hp = N_HEADS // hp
    w = hp * HEAD_DIM
    assert tq & (tq - 1) == 0
    return pl.pallas_call(
        functools.partial(_sb_prompt_kernel, tq=tq),
        grid=(b, n_hp, nq),
        in_specs=[
            pl.BlockSpec((tq, w), lambda i, h, qi: (i * nq + qi, h)),
            pl.BlockSpec((t, w), lambda i, h, qi: (i, n_hp + h)),
            pl.BlockSpec((t, w), lambda i, h, qi: (i, 2 * n_hp + h)),
        ],
        out_specs=pl.BlockSpec((tq, w), lambda i, h, qi: (i * nq + qi, h)),
        out_shape=jax.ShapeDtypeStruct((b * t, N_HEADS * HEAD_DIM), BF16),
        scratch_shapes=[pltpu.VMEM((t, w), BF16)] * 2 + [pltpu.VMEM((hp, tq, HEAD_DIM), F32)],
        compiler_params=_cparams(("parallel", "parallel", "arbitrary")),
        name="sb_prompt_attn",
    )(qkv, qkv, qkv)


def _sb_sample_kernel(pt_ref, qbd_ref, cache_ref, o_ref, buf, sems, acc_scr, *, layer, n_pages):
    b = pl.program_id(0)
    per_row = 2 * N_HEADS
    rows = buf.shape[0] // 2
    psz = rows // per_row

    def page_copy(p, slot):
        return pltpu.make_async_copy(cache_ref.at[layer, pt_ref[b, n_pages - 1 - p]],
                                     buf.at[pl.ds(slot * rows, rows)], sems.at[slot])

    acc_scr[...] = jnp.zeros_like(acc_scr)
    after = jnp.where(_iota((psz, psz), 1) > _iota((psz, psz), 0), 1.0, 0.0).astype(BF16)
    live = _iota((1, 128), 1) < N_HEADS
    page_copy(0, 0).start()

    def cond(carry):
        p, run = carry
        return (p < n_pages) & (jnp.max(jnp.where(live, run, NEG_INF)) >= EXP_UNDERFLOW)

    def body(carry):
        p, run = carry
        slot = p % 2
        page_copy(p, slot).wait()

        @pl.when(p + 1 < n_pages)
        def _():
            page_copy(p + 1, 1 - slot).start()

        base = slot * rows

        def heads_wide(first):
            return jnp.concatenate([buf[pl.ds(base + first + h, psz, stride=per_row), :].astype(BF16)
                                    for h in range(N_HEADS)], axis=1)

        z = _dot(heads_wide(0), qbd_ref[...]) * SCALE
        log_beta = _log_sigmoid(z)
        log_keep = log_beta - z
        hi, lo = _split2(log_keep)
        within = _dot(after, hi) + _dot(after, lo)
        a = jnp.exp(log_beta + within + run)
        acc_scr[...] += _dot(a.T.astype(BF16), heads_wide(N_HEADS))
        return p + 1, run + within[0:1, :] + log_keep[0:1, :]

    p_end, _ = lax.while_loop(cond, body, (0, jnp.zeros((1, 128), F32)))

    @pl.when(p_end < n_pages)
    def _():
        page_copy(p_end, p_end % 2).wait()

    acc = acc_scr[...]
    own = _iota(acc.shape, 0) == (_iota(acc.shape, 1) >> 7)
    o_ref[...] = jnp.sum(jnp.where(own, acc, 0.0), axis=0, keepdims=True)


def _sb_sample_attn(cache, layer, page_table, qbd):
    b, n_pages = page_table.shape
    aw = N_HEADS * HEAD_DIM
    grid_spec = pltpu.PrefetchScalarGridSpec(
        num_scalar_prefetch=1,
        grid=(b,),
        in_specs=[
            pl.BlockSpec((None, aw, 128), lambda i, pt: (i, 0, 0)),
            pl.BlockSpec(memory_space=pl.ANY),
        ],
        out_specs=pl.BlockSpec((None, 1, aw), lambda i, pt: (i, 0, 0)),
        scratch_shapes=[pltpu.VMEM((2 * cache.shape[2], HEAD_DIM), F32), pltpu.SemaphoreType.DMA((2,)),
                        pltpu.VMEM((128, aw), F32)],
    )
    return pl.pallas_call(
        functools.partial(_sb_sample_kernel, layer=layer, n_pages=n_pages),
        grid_spec=grid_spec,
        out_shape=jax.ShapeDtypeStruct((b, 1, aw), F32),
        compiler_params=_cparams(("arbitrary",)),
        name="sb_sample_attn",
    )(page_table, qbd, cache)


def _tile_rows(m, tile=512):
    return tile if m % tile == 0 else m


PROJ_ROWS = 1024


def _nsa_weights(w_in, q_gain, ks_gain, kw_gain):
    d = w_in.shape[0]
    main = N_HEADS * HEAD_DIM + 6 * KV_HEADS * HEAD_DIM
    wg = w_in[:, main:].reshape(d, KV_HEADS, GROUP * 3)
    wg = jnp.pad(wg, ((0, 0), (0, 0), (0, HEAD_DIM - GROUP * 3))).reshape(d, KV_HEADS * HEAD_DIM)
    w = jnp.concatenate([w_in[:, :main], wg], axis=1).astype(BF16)
    kvh = KV_HEADS
    tiles = [("rope", 0, None)] * 4 + [
        ("plain", 1, (0, 0)), ("plain", 1, (0, kvh)), ("rope", 1, (0, 2 * kvh)), ("plain", 1, (0, 3 * kvh)),
        ("rope", 2, (1, 0)), ("plain", 2, (1, kvh)), ("sigmoid", 3, None)]
    ones = jnp.ones((HEAD_DIM,), F32)
    hg = jnp.stack([q_gain] * 4 + [ones, ones, ks_gain, ones, kw_gain, ones, ones])[:, None, :]
    outs = [(4, BF16), (4, F32), (2, F32), (1, F32)]
    return w, tiles, hg, outs


def _nsa_layer(xp, xs, b, t, cache, layer, page_table, win_all, norm_gain, w_in, w_out, q_gain, kc_gain,
               ks_gain, kw_gain, cmp_pos, cmp_w1, cmp_w2):
    bs = xs.shape[0]
    n_pages, psz = page_table.shape[1], cache.shape[2]
    past = n_pages * psz
    w, tiles, hg, outs = _nsa_weights(w_in, q_gain, ks_gain, kw_gain)
    w1, w2 = cmp_w1.astype(BF16), cmp_w2.astype(BF16)
    w_out = w_out.astype(BF16)

    cos_p, sin_p = _rope_tables(jnp.arange(t, dtype=jnp.int32))
    cos_pt, sin_pt = jnp.tile(cos_p, (b, 1)), jnp.tile(sin_p, (b, 1))
    cache_rows = (4 * KV_HEADS, 2 * KV_HEADS)
    q, rows, win, gates, rows_c, win_c = _norm_proj(xp, norm_gain, w, hg, cos_pt, sin_pt, tiles, outs,
                                                    _tile_rows(b * t, PROJ_ROWS), cache_rows=cache_rows)
    nc = t // CMP_BLOCK
    cos_c, sin_c = _rope_tables(jnp.arange(nc, dtype=jnp.int32) * CMP_BLOCK + (CMP_BLOCK - 1))
    k_cmp, v_cmp = _compress_prompt(rows, b, t, cmp_pos, w1, w2, kc_gain, cos_c, sin_c)
    o = _nsa_prompt_attn(q, rows, win, k_cmp, v_cmp, gates, b, t)
    xp = _out_proj(o, w_out, xp, _tile_rows(b * t, PROJ_ROWS))

    cos_s, sin_s = _rope_tables(jnp.full((bs,), past, jnp.int32))
    qs, _, _, gates_s, rows_s, win_s = _norm_proj(xs, norm_gain, w, hg, cos_s, sin_s, tiles, outs, bs,
                                                  cache_rows=cache_rows)
    ncs = past // CMP_BLOCK
    cos_cs, sin_cs = _rope_tables(jnp.arange(ncs, dtype=jnp.int32) * CMP_BLOCK + (CMP_BLOCK - 1))
    n_layers, n_pool = cache.shape[:2]
    k_cmp_s, v_cmp_s = _compress_sample(cache.reshape(n_layers, n_pool, psz, 2, 2 * KV_HEADS, HEAD_DIM), layer,
                                        page_table, cmp_pos, w1, w2, kc_gain, cos_cs, sin_cs)
    q4 = qs.reshape(bs, KV_HEADS, GROUP, HEAD_DIM)
    o_cmp, idx = _nsa_sample_cmp(q4, k_cmp_s, v_cmp_s, past)
    idx = idx[:, :, :N_SELECT - 1].reshape(bs, KV_HEADS * (N_SELECT - 1))
    halves = psz // CMP_BLOCK
    block_view = cache.reshape(n_layers, n_pool * halves, CMP_BLOCK * 4 * KV_HEADS, HEAD_DIM)
    nw = win_all.shape[2]
    win_view = win_all.reshape(win_all.shape[0], bs, nw * 2 * KV_HEADS, HEAD_DIM)
    o_s = _nsa_sample_slc(block_view, layer, page_table, idx, q4,
                          rows_s.reshape(bs, 4 * KV_HEADS, 1, HEAD_DIM), win_view, layer,
                          win_s.reshape(bs, 2 * KV_HEADS, 1, HEAD_DIM), o_cmp,
                          gates_s.reshape(bs, KV_HEADS, 1, HEAD_DIM), halves)
    xs = _out_proj(o_s.reshape(bs, N_HEADS * HEAD_DIM).astype(BF16), w_out, xs, bs)

    kv_p = rows_c.reshape(b, t, 4, KV_HEADS, HEAD_DIM)
    kv_s = rows_s.reshape(bs, 1, 4, KV_HEADS, HEAD_DIM)
    win_p = win_c.reshape(b, t, 2, KV_HEADS, HEAD_DIM)[:, -min(WINDOW, t):]
    win_new = jnp.concatenate([win_all[layer], win_s.reshape(bs, 1, 2, KV_HEADS, HEAD_DIM)], axis=1)[:, -nw:]
    return xp, xs, kv_p, kv_s, win_p, win_new


def _moba_layer(xp, xs, b, t, cache, layer, page_table, norm_gain, w_in, w_out, q_gain, k_gain):
    bs = xs.shape[0]
    n_pages, psz = page_table.shape[1], cache.shape[2]
    past = n_pages * psz
    w = w_in.astype(BF16)
    w_out = w_out.astype(BF16)
    tiles = [("rope", 0, None)] * 4 + [("rope", 1, (0, 0)), ("plain", 1, (0, KV_HEADS))]
    cache_rows = (2 * KV_HEADS,)
    ones = jnp.ones((HEAD_DIM,), F32)
    hg = jnp.stack([q_gain] * 4 + [k_gain, ones])[:, None, :]
    outs = [(4, BF16), (2, F32)]

    cos_p, sin_p = _rope_tables(jnp.arange(t, dtype=jnp.int32))
    cos_pt, sin_pt = jnp.tile(cos_p, (b, 1)), jnp.tile(sin_p, (b, 1))
    q, kv, kv_c = _norm_proj(xp, norm_gain, w, hg, cos_pt, sin_pt, tiles, outs, _tile_rows(b * t, PROJ_ROWS),
                             cache_rows=cache_rows)
    o = _moba_prompt_attn(q, kv, b, t)
    xp = _out_proj(o, w_out, xp, _tile_rows(b * t, PROJ_ROWS))

    cos_s, sin_s = _rope_tables(jnp.full((bs,), past, jnp.int32))
    qs, _, kv_s = _norm_proj(xs, norm_gain, w, hg, cos_s, sin_s, tiles, outs, bs, cache_rows=cache_rows)
    cache2 = cache.reshape(cache.shape[0], cache.shape[1], psz * 2 * KV_HEADS, HEAD_DIM)
    idx = _moba_sample_gate(cache2, layer, page_table, qs.reshape(bs, KV_HEADS, GROUP, HEAD_DIM))
    idx = idx.reshape(bs, N_HEADS, 128)[:, :, :MOBA_TOPK].reshape(bs, N_HEADS * MOBA_TOPK)
    o_s = _moba_sample_attn(cache2, layer, page_table, idx, qs.reshape(bs, N_HEADS, 1, HEAD_DIM),
                            kv_s.reshape(bs, 2 * KV_HEADS, 1, HEAD_DIM))
    xs = _out_proj(o_s.reshape(bs, N_HEADS * HEAD_DIM), w_out, xs, bs)
    return (xp, xs, kv_c.reshape(b, t, 2, KV_HEADS, HEAD_DIM), kv_s.reshape(bs, 1, 2, KV_HEADS, HEAD_DIM))


def _sb_layer(xp, xs, b, t, cache, layer, page_table, norm_gain, w_in, w_out):
    bs = xs.shape[0]
    psz = cache.shape[2]
    aw = N_HEADS * HEAD_DIM
    w = w_in.astype(BF16)
    w_out = w_out.astype(BF16)
    n_tiles = w.shape[1] // 512
    heads_per_tile = 512 // HEAD_DIM
    q_tiles = aw // 512
    tiles = [("plain", 0, None)] * q_tiles + [("plain", 0, (0, (jt - q_tiles) * heads_per_tile))
                                              for jt in range(q_tiles, n_tiles)]
    cache_rows = (2 * N_HEADS,)
    hg = jnp.ones((n_tiles, 1, HEAD_DIM), F32)
    outs = [(n_tiles, F32)]
    dummy = jnp.zeros((b * t, HEAD_DIM), F32)
    qkv, kv_c = _norm_proj(xp, norm_gain, w, hg, dummy, dummy, tiles, outs, _tile_rows(b * t, PROJ_ROWS),
                           cache_rows=cache_rows)
    o = _sb_prompt_attn(qkv, b, t)
    xp = _out_proj(o, w_out, xp, _tile_rows(b * t, PROJ_ROWS))

    qkv_s, kv_s = _norm_proj(xs, norm_gain, w, hg, dummy[:bs], dummy[:bs], tiles, outs, bs, cache_rows=cache_rows)
    qs = qkv_s[:, :aw].astype(BF16).reshape(bs, N_HEADS, HEAD_DIM)
    eye = jnp.eye(N_HEADS, 128, dtype=BF16)
    qbd = (qs[:, :, :, None] * eye[None, :, None, :]).reshape(bs, aw, 128)
    cache2 = cache.reshape(cache.shape[0], cache.shape[1], psz * 2 * N_HEADS, HEAD_DIM)
    o_s = _sb_sample_attn(cache2, layer, page_table, qbd)
    xs = _out_proj(o_s.reshape(bs, aw).astype(BF16), w_out, xs, bs)
    return (xp, xs, kv_c.reshape(b, t, 2, N_HEADS, HEAD_DIM), kv_s.reshape(bs, 1, 2, N_HEADS, HEAD_DIM))


def kernel(x_prompt, x_sample, cache_nsa_kv, state_nsa_win, cache_moba_kv, cache_sb_kv, page_table, mix_norm, ffn_norm, nsa_w_in, nsa_w_out, nsa_q_gain, nsa_kc_gain, nsa_ks_gain, nsa_kw_gain, nsa_cmp_pos, nsa_cmp_w1, nsa_cmp_w2, moba_w_in, moba_w_out, moba_q_gain, moba_k_gain, sb_w_in, sb_w_out, ffn_w_gate, ffn_w_up, ffn_w_down):
    b, t, d = x_prompt.shape
    bs, ts, _ = x_sample.shape
    assert ts == 1
    depth = mix_norm.shape[0]
    xp = x_prompt.reshape(b * t, d)
    xs = x_sample.reshape(bs * ts, d)
    nsa_kv_p, nsa_kv_s, nsa_win_p, nsa_win_s = [], [], [], []
    moba_kv_p, moba_kv_s, sb_kv_p, sb_kv_s = [], [], [], []
    wg, wu, wd = ffn_w_gate.astype(BF16), ffn_w_up.astype(BF16), ffn_w_down.astype(BF16)
    for i in range(depth):
        j = i // N_MIXERS
        if i % N_MIXERS == 0:
            xp, xs, kvp, kvs, wp, ws = _nsa_layer(
                xp, xs, b, t, cache_nsa_kv, j, page_table, state_nsa_win, mix_norm[i], nsa_w_in[j], nsa_w_out[j],
                nsa_q_gain[j], nsa_kc_gain[j], nsa_ks_gain[j], nsa_kw_gain[j], nsa_cmp_pos[j], nsa_cmp_w1[j],
                nsa_cmp_w2[j])
            nsa_kv_p.append(kvp)
            nsa_kv_s.append(kvs)
            nsa_win_p.append(wp)
            nsa_win_s.append(ws)
        elif i % N_MIXERS == 1:
            xp, xs, kvp, kvs = _moba_layer(xp, xs, b, t, cache_moba_kv, j, page_table, mix_norm[i], moba_w_in[j],
                                           moba_w_out[j], moba_q_gain[j], moba_k_gain[j])
            moba_kv_p.append(kvp)
            moba_kv_s.append(kvs)
        else:
            xp, xs, kvp, kvs = _sb_layer(xp, xs, b, t, cache_sb_kv, j, page_table, mix_norm[i], sb_w_in[j], sb_w_out[j])
            sb_kv_p.append(kvp)
            sb_kv_s.append(kvs)
        xp = _ffn(xp, ffn_norm[i], wg, wu, wd, i, _tile_rows(b * t))
        xs = _ffn(xs, ffn_norm[i], wg, wu, wd, i, bs * ts)
    return (xp.reshape(b, t, d), xs.reshape(bs, ts, d), jnp.stack(nsa_kv_p), jnp.stack(nsa_kv_s),
            jnp.stack(nsa_win_p), jnp.stack(nsa_win_s), jnp.stack(moba_kv_p), jnp.stack(moba_kv_s),
            jnp.stack(sb_kv_p), jnp.stack(sb_kv_s))
```

```python
import functools
import math

import jax
import jax.numpy as jnp
from jax import lax
from jax.experimental import pallas as pl
from jax.experimental.pallas import tpu as pltpu

N_HEADS = 16
HEAD_DIM = 128
KV_HEADS = 4
GROUP = N_HEADS // KV_HEADS
CMP_BLOCK = 64
N_SELECT = 16
WINDOW = 512
MOBA_BLOCK = 256
MOBA_TOPK = 3
N_MIXERS = 3
ROPE_THETA = 10000.0
RMS_EPS = 1e-6
NEG_INF = -1e30
REMOVED = -3e38
SCALE = HEAD_DIM ** -0.5
LOG2_E = math.log2(math.e)

VMEM_LIMIT = 56 * 1024 * 1024
F32 = jnp.float32
BF16 = jnp.bfloat16


def _cparams(sem):
    return pltpu.CompilerParams(dimension_semantics=sem, vmem_limit_bytes=VMEM_LIMIT)


def _dot(a, b):
    return jnp.dot(a, b, preferred_element_type=F32)


def _dot_nt(a, b):
    return lax.dot_general(a, b, (((1,), (1,)), ((), ())), preferred_element_type=F32)


def _iota(shape, dim):
    return lax.broadcasted_iota(jnp.int32, shape, dim)


def _masked_softmax(s, mask):
    s = jnp.where(mask, s, NEG_INF)
    m = jnp.max(s, axis=-1, keepdims=True)
    e = jnp.exp(s - m)
    p = e * (1.0 / jnp.sum(e, axis=-1, keepdims=True))
    return jnp.where(mask, p, 0.0)


def _attend_group(q_all, k, v, mask):
    n = k.shape[0]
    m = mask.shape[-2]
    heads = q_all.shape[0] // m
    s = _dot_nt(q_all, k).reshape(heads, m, n) * (SCALE * LOG2_E)
    s = jnp.where(mask if mask.ndim == 3 else mask[None], s, NEG_INF)
    e = jnp.exp2(s - jnp.max(s, axis=-1, keepdims=True))
    inv = 1.0 / jnp.sum(e, axis=-1, keepdims=True)
    return _dot(e.reshape(heads * m, n).astype(BF16), v) * inv.reshape(heads * m, 1)


def _head_norm_rope(x, gain, cos, sin):
    y = x * lax.rsqrt(jnp.mean(x * x, axis=-1, keepdims=True) + RMS_EPS) * gain
    return y * cos + pltpu.roll(y, HEAD_DIM // 2, 1) * sin


def _rope_tables(pos):
    half = HEAD_DIM // 2
    inv_freq = ROPE_THETA ** (-jnp.arange(half, dtype=F32) / half)
    ang = pos.astype(F32)[:, None] * inv_freq[None, :]
    cos, sin = jnp.cos(ang), jnp.sin(ang)
    return jnp.concatenate([cos, cos], axis=-1), jnp.concatenate([-sin, sin], axis=-1)


def _rank_select(val, cand, k, axis):
    n = val.shape[axis]
    idx = _iota(val.shape, axis)
    v = jnp.where(cand, val, NEG_INF)
    rank = jnp.zeros(val.shape, jnp.int32)
    for j in range(n):
        vj = lax.slice_in_dim(v, j, j + 1, axis=axis)
        ahead = (vj > v) | ((vj == v) & (idx > j))
        rank = rank + jnp.where(ahead, 1, 0)
    return cand & (rank < k)


def _topk_indices(val, k):
    rows, n = val.shape
    lane = _iota((rows, n), 1).astype(F32)
    out_lane = _iota((rows, 128), 1)
    out = jnp.zeros((rows, 128), F32)
    for i in range(k):
        m = jnp.max(val, axis=-1, keepdims=True)
        pick = jnp.min(jnp.where(val == m, lane, float(n)), axis=-1, keepdims=True)
        out = jnp.where(out_lane == i, pick, out)
        val = jnp.where(lane == pick, REMOVED, val)
    return out.astype(jnp.int32)


def _double_buffer(step, n_steps, copies):
    slot = step % 2

    @pl.when(step == 0)
    def _():
        for c in copies(step, slot):
            c.start()

    @pl.when(step + 1 < n_steps)
    def _():
        for c in copies(step + 1, 1 - slot):
            c.start()

    for c in copies(step, slot):
        c.wait()
    return slot


def _proj_kernel(x_ref, g_ref, w_ref, hg_ref, cos_ref, sin_ref, *refs, tiles, n_out, cache_rows):
    out_refs, cache_refs, h_scr = refs[:n_out], refs[n_out:n_out + len(cache_rows)], refs[-1]
    j = pl.program_id(1)
    tm = x_ref.shape[0]

    @pl.when(j == 0)
    def _():
        x = x_ref[...]
        h = x * lax.rsqrt(jnp.mean(x * x, axis=-1, keepdims=True) + RMS_EPS) * g_ref[...]
        h_scr[...] = h.astype(BF16)

    acc = _dot(h_scr[...], w_ref[...])
    for jt, (mode, oi, cache_at) in enumerate(tiles):
        @pl.when(j == jt)
        def _(mode=mode, oi=oi, cache_at=cache_at):
            o_ref = out_refs[oi]
            for hh in range(acc.shape[1] // HEAD_DIM):
                sl = slice(hh * HEAD_DIM, (hh + 1) * HEAD_DIM)
                y = acc[:, sl]
                if mode == "rope":
                    y = _head_norm_rope(y, hg_ref[0], cos_ref[...], sin_ref[...])
                elif mode == "sigmoid":
                    y = 1.0 / (1.0 + jnp.exp(-y))
                o_ref[:, sl] = y.astype(o_ref.dtype)
                if cache_at is not None:
                    ci, first = cache_at
                    cache_refs[ci][pl.ds(first + hh, tm, stride=cache_rows[ci]), :] = y


def _norm_proj(x, gain, w, head_gains, cos, sin, tiles, outs, tm, tn=512, cache_rows=()):
    m, d = x.shape
    n_tiles = len(tiles)
    assert w.shape == (d, n_tiles * tn) and m % tm == 0
    starts = []
    for oi in range(len(outs)):
        starts.append(min(jt for jt, t in enumerate(tiles) if t[1] == oi))

    def out_map(oi):
        s, cnt = starts[oi], outs[oi][0]
        return lambda i, j: (i, jnp.clip(j - s, 0, cnt - 1))

    return pl.pallas_call(
        functools.partial(_proj_kernel, tiles=tuple(tiles), n_out=len(outs), cache_rows=tuple(cache_rows)),
        grid=(m // tm, n_tiles),
        in_specs=[
            pl.BlockSpec((tm, d), lambda i, j: (i, 0)),
            pl.BlockSpec((1, d), lambda i, j: (0, 0)),
            pl.BlockSpec((d, tn), lambda i, j: (0, j)),
            pl.BlockSpec((1, 1, HEAD_DIM), lambda i, j: (j, 0, 0)),
            pl.BlockSpec((tm, HEAD_DIM), lambda i, j: (i, 0)),
            pl.BlockSpec((tm, HEAD_DIM), lambda i, j: (i, 0)),
        ],
        out_specs=[pl.BlockSpec((tm, tn), out_map(oi)) for oi in range(len(outs))] + [
            pl.BlockSpec((tm * r, HEAD_DIM), lambda i, j: (i, 0), pipeline_mode=pl.Buffered(1)) for r in cache_rows],
        out_shape=[jax.ShapeDtypeStruct((m, cnt * tn), dt) for cnt, dt in outs] + [
            jax.ShapeDtypeStruct((m * r, HEAD_DIM), F32) for r in cache_rows],
        scratch_shapes=[pltpu.VMEM((tm, d), BF16)],
        compiler_params=_cparams(("parallel", "arbitrary")),
        name="norm_proj",
    )(x, gain.reshape(1, d), w, head_gains, cos, sin)


def _out_proj_kernel(o_ref, w_ref, x_ref, y_ref):
    y_ref[...] = x_ref[...] + _dot(o_ref[...], w_ref[...])


def _out_proj(o, w, x, tm, tn=512):
    m, k = o.shape
    n = w.shape[1]
    return pl.pallas_call(
        _out_proj_kernel,
        grid=(m // tm, n // tn),
        in_specs=[
            pl.BlockSpec((tm, k), lambda i, j: (i, 0)),
            pl.BlockSpec((k, tn), lambda i, j: (0, j)),
            pl.BlockSpec((tm, tn), lambda i, j: (i, j)),
        ],
        out_specs=pl.BlockSpec((tm, tn), lambda i, j: (i, j)),
        out_shape=jax.ShapeDtypeStruct((m, n), F32),
        compiler_params=_cparams(("parallel", "arbitrary")),
        name="out_proj",
    )(o, w, x)


def _ffn_kernel(x_ref, g_ref, wg_ref, wu_ref, wd_ref, y_ref, h_scr, acc_scr):
    f = pl.program_id(1)

    @pl.when(f == 0)
    def _():
        x = x_ref[...]
        h = x * lax.rsqrt(jnp.mean(x * x, axis=-1, keepdims=True) + RMS_EPS) * g_ref[...]
        h_scr[...] = h.astype(BF16)
        acc_scr[...] = jnp.zeros_like(acc_scr)

    h = h_scr[...]
    a = _dot(h, wg_ref[...])
    u = _dot(h, wu_ref[...])
    act = (a * (1.0 / (1.0 + jnp.exp(-a))) * u).astype(BF16)
    acc_scr[...] += _dot(act, wd_ref[...])

    @pl.when(f == pl.num_programs(1) - 1)
    def _():
        y_ref[...] = x_ref[...] + acc_scr[...]


def _ffn(x, gain, wg, wu, wd, layer, tm, tf=512):
    m, d = x.shape
    dff = wg.shape[2]
    assert dff % tf == 0 and m % tm == 0
    return pl.pallas_call(
        _ffn_kernel,
        grid=(m // tm, dff // tf),
        in_specs=[
            pl.BlockSpec((tm, d), lambda i, f: (i, 0)),
            pl.BlockSpec((1, d), lambda i, f: (0, 0)),
            pl.BlockSpec((None, d, tf), lambda i, f: (layer, 0, f)),
            pl.BlockSpec((None, d, tf), lambda i, f: (layer, 0, f)),
            pl.BlockSpec((None, tf, d), lambda i, f: (layer, f, 0)),
        ],
        out_specs=pl.BlockSpec((tm, d), lambda i, f: (i, 0)),
        out_shape=jax.ShapeDtypeStruct((m, d), F32),
        scratch_shapes=[pltpu.VMEM((tm, d), BF16), pltpu.VMEM((tm, d), F32)],
        compiler_params=_cparams(("parallel", "arbitrary")),
        name="ffn",
    )(x, gain.reshape(1, d), wg, wu, wd)


def _gelu_tanh(x):
    return 0.5 * x * (1.0 + jnp.tanh(math.sqrt(2.0 / math.pi) * (x + 0.044715 * (x * x * x))))


def _compress_slab(load, pos_ref, w1_ref, w2_ref, gain_ref, cos_ref, sin_ref, ok_ref, ov_ref, nblk):
    jc = 8

    for s, o_ref in ((0, ok_ref), (1, ov_ref)):
        def body(jj, acc, s=s):
            pieces = []
            for g in range(KV_HEADS):
                cols = []
                for jo in range(jc):
                    j = jj * jc + jo
                    xj = load(s * KV_HEADS + g, j)
                    cols.append((xj + pos_ref[s, pl.ds(j, 1), :]).astype(BF16))
                pieces.append(jnp.concatenate(cols, axis=1))
            lhs = jnp.concatenate(pieces, axis=0)
            k0 = pl.multiple_of(jj * (jc * HEAD_DIM), jc * HEAD_DIM)
            return acc + _dot(lhs, w1_ref[s, pl.ds(k0, jc * HEAD_DIM), :])

        hidden = lax.fori_loop(0, CMP_BLOCK // jc, body, jnp.zeros((KV_HEADS * nblk, w1_ref.shape[2]), F32))
        y = _dot(_gelu_tanh(hidden).astype(BF16), w2_ref[s])
        for g in range(KV_HEADS):
            yg = y[g * nblk:(g + 1) * nblk]
            if s == 0:
                yg = _head_norm_rope(yg, gain_ref[...], cos_ref[...], sin_ref[...])
            o_ref[g] = yg.astype(o_ref.dtype)


def _compress_prompt_kernel(*refs, nblk):
    n_col = 2 * KV_HEADS
    load = lambda c, j: refs[c][pl.ds(j, nblk, stride=CMP_BLOCK), :]
    _compress_slab(load, *refs[n_col:], nblk)


def _compress_prompt(rows, b, t, pos_emb, w1, w2, kc_gain, cos_c, sin_c):
    nc = t // CMP_BLOCK
    n_col = 2 * KV_HEADS
    out = jax.ShapeDtypeStruct((b, KV_HEADS, nc, HEAD_DIM), BF16)
    return pl.pallas_call(
        functools.partial(_compress_prompt_kernel, nblk=nc),
        grid=(b,),
        in_specs=[pl.BlockSpec((t, HEAD_DIM), lambda i, c=c: (i, c)) for c in range(n_col)] + [
            pl.BlockSpec(pos_emb.shape, lambda i: (0, 0, 0)),
            pl.BlockSpec(w1.shape, lambda i: (0, 0, 0)),
            pl.BlockSpec(w2.shape, lambda i: (0, 0, 0)),
            pl.BlockSpec((1, HEAD_DIM), lambda i: (0, 0)),
            pl.BlockSpec((nc, HEAD_DIM), lambda i: (0, 0)),
            pl.BlockSpec((nc, HEAD_DIM), lambda i: (0, 0)),
        ],
        out_specs=[pl.BlockSpec((None, KV_HEADS, nc, HEAD_DIM), lambda i: (i, 0, 0, 0))] * 2,
        out_shape=[out, out],
        compiler_params=_cparams(("arbitrary",)),
        name="nsa_compress_prompt",
    )(*([rows] * n_col), pos_emb, w1, w2, kc_gain.reshape(1, HEAD_DIM), cos_c, sin_c)


def _compress_sample_kernel(pt_ref, cache_ref, pos_ref, w1_ref, w2_ref, gain_ref, cos_ref, sin_ref,
                            ok_ref, ov_ref, slab, sems, y_scr, *, layer, pages, n_seq, n_chunks):
    step = pl.program_id(0) * n_chunks + pl.program_id(1)
    n_col = slab.shape[1]
    psz = slab.shape[0] // (2 * pages)
    nblk = pages * psz // CMP_BLOCK
    m = nblk * n_col

    def copies(st, slot):
        b, c = st // n_chunks, st % n_chunks
        return [pltpu.make_async_copy(cache_ref.at[layer, pt_ref[b, c * pages + p], :, 0],
                                      slab.at[pl.ds((slot * pages + p) * psz, psz)], sems.at[slot, p])
                for p in range(pages)]

    slot = _double_buffer(step, n_seq * n_chunks, copies)
    base = slot * (pages * psz)
    jc = 8

    hidden = [jnp.zeros((m, w1_ref.shape[2]), F32)] * 2
    for jj in range(CMP_BLOCK // jc):
        cols = []
        for j in range(jj * jc, (jj + 1) * jc):
            xj = slab[pl.ds(base + j, nblk, stride=CMP_BLOCK)] + pos_ref[j:j + 1]
            cols.append(xj.reshape(m, HEAD_DIM).astype(BF16))
        lhs = jnp.concatenate(cols, axis=1)
        rows = slice(jj * jc * HEAD_DIM, (jj + 1) * jc * HEAD_DIM)
        hidden = [hidden[s] + _dot(lhs, w1_ref[s, rows, :]) for s in range(2)]

    is_key = (_iota((m, 1), 0) & (n_col - 1)) < KV_HEADS
    act = _gelu_tanh(jnp.where(is_key, hidden[0], hidden[1])).astype(BF16)
    y_scr[...] = jnp.where(is_key, _dot(act, w2_ref[0]), _dot(act, w2_ref[1]))
    for g in range(KV_HEADS):
        yk = y_scr[pl.ds(g, nblk, stride=n_col), :]
        ok_ref[g] = _head_norm_rope(yk, gain_ref[...], cos_ref[...], sin_ref[...]).astype(ok_ref.dtype)
        ov_ref[g] = y_scr[pl.ds(KV_HEADS + g, nblk, stride=n_col), :].astype(ov_ref.dtype)


def _compress_sample(cache, layer, page_table, pos_emb, w1, w2, kc_gain, cos_c, sin_c, pages=16):
    b, n_pages = page_table.shape
    psz = cache.shape[2]
    pages = min(pages, n_pages)
    assert n_pages % pages == 0 and psz % CMP_BLOCK == 0
    nblk = pages * psz // CMP_BLOCK
    nc = n_pages * psz // CMP_BLOCK
    n_col = 2 * KV_HEADS
    pos_rows = jnp.repeat(jnp.swapaxes(pos_emb, 0, 1), KV_HEADS, axis=1)
    out = jax.ShapeDtypeStruct((b, KV_HEADS, nc, HEAD_DIM), BF16)
    grid_spec = pltpu.PrefetchScalarGridSpec(
        num_scalar_prefetch=1,
        grid=(b, n_pages // pages),
        in_specs=[
            pl.BlockSpec(memory_space=pl.ANY),
            pl.BlockSpec(pos_rows.shape, lambda i, c, pt: (0, 0, 0)),
            pl.BlockSpec(w1.shape, lambda i, c, pt: (0, 0, 0)),
            pl.BlockSpec(w2.shape, lambda i, c, pt: (0, 0, 0)),
            pl.BlockSpec((1, HEAD_DIM), lambda i, c, pt: (0, 0)),
            pl.BlockSpec((nblk, HEAD_DIM), lambda i, c, pt: (c, 0)),
            pl.BlockSpec((nblk, HEAD_DIM), lambda i, c, pt: (c, 0)),
        ],
        out_specs=[pl.BlockSpec((None, KV_HEADS, nblk, HEAD_DIM), lambda i, c, pt: (i, 0, c, 0))] * 2,
        scratch_shapes=[pltpu.VMEM((2 * pages * psz, n_col, HEAD_DIM), F32), pltpu.SemaphoreType.DMA((2, pages)),
                        pltpu.VMEM((nblk * n_col, HEAD_DIM), F32)],
    )
    return pl.pallas_call(
        functools.partial(_compress_sample_kernel, layer=layer, pages=pages, n_seq=b, n_chunks=n_pages // pages),
        grid_spec=grid_spec,
        out_shape=[out, out],
        compiler_params=_cparams(("arbitrary", "arbitrary")),
        name="nsa_compress_sample",
    )(page_table, cache, pos_rows, w1, w2, kc_gain.reshape(1, HEAD_DIM), cos_c, sin_c)


def _nsa_prompt_kernel(q_ref, ks_ref, vs_ref, kw_ref, vw_ref, kc_ref, vc_ref, gt_ref, o_ref,
                       ks_s, vs_s, kw_s, vw_s, *, t_len, tq):
    qi = pl.program_id(2)

    @pl.when(qi == 0)
    def _():
        ks_s[...] = ks_ref[...].astype(BF16)
        vs_s[...] = vs_ref[...].astype(BF16)
        kw_s[...] = kw_ref[...].astype(BF16)
        vw_s[...] = vw_ref[...].astype(BF16)

    nc = kc_ref.shape[0]
    t0 = qi * tq
    tpos = t0 + _iota((tq, 1), 0)
    qb = q_ref[...]
    gt = gt_ref[...]
    kc, vc = kc_ref[...], vc_ref[...]
    q_all = jnp.concatenate([qb[:, r * HEAD_DIM:(r + 1) * HEAD_DIM] for r in range(GROUP)], axis=0)

    cmask = ((_iota((tq, nc), 1) * CMP_BLOCK + (CMP_BLOCK - 1)) <= tpos)[None]
    s_cmp = (_dot_nt(q_all, kc) * SCALE).reshape(GROUP, tq, nc)
    o_cmp = _dot(_masked_softmax(s_cmp, cmask).reshape(GROUP * tq, nc).astype(BF16), vc)
    tpos_t = t0 + _iota((nc, tq), 1)
    cblk_t = _iota((nc, tq), 0)
    cmask_t = (cblk_t * CMP_BLOCK + (CMP_BLOCK - 1)) <= tpos_t
    s_t_all = _dot_nt(kc, q_all) * SCALE
    imp_t = jnp.zeros((nc, tq), F32)
    for r in range(GROUP):
        s_t = jnp.where(cmask_t, s_t_all[:, r * tq:(r + 1) * tq], NEG_INF)
        e_t = jnp.exp(s_t - jnp.max(s_t, axis=0, keepdims=True))
        imp_t = imp_t + jnp.where(cmask_t, e_t * (1.0 / jnp.sum(e_t, axis=0, keepdims=True)), 0.0)

    cur_t = tpos_t >> 6
    sel_t = _rank_select(imp_t, cblk_t < cur_t, N_SELECT - 1, axis=0) | (cblk_t == cur_t)
    sel_t = jnp.where(sel_t, 1.0, 0.0).astype(BF16)

    wn = min(WINDOW + tq, t_len)
    w0 = pl.multiple_of(jnp.maximum(t0 + tq - wn, 0), tq)
    wd = tpos - (w0 + _iota((tq, wn), 1))
    wmask = (wd >= 0) & (wd < WINDOW)
    kw, vw = kw_s[pl.ds(w0, wn), :], vw_s[pl.ds(w0, wn), :]
    o_win = _attend_group(q_all, kw, vw, wmask)

    n_prefix = t_len // tq
    step = t_len // n_prefix
    for pi in range(n_prefix):
        @pl.when((t0 + tq - 1) // step == pi)
        def _(lk=(pi + 1) * step):
            expand = jnp.where((_iota((nc, lk), 1) >> 6) == _iota((nc, lk), 0), 1.0, 0.0).astype(BF16)
            picked = lax.dot_general(sel_t, expand, (((0,), (0,)), ((), ())), preferred_element_type=F32)
            smask = (picked > 0.5) & (_iota((tq, lk), 1) <= tpos)
            o_slc = _attend_group(q_all, ks_s[0:lk, :], vs_s[0:lk, :], smask)
            for r in range(GROUP):
                rows = slice(r * tq, (r + 1) * tq)
                o = (gt[:, 3 * r:3 * r + 1] * o_cmp[rows] + gt[:, 3 * r + 1:3 * r + 2] * o_slc[rows]
                     + gt[:, 3 * r + 2:3 * r + 3] * o_win[rows])
                o_ref[:, r * HEAD_DIM:(r + 1) * HEAD_DIM] = o.astype(o_ref.dtype)


def _nsa_prompt_attn(q, rows, win, k_cmp, v_cmp, gates, b, t, tq=256):
    nq = t // tq
    nc = t // CMP_BLOCK
    gw = GROUP * HEAD_DIM
    kv_spec = lambda col0: pl.BlockSpec((t, HEAD_DIM), lambda i, g, qi: (i, col0 + g))
    cmp_spec = pl.BlockSpec((None, None, nc, HEAD_DIM), lambda i, g, qi: (i, g, 0, 0))
    return pl.pallas_call(
        functools.partial(_nsa_prompt_kernel, t_len=t, tq=tq),
        grid=(b, KV_HEADS, nq),
        in_specs=[
            pl.BlockSpec((tq, gw), lambda i, g, qi: (i * nq + qi, g)),
            kv_spec(2 * KV_HEADS), kv_spec(3 * KV_HEADS), kv_spec(0), kv_spec(KV_HEADS),
            cmp_spec, cmp_spec,
            pl.BlockSpec((tq, HEAD_DIM), lambda i, g, qi: (i * nq + qi, g)),
        ],
        out_specs=pl.BlockSpec((tq, gw), lambda i, g, qi: (i * nq + qi, g)),
        out_shape=jax.ShapeDtypeStruct((b * t, N_HEADS * HEAD_DIM), BF16),
        scratch_shapes=[pltpu.VMEM((t, HEAD_DIM), BF16)] * 4,
        compiler_params=_cparams(("parallel", "parallel", "arbitrary")),
        name="nsa_prompt_attn",
    )(q, rows, rows, win, win, k_cmp, v_cmp, gates)


def _nsa_sample_cmp_kernel(q_ref, kc_ref, vc_ref, o_ref, idx_ref, *, pos):
    nc = kc_ref.shape[1]
    cblk = _iota((GROUP, nc), 1)
    cmask = (cblk * CMP_BLOCK + (CMP_BLOCK - 1)) <= pos
    group_row = _iota((KV_HEADS, nc), 0)
    imp = jnp.zeros((KV_HEADS, nc), F32)
    for g in range(KV_HEADS):
        p = _masked_softmax(_dot_nt(q_ref[g], kc_ref[g]) * SCALE, cmask)
        o_ref[g] = _dot(p.astype(BF16), vc_ref[g])
        imp = jnp.where(group_row == g, jnp.sum(p, axis=0, keepdims=True), imp)
    cand = _iota((KV_HEADS, nc), 1) < (pos // CMP_BLOCK)
    idx_ref[...] = _topk_indices(jnp.where(cand, imp, NEG_INF), N_SELECT - 1)


def _nsa_sample_cmp(q4, k_cmp, v_cmp, pos):
    b = q4.shape[0]
    nc = k_cmp.shape[2]
    assert nc >= N_SELECT - 1 and pos // CMP_BLOCK >= N_SELECT - 1
    spec4 = pl.BlockSpec((None, KV_HEADS, GROUP, HEAD_DIM), lambda i: (i, 0, 0, 0))
    cspec = pl.BlockSpec((None, KV_HEADS, nc, HEAD_DIM), lambda i: (i, 0, 0, 0))
    return pl.pallas_call(
        functools.partial(_nsa_sample_cmp_kernel, pos=pos),
        grid=(b,),
        in_specs=[spec4, cspec, cspec],
        out_specs=[spec4, pl.BlockSpec((None, KV_HEADS, 128), lambda i: (i, 0, 0))],
        out_shape=[jax.ShapeDtypeStruct((b, KV_HEADS, GROUP, HEAD_DIM), F32),
                   jax.ShapeDtypeStruct((b, KV_HEADS, 128), jnp.int32)],
        compiler_params=_cparams(("parallel",)),
        name="nsa_sample_cmp",
    )(q4, k_cmp, v_cmp)


def _softmax_with_extra(s, s_new):
    m = jnp.maximum(jnp.max(s, axis=-1, keepdims=True), s_new)
    e, e_new = jnp.exp(s - m), jnp.exp(s_new - m)
    inv = 1.0 / (jnp.sum(e, axis=-1, keepdims=True) + e_new)
    return e * inv, e_new * inv


def _row_dot(q, k_row):
    return jnp.sum(q.astype(F32) * k_row.astype(BF16).astype(F32), axis=-1, keepdims=True)


def _nsa_sample_slc_kernel(pt_ref, idx_ref, q_ref, cache_ref, knew_ref, vnew_ref,
                           win_ref, kwnew_ref, vwnew_ref, ocmp_ref, gt_ref, o_ref,
                           slab, sems, *, layer, n_sel, halves, n_seq):
    g = pl.program_id(1)
    step = pl.program_id(0) * KV_HEADS + g
    per_row = 4 * KV_HEADS
    kv_rows = 2 * KV_HEADS
    rows = slab.shape[0] // 2
    blk_rows = CMP_BLOCK * per_row

    def copies(st, slot):
        b, gg = st // KV_HEADS, st % KV_HEADS
        out = []
        for n in range(n_sel):
            blk = idx_ref[b, gg * n_sel + n]
            src = cache_ref.at[layer, pt_ref[b, blk // halves] * halves + blk % halves]
            out.append(pltpu.make_async_copy(src, slab.at[pl.ds((slot * n_sel + n) * blk_rows, blk_rows)],
                                             sems.at[slot, n]))
        return out

    slot = _double_buffer(step, n_seq * KV_HEADS, copies)
    base = slot * rows
    n_keys = n_sel * CMP_BLOCK
    q = q_ref[...]
    k_sel = slab[pl.ds(base + 2 * KV_HEADS + g, n_keys, stride=per_row), :].astype(BF16)
    v_sel = slab[pl.ds(base + 3 * KV_HEADS + g, n_keys, stride=per_row), :].astype(BF16)
    p, p_new = _softmax_with_extra(_dot_nt(q, k_sel) * SCALE, _row_dot(q, knew_ref[...]) * SCALE)
    o_slc = _dot(p.astype(BF16), v_sel) + p_new.astype(BF16).astype(F32) * vnew_ref[...].astype(BF16).astype(F32)

    nw = win_ref.shape[0] // kv_rows
    kwin = win_ref[pl.ds(g, nw, stride=kv_rows), :].astype(BF16)
    vwin = win_ref[pl.ds(KV_HEADS + g, nw, stride=kv_rows), :].astype(BF16)
    sw = jnp.where(nw - _iota((GROUP, nw), 1) < WINDOW, _dot_nt(q, kwin) * SCALE, NEG_INF)
    p, p_new = _softmax_with_extra(sw, _row_dot(q, kwnew_ref[...]) * SCALE)
    o_win = _dot(p.astype(BF16), vwin) + p_new.astype(BF16).astype(F32) * vwnew_ref[...].astype(BF16).astype(F32)

    gt = gt_ref[...]
    rowi = _iota((GROUP, 1), 0)
    gcol = [sum(jnp.where(rowi == r, gt[:, 3 * r + c:3 * r + c + 1], 0.0) for r in range(GROUP)) for c in range(3)]
    o_ref[...] = gcol[0] * ocmp_ref[...] + gcol[1] * o_slc + gcol[2] * o_win


def _nsa_sample_slc(cache, layer, page_table, idx, q4, rows_s, win_buf, win_layer, win_s, o_cmp, gates_s, halves):
    b = q4.shape[0]
    n_sel = idx.shape[1] // KV_HEADS
    kv_rows = 2 * KV_HEADS
    spec4 = pl.BlockSpec((None, None, GROUP, HEAD_DIM), lambda i, g, pt, ix: (i, g, 0, 0))
    new_spec = lambda col0: pl.BlockSpec((None, None, 1, HEAD_DIM), lambda i, g, pt, ix: (i, col0 + g, 0, 0))
    grid_spec = pltpu.PrefetchScalarGridSpec(
        num_scalar_prefetch=2,
        grid=(b, KV_HEADS),
        in_specs=[
            spec4,
            pl.BlockSpec(memory_space=pl.ANY),
            new_spec(2 * KV_HEADS), new_spec(3 * KV_HEADS),
            pl.BlockSpec((None, None) + win_buf.shape[2:], lambda i, g, pt, ix: (win_layer, i, 0, 0)),
            new_spec(0), new_spec(KV_HEADS),
            spec4,
            pl.BlockSpec((None, None, 1, HEAD_DIM), lambda i, g, pt, ix: (i, g, 0, 0)),
        ],
        out_specs=spec4,
        scratch_shapes=[pltpu.VMEM((2 * n_sel * cache.shape[2], HEAD_DIM), F32),
                        pltpu.SemaphoreType.DMA((2, n_sel))],
    )
    return pl.pallas_call(
        functools.partial(_nsa_sample_slc_kernel, layer=layer, n_sel=n_sel, halves=halves, n_seq=b),
        grid_spec=grid_spec,
        out_shape=jax.ShapeDtypeStruct((b, KV_HEADS, GROUP, HEAD_DIM), F32),
        compiler_params=_cparams(("arbitrary", "arbitrary")),
        name="nsa_sample_slc",
    )(page_table, idx, q4, cache, rows_s, rows_s, win_buf, win_s, win_s, o_cmp, gates_s)


def _moba_prompt_kernel(q_ref, k_ref, v_ref, o_ref, k_s, v_s, km_s, *, t_len, tq):
    qi = pl.program_id(2)
    nb = km_s.shape[0]

    @pl.when(qi == 0)
    def _():
        k = k_ref[...]
        k_s[...] = k.astype(BF16)
        v_s[...] = v_ref[...].astype(BF16)
        km_s[...] = (jnp.sum(k.reshape(nb, MOBA_BLOCK, HEAD_DIM), axis=1) * (1.0 / MOBA_BLOCK)).astype(BF16)

    t0 = qi * tq
    tpos = t0 + _iota((tq, 1), 0)
    cur_t = (t0 + _iota((nb, tq), 1)) >> 8
    blk_t = _iota((nb, tq), 0)
    qb = q_ref[...]
    km = km_s[...]
    q_all = jnp.concatenate([qb[:, r * HEAD_DIM:(r + 1) * HEAD_DIM] for r in range(GROUP)], axis=0)
    gate_t_all = _dot_nt(km, q_all)
    sel_t = []
    for r in range(GROUP):
        sel = _rank_select(gate_t_all[:, r * tq:(r + 1) * tq], blk_t < cur_t, MOBA_TOPK, axis=0) | (blk_t == cur_t)
        sel_t.append(jnp.where(sel, 1.0, 0.0).astype(BF16))

    n_prefix = t_len // tq if tq % MOBA_BLOCK == 0 else 1
    step = t_len // n_prefix
    for pi in range(n_prefix):
        @pl.when((t0 + tq - 1) // step == pi)
        def _(lk=(pi + 1) * step):
            expand = jnp.where((_iota((nb, lk), 1) >> 8) == _iota((nb, lk), 0), 1.0, 0.0).astype(BF16)
            causal = _iota((tq, lk), 1) <= tpos
            picked = jnp.concatenate(
                [lax.dot_general(sel_t[r], expand, (((0,), (0,)), ((), ())), preferred_element_type=F32)
                 for r in range(GROUP)], axis=0).reshape(GROUP, tq, lk)
            o = _attend_group(q_all, k_s[0:lk, :], v_s[0:lk, :], (picked > 0.5) & causal[None])
            for r in range(GROUP):
                o_ref[:, r * HEAD_DIM:(r + 1) * HEAD_DIM] = o[r * tq:(r + 1) * tq].astype(o_ref.dtype)


def _moba_prompt_attn(q, kv, b, t, tq=256):
    assert t % MOBA_BLOCK == 0
    nq = t // tq
    gw = GROUP * HEAD_DIM
    kv_spec = lambda col0: pl.BlockSpec((t, HEAD_DIM), lambda i, g, qi: (i, col0 + g))
    return pl.pallas_call(
        functools.partial(_moba_prompt_kernel, t_len=t, tq=tq),
        grid=(b, KV_HEADS, nq),
        in_specs=[pl.BlockSpec((tq, gw), lambda i, g, qi: (i * nq + qi, g)), kv_spec(0), kv_spec(KV_HEADS)],
        out_specs=pl.BlockSpec((tq, gw), lambda i, g, qi: (i * nq + qi, g)),
        out_shape=jax.ShapeDtypeStruct((b * t, N_HEADS * HEAD_DIM), BF16),
        scratch_shapes=[pltpu.VMEM((t, HEAD_DIM), BF16), pltpu.VMEM((t, HEAD_DIM), BF16),
                        pltpu.VMEM((t // MOBA_BLOCK, HEAD_DIM), BF16)],
        compiler_params=_cparams(("parallel", "parallel", "arbitrary")),
        name="moba_prompt_attn",
    )(q, kv, kv)


def _moba_sample_gate_kernel(pt_ref, q_ref, cache_ref, idx_ref, slab, sems, ksum, *, layer, pages, ppb, n_seq, n_chunks):
    c = pl.program_id(1)
    step = pl.program_id(0) * n_chunks + c
    per_row = 2 * KV_HEADS
    rows = slab.shape[0] // 2
    page_rows = rows // pages

    def copies(st, slot):
        b, cc = st // n_chunks, st % n_chunks
        return [pltpu.make_async_copy(cache_ref.at[layer, pt_ref[b, cc * pages + p]],
                                      slab.at[pl.ds((slot * pages + p) * page_rows, page_rows)], sems.at[slot, p])
                for p in range(pages)]

    slot = _double_buffer(step, n_seq * n_chunks, copies)
    x = slab[pl.ds(pl.multiple_of(slot * rows, rows), rows), :]
    n_blk = pages // ppb
    sums = jnp.sum(x.reshape(n_blk, rows // (n_blk * per_row), per_row, HEAD_DIM), axis=1)
    ksum[pl.ds(pl.multiple_of(c * (n_blk * per_row), n_blk * per_row), n_blk * per_row), :] = sums.reshape(
        n_blk * per_row, HEAD_DIM)

    @pl.when(c == n_chunks - 1)
    def _():
        nb = ksum.shape[0] // per_row
        for g in range(KV_HEADS):
            km = (ksum[pl.ds(g, nb, stride=per_row), :] * (1.0 / MOBA_BLOCK)).astype(BF16)
            idx_ref[g] = _topk_indices(_dot_nt(q_ref[g], km), MOBA_TOPK)


def _moba_sample_gate(cache, layer, page_table, q4, pages=16):
    b, n_pages = page_table.shape
    per_row = 2 * KV_HEADS
    page_rows = cache.shape[2]
    psz = page_rows // per_row
    ppb = MOBA_BLOCK // psz
    nb = n_pages // ppb
    pages = min(pages, n_pages)
    assert MOBA_BLOCK % psz == 0 and n_pages % pages == 0 and pages % ppb == 0 and nb >= MOBA_TOPK
    grid_spec = pltpu.PrefetchScalarGridSpec(
        num_scalar_prefetch=1,
        grid=(b, n_pages // pages),
        in_specs=[
            pl.BlockSpec((None, KV_HEADS, GROUP, HEAD_DIM), lambda i, c, pt: (i, 0, 0, 0)),
            pl.BlockSpec(memory_space=pl.ANY),
        ],
        out_specs=pl.BlockSpec((None, KV_HEADS, GROUP, 128), lambda i, c, pt: (i, 0, 0, 0)),
        scratch_shapes=[pltpu.VMEM((2 * pages * page_rows, HEAD_DIM), F32), pltpu.SemaphoreType.DMA((2, pages)),
                        pltpu.VMEM((nb * per_row, HEAD_DIM), F32)],
    )
    return pl.pallas_call(
        functools.partial(_moba_sample_gate_kernel, layer=layer, pages=pages, ppb=ppb, n_seq=b,
                          n_chunks=n_pages // pages),
        grid_spec=grid_spec,
        out_shape=jax.ShapeDtypeStruct((b, KV_HEADS, GROUP, 128), jnp.int32),
        compiler_params=_cparams(("arbitrary", "arbitrary")),
        name="moba_sample_gate",
    )(page_table, q4, cache)


def _moba_sample_attn_kernel(pt_ref, idx_ref, q_ref, cache_ref, knew_ref, vnew_ref, o_ref, slab, sems,
                             *, layer, n_sel, ppb, n_seq):
    h = pl.program_id(1)
    step = pl.program_id(0) * N_HEADS + h
    per_row = 2 * KV_HEADS
    n_pg = n_sel * ppb
    rows = slab.shape[0] // 2
    page_rows = rows // n_pg

    def copies(st, slot):
        b, hh = st // N_HEADS, st % N_HEADS
        out = []
        for n in range(n_pg):
            page = pt_ref[b, idx_ref[b, hh * n_sel + n // ppb] * ppb + n % ppb]
            out.append(pltpu.make_async_copy(cache_ref.at[layer, page],
                                             slab.at[pl.ds((slot * n_pg + n) * page_rows, page_rows)],
                                             sems.at[slot, n]))
        return out

    slot = _double_buffer(step, n_seq * N_HEADS, copies)
    base = slot * rows
    g = h // GROUP
    n_keys = rows // per_row
    q = q_ref[...]
    k_sel = slab[pl.ds(base + g, n_keys, stride=per_row), :].astype(BF16)
    v_sel = slab[pl.ds(base + KV_HEADS + g, n_keys, stride=per_row), :].astype(BF16)
    p, p_new = _softmax_with_extra(_dot_nt(q, k_sel) * SCALE, _row_dot(q, knew_ref[...]) * SCALE)
    o = _dot(p.astype(BF16), v_sel) + p_new.astype(BF16).astype(F32) * vnew_ref[...].astype(BF16).astype(F32)
    o_ref[...] = o.astype(o_ref.dtype)


def _moba_sample_attn(cache, layer, page_table, idx, q16, kv_new):
    b = q16.shape[0]
    page_rows = cache.shape[2]
    ppb = MOBA_BLOCK // (page_rows // (2 * KV_HEADS))
    n_sel = idx.shape[1] // N_HEADS
    n_pg = n_sel * ppb
    hspec = pl.BlockSpec((None, None, 1, HEAD_DIM), lambda i, h, pt, ix: (i, h, 0, 0))
    new_spec = lambda col0: pl.BlockSpec((None, None, 1, HEAD_DIM), lambda i, h, pt, ix: (i, col0 + h // GROUP, 0, 0))
    grid_spec = pltpu.PrefetchScalarGridSpec(
        num_scalar_prefetch=2,
        grid=(b, N_HEADS),
        in_specs=[hspec, pl.BlockSpec(memory_space=pl.ANY), new_spec(0), new_spec(KV_HEADS)],
        out_specs=hspec,
        scratch_shapes=[pltpu.VMEM((2 * n_pg * page_rows, HEAD_DIM), F32), pltpu.SemaphoreType.DMA((2, n_pg))],
    )
    return pl.pallas_call(
        functools.partial(_moba_sample_attn_kernel, layer=layer, n_sel=n_sel, ppb=ppb, n_seq=b),
        grid_spec=grid_spec,
        out_shape=jax.ShapeDtypeStruct((b, N_HEADS, 1, HEAD_DIM), BF16),
        compiler_params=_cparams(("arbitrary", "arbitrary")),
        name="moba_sample_attn",
    )(page_table, idx, q16, cache, kv_new, kv_new)


def _log_sigmoid(z):
    return jnp.minimum(z, 0.0) - jnp.log(1.0 + jnp.exp(-jnp.abs(z)))


def _split2(x):
    hi = x.astype(BF16)
    return hi, (x - hi.astype(F32)).astype(BF16)


EXP_UNDERFLOW = -104.0


def _sb_prompt_kernel(q_ref, k_ref, v_ref, o_ref, k_s, v_s, acc_scr, *, tq):
    qi = pl.program_id(2)

    @pl.when(qi == 0)
    def _():
        k_s[...] = k_ref[...].astype(BF16)
        v_s[...] = v_ref[...].astype(BF16)

    hp = acc_scr.shape[0]
    lanes = lambda h: slice(h * HEAD_DIM, (h + 1) * HEAD_DIM)
    qs = [q_ref[:, lanes(h)].astype(BF16) for h in range(hp)]
    after = jnp.where(_iota((tq, tq), 0) > _iota((tq, tq), 1), 1.0, 0.0).astype(BF16)
    below = (_iota((hp * tq, tq), 0) & (tq - 1)) > _iota((hp * tq, tq), 1)
    acc_scr[...] = jnp.zeros_like(acc_scr)

    def chunk(c, run, diagonal):
        s0 = pl.multiple_of(c * tq, tq)
        z = jnp.concatenate([_dot_nt(qs[h], k_s[pl.ds(s0, tq), lanes(h)]) for h in range(hp)], axis=0) * SCALE
        log_beta = _log_sigmoid(z)
        log_keep = log_beta - z
        if diagonal:
            log_keep = jnp.where(below, log_keep, 0.0)
        both = _dot(jnp.concatenate(_split2(log_keep), axis=0), after)
        within = both[:hp * tq] + both[hp * tq:]
        a = jnp.exp(log_beta + within + run)
        if diagonal:
            a = jnp.where(below, a, 0.0)
        a = a.astype(BF16)
        for h in range(hp):
            acc_scr[h] += _dot(a[h * tq:(h + 1) * tq], v_s[pl.ds(s0, tq), lanes(h)])
        return run + within[:, 0:1] + log_keep[:, 0:1]

    run = chunk(qi, jnp.zeros((hp * tq, 1), F32), True)

    def cond(carry):
        c, run = carry
        return (c >= 0) & (jnp.max(run) >= EXP_UNDERFLOW)

    def body(carry):
        c, run = carry
        return c - 1, chunk(c, run, False)

    lax.while_loop(cond, body, (qi - 1, run))
    for h in range(hp):
        o_ref[:, lanes(h)] = acc_scr[h].astype(o_ref.dtype)


def _sb_prompt_attn(qkv, b, t, tq=256, hp=2):
    nq = t // tq
    n_hp = N_HEADS // hp
    w = hp * HEAD_DIM
    assert tq & (tq - 1) == 0
    return pl.pallas_call(
        functools.partial(_sb_prompt_kernel, tq=tq),
        grid=(b, n_hp, nq),
        in_specs=[
            pl.BlockSpec((tq, w), lambda i, h, qi: (i * nq + qi, h)),
            pl.BlockSpec((t, w), lambda i, h, qi: (i, n_hp + h)),
            pl.BlockSpec((t, w), lambda i, h, qi: (i, 2 * n_hp + h)),
        ],
        out_specs=pl.BlockSpec((tq, w), lambda i, h, qi: (i * nq + qi, h)),
        out_shape=jax.ShapeDtypeStruct((b * t, N_HEADS * HEAD_DIM), BF16),
        scratch_shapes=[pltpu.VMEM((t, w), BF16)] * 2 + [pltpu.VMEM((hp, tq, HEAD_DIM), F32)],
        compiler_params=_cparams(("parallel", "parallel", "arbitrary")),
        name="sb_prompt_attn",
    )(qkv, qkv, qkv)


def _sb_sample_kernel(pt_ref, qbd_ref, cache_ref, o_ref, buf, sems, acc_scr, *, layer, n_pages):
    b = pl.program_id(0)
    per_row = 2 * N_HEADS
    rows = buf.shape[0] // 2
    psz = rows // per_row

    def page_copy(p, slot):
        return pltpu.make_async_copy(cache_ref.at[layer, pt_ref[b, n_pages - 1 - p]],
                                     buf.at[pl.ds(slot * rows, rows)], sems.at[slot])

    acc_scr[...] = jnp.zeros_like(acc_scr)
    after = jnp.where(_iota((psz, psz), 1) > _iota((psz, psz), 0), 1.0, 0.0).astype(BF16)
    live = _iota((1, 128), 1) < N_HEADS
    page_copy(0, 0).start()

    def cond(carry):
        p, run = carry
        return (p < n_pages) & (jnp.max(jnp.where(live, run, NEG_INF)) >= EXP_UNDERFLOW)

    def body(carry):
        p, run = carry
        slot = p % 2
        page_copy(p, slot).wait()

        @pl.when(p + 1 < n_pages)
        def _():
            page_copy(p + 1, 1 - slot).start()

        base = slot * rows

        def heads_wide(first):
            return jnp.concatenate([buf[pl.ds(base + first + h, psz, stride=per_row), :].astype(BF16)
                                    for h in range(N_HEADS)], axis=1)

        z = _dot(heads_wide(0), qbd_ref[...]) * SCALE
        log_beta = _log_sigmoid(z)
        log_keep = log_beta - z
        hi, lo = _split2(log_keep)
        within = _dot(after, hi) + _dot(after, lo)
        a = jnp.exp(log_beta + within + run)
        acc_scr[...] += _dot(a.T.astype(BF16), heads_wide(N_HEADS))
        return p + 1, run + within[0:1, :] + log_keep[0:1, :]

    p_end, _ = lax.while_loop(cond, body, (0, jnp.zeros((1, 128), F32)))

    @pl.when(p_end < n_pages)
    def _():
        page_copy(p_end, p_end % 2).wait()

    acc = acc_scr[...]
    own = _iota(acc.shape, 0) == (_iota(acc.shape, 1) >> 7)
    o_ref[...] = jnp.sum(jnp.where(own, acc, 0.0), axis=0, keepdims=True)


def _sb_sample_attn(cache, layer, page_table, qbd):
    b, n_pages = page_table.shape
    aw = N_HEADS * HEAD_DIM
    grid_spec = pltpu.PrefetchScalarGridSpec(
        num_scalar_prefetch=1,
        grid=(b,),
        in_specs=[
            pl.BlockSpec((None, aw, 128), lambda i, pt: (i, 0, 0)),
            pl.BlockSpec(memory_space=pl.ANY),
        ],
        out_specs=pl.BlockSpec((None, 1, aw), lambda i, pt: (i, 0, 0)),
        scratch_shapes=[pltpu.VMEM((2 * cache.shape[2], HEAD_DIM), F32), pltpu.SemaphoreType.DMA((2,)),
                        pltpu.VMEM((128, aw), F32)],
    )
    return pl.pallas_call(
        functools.partial(_sb_sample_kernel, layer=layer, n_pages=n_pages),
        grid_spec=grid_spec,
        out_shape=jax.ShapeDtypeStruct((b, 1, aw), F32),
        compiler_params=_cparams(("arbitrary",)),
        name="sb_sample_attn",
    )(page_table, qbd, cache)


def _tile_rows(m, tile=512):
    return tile if m % tile == 0 else m


PROJ_ROWS = 1024


def _nsa_weights(w_in, q_gain, ks_gain, kw_gain):
    d = w_in.shape[0]
    main = N_HEADS * HEAD_DIM + 6 * KV_HEADS * HEAD_DIM
    wg = w_in[:, main:].reshape(d, KV_HEADS, GROUP * 3)
    wg = jnp.pad(wg, ((0, 0), (0, 0), (0, HEAD_DIM - GROUP * 3))).reshape(d, KV_HEADS * HEAD_DIM)
    w = jnp.concatenate([w_in[:, :main], wg], axis=1).astype(BF16)
    kvh = KV_HEADS
    tiles = [("rope", 0, None)] * 4 + [
        ("plain", 1, (0, 0)), ("plain", 1, (0, kvh)), ("rope", 1, (0, 2 * kvh)), ("plain", 1, (0, 3 * kvh)),
        ("rope", 2, (1, 0)), ("plain", 2, (1, kvh)), ("sigmoid", 3, None)]
    ones = jnp.ones((HEAD_DIM,), F32)
    hg = jnp.stack([q_gain] * 4 + [ones, ones, ks_gain, ones, kw_gain, ones, ones])[:, None, :]
    outs = [(4, BF16), (4, F32), (2, F32), (1, F32)]
    return w, tiles, hg, outs


def _nsa_layer(xp, xs, b, t, cache, layer, page_table, win_all, norm_gain, w_in, w_out, q_gain, kc_gain,
               ks_gain, kw_gain, cmp_pos, cmp_w1, cmp_w2):
    bs = xs.shape[0]
    n_pages, psz = page_table.shape[1], cache.shape[2]
    past = n_pages * psz
    w, tiles, hg, outs = _nsa_weights(w_in, q_gain, ks_gain, kw_gain)
    w1, w2 = cmp_w1.astype(BF16), cmp_w2.astype(BF16)
    w_out = w_out.astype(BF16)

    cos_p, sin_p = _rope_tables(jnp.arange(t, dtype=jnp.int32))
    cos_pt, sin_pt = jnp.tile(cos_p, (b, 1)), jnp.tile(sin_p, (b, 1))
    cache_rows = (4 * KV_HEADS, 2 * KV_HEADS)
    q, rows, win, gates, rows_c, win_c = _norm_proj(xp, norm_gain, w, hg, cos_pt, sin_pt, tiles, outs,
                                                    _tile_rows(b * t, PROJ_ROWS), cache_rows=cache_rows)
    nc = t // CMP_BLOCK
    cos_c, sin_c = _rope_tables(jnp.arange(nc, dtype=jnp.int32) * CMP_BLOCK + (CMP_BLOCK - 1))
    k_cmp, v_cmp = _compress_prompt(rows, b, t, cmp_pos, w1, w2, kc_gain, cos_c, sin_c)
    o = _nsa_prompt_attn(q, rows, win, k_cmp, v_cmp, gates, b, t)
    xp = _out_proj(o, w_out, xp, _tile_rows(b * t, PROJ_ROWS))

    cos_s, sin_s = _rope_tables(jnp.full((bs,), past, jnp.int32))
    qs, _, _, gates_s, rows_s, win_s = _norm_proj(xs, norm_gain, w, hg, cos_s, sin_s, tiles, outs, bs,
                                                  cache_rows=cache_rows)
    ncs = past // CMP_BLOCK
    cos_cs, sin_cs = _rope_tables(jnp.arange(ncs, dtype=jnp.int32) * CMP_BLOCK + (CMP_BLOCK - 1))
    n_layers, n_pool = cache.shape[:2]
    k_cmp_s, v_cmp_s = _compress_sample(cache.reshape(n_layers, n_pool, psz, 2, 2 * KV_HEADS, HEAD_DIM), layer,
                                        page_table, cmp_pos, w1, w2, kc_gain, cos_cs, sin_cs)
    q4 = qs.reshape(bs, KV_HEADS, GROUP, HEAD_DIM)
    o_cmp, idx = _nsa_sample_cmp(q4, k_cmp_s, v_cmp_s, past)
    idx = idx[:, :, :N_SELECT - 1].reshape(bs, KV_HEADS * (N_SELECT - 1))
    halves = psz // CMP_BLOCK
    block_view = cache.reshape(n_layers, n_pool * halves, CMP_BLOCK * 4 * KV_HEADS, HEAD_DIM)
    nw = win_all.shape[2]
    win_view = win_all.reshape(win_all.shape[0], bs, nw * 2 * KV_HEADS, HEAD_DIM)
    o_s = _nsa_sample_slc(block_view, layer, page_table, idx, q4,
                          rows_s.reshape(bs, 4 * KV_HEADS, 1, HEAD_DIM), win_view, layer,
                          win_s.reshape(bs, 2 * KV_HEADS, 1, HEAD_DIM), o_cmp,
                          gates_s.reshape(bs, KV_HEADS, 1, HEAD_DIM), halves)
    xs = _out_proj(o_s.reshape(bs, N_HEADS * HEAD_DIM).astype(BF16), w_out, xs, bs)

    kv_p = rows_c.reshape(b, t, 4, KV_HEADS, HEAD_DIM)
    kv_s = rows_s.reshape(bs, 1, 4, KV_HEADS, HEAD_DIM)
    win_p = win_c.reshape(b, t, 2, KV_HEADS, HEAD_DIM)[:, -min(WINDOW, t):]
    win_new = jnp.concatenate([win_all[layer], win_s.reshape(bs, 1, 2, KV_HEADS, HEAD_DIM)], axis=1)[:, -nw:]
    return xp, xs, kv_p, kv_s, win_p, win_new


def _moba_layer(xp, xs, b, t, cache, layer, page_table, norm_gain, w_in, w_out, q_gain, k_gain):
    bs = xs.shape[0]
    n_pages, psz = page_table.shape[1], cache.shape[2]
    past = n_pages * psz
    w = w_in.astype(BF16)
    w_out = w_out.astype(BF16)
    tiles = [("rope", 0, None)] * 4 + [("rope", 1, (0, 0)), ("plain", 1, (0, KV_HEADS))]
    cache_rows = (2 * KV_HEADS,)
    ones = jnp.ones((HEAD_DIM,), F32)
    hg = jnp.stack([q_gain] * 4 + [k_gain, ones])[:, None, :]
    outs = [(4, BF16), (2, F32)]

    cos_p, sin_p = _rope_tables(jnp.arange(t, dtype=jnp.int32))
    cos_pt, sin_pt = jnp.tile(cos_p, (b, 1)), jnp.tile(sin_p, (b, 1))
    q, kv, kv_c = _norm_proj(xp, norm_gain, w, hg, cos_pt, sin_pt, tiles, outs, _tile_rows(b * t, PROJ_ROWS),
                             cache_rows=cache_rows)
    o = _moba_prompt_attn(q, kv, b, t)
    xp = _out_proj(o, w_out, xp, _tile_rows(b * t, PROJ_ROWS))

    cos_s, sin_s = _rope_tables(jnp.full((bs,), past, jnp.int32))
    qs, _, kv_s = _norm_proj(xs, norm_gain, w, hg, cos_s, sin_s, tiles, outs, bs, cache_rows=cache_rows)
    cache2 = cache.reshape(cache.shape[0], cache.shape[1], psz * 2 * KV_HEADS, HEAD_DIM)
    idx = _moba_sample_gate(cache2, layer, page_table, qs.reshape(bs, KV_HEADS, GROUP, HEAD_DIM))
    idx = idx.reshape(bs, N_HEADS, 128)[:, :, :MOBA_TOPK].reshape(bs, N_HEADS * MOBA_TOPK)
    o_s = _moba_sample_attn(cache2, layer, page_table, idx, qs.reshape(bs, N_HEADS, 1, HEAD_DIM),
                            kv_s.reshape(bs, 2 * KV_HEADS, 1, HEAD_DIM))
    xs = _out_proj(o_s.reshape(bs, N_HEADS * HEAD_DIM), w_out, xs, bs)
    return (xp, xs, kv_c.reshape(b, t, 2, KV_HEADS, HEAD_DIM), kv_s.reshape(bs, 1, 2, KV_HEADS, HEAD_DIM))


def _sb_layer(xp, xs, b, t, cache, layer, page_table, norm_gain, w_in, w_out):
    bs = xs.shape[0]
    psz = cache.shape[2]
    aw = N_HEADS * HEAD_DIM
    w = w_in.astype(BF16)
    w_out = w_out.astype(BF16)
    n_tiles = w.shape[1] // 512
    heads_per_tile = 512 // HEAD_DIM
    q_tiles = aw // 512
    tiles = [("plain", 0, None)] * q_tiles + [("plain", 0, (0, (jt - q_tiles) * heads_per_tile))
                                              for jt in range(q_tiles, n_tiles)]
    cache_rows = (2 * N_HEADS,)
    hg = jnp.ones((n_tiles, 1, HEAD_DIM), F32)
    outs = [(n_tiles, F32)]
    dummy = jnp.zeros((b * t, HEAD_DIM), F32)
    qkv, kv_c = _norm_proj(xp, norm_gain, w, hg, dummy, dummy, tiles, outs, _tile_rows(b * t, PROJ_ROWS),
                           cache_rows=cache_rows)
    o = _sb_prompt_attn(qkv, b, t)
    xp = _out_proj(o, w_out, xp, _tile_rows(b * t, PROJ_ROWS))

    qkv_s, kv_s = _norm_proj(xs, norm_gain, w, hg, dummy[:bs], dummy[:bs], tiles, outs, bs, cache_rows=cache_rows)
    qs = qkv_s[:, :aw].astype(BF16).reshape(bs, N_HEADS, HEAD_DIM)
    eye = jnp.eye(N_HEADS, 128, dtype=BF16)
    qbd = (qs[:, :, :, None] * eye[None, :, None, :]).reshape(bs, aw, 128)
    cache2 = cache.reshape(cache.shape[0], cache.shape[1], psz * 2 * N_HEADS, HEAD_DIM)
    o_s = _sb_sample_attn(cache2, layer, page_table, qbd)
    xs = _out_proj(o_s.reshape(bs, aw).astype(BF16), w_out, xs, bs)
    return (xp, xs, kv_c.reshape(b, t, 2, N_HEADS, HEAD_DIM), kv_s.reshape(bs, 1, 2, N_HEADS, HEAD_DIM))


def kernel(x_prompt, x_sample, cache_nsa_kv, state_nsa_win, cache_moba_kv, cache_sb_kv, page_table, mix_norm, ffn_norm, nsa_w_in, nsa_w_out, nsa_q_gain, nsa_kc_gain, nsa_ks_gain, nsa_kw_gain, nsa_cmp_pos, nsa_cmp_w1, nsa_cmp_w2, moba_w_in, moba_w_out, moba_q_gain, moba_k_gain, sb_w_in, sb_w_out, ffn_w_gate, ffn_w_up, ffn_w_down):
    b, t, d = x_prompt.shape
    bs, ts, _ = x_sample.shape
    assert ts == 1
    depth = mix_norm.shape[0]
    xp = x_prompt.reshape(b * t, d)
    xs = x_sample.reshape(bs * ts, d)
    nsa_kv_p, nsa_kv_s, nsa_win_p, nsa_win_s = [], [], [], []
    moba_kv_p, moba_kv_s, sb_kv_p, sb_kv_s = [], [], [], []
    wg, wu, wd = ffn_w_gate.astype(BF16), ffn_w_up.astype(BF16), ffn_w_down.astype(BF16)
    for i in range(depth):
        j = i // N_MIXERS
        if i % N_MIXERS == 0:
            xp, xs, kvp, kvs, wp, ws = _nsa_layer(
                xp, xs, b, t, cache_nsa_kv, j, page_table, state_nsa_win, mix_norm[i], nsa_w_in[j], nsa_w_out[j],
                nsa_q_gain[j], nsa_kc_gain[j], nsa_ks_gain[j], nsa_kw_gain[j], nsa_cmp_pos[j], nsa_cmp_w1[j],
                nsa_cmp_w2[j])
            nsa_kv_p.append(kvp)
            nsa_kv_s.append(kvs)
            nsa_win_p.append(wp)
            nsa_win_s.append(ws)
        elif i % N_MIXERS == 1:
            xp, xs, kvp, kvs = _moba_layer(xp, xs, b, t, cache_moba_kv, j, page_table, mix_norm[i], moba_w_in[j],
                                           moba_w_out[j], moba_q_gain[j], moba_k_gain[j])
            moba_kv_p.append(kvp)
            moba_kv_s.append(kvs)
        else:
            xp, xs, kvp, kvs = _sb_layer(xp, xs, b, t, cache_sb_kv, j, page_table, mix_norm[i], sb_w_in[j], sb_w_out[j])
            sb_kv_p.append(kvp)
            sb_kv_s.append(kvs)
        xp = _ffn(xp, ffn_norm[i], wg, wu, wd, i, _tile_rows(b * t))
        xs = _ffn(xs, ffn_norm[i], wg, wu, wd, i, bs * ts)
    return (xp.reshape(b, t, d), xs.reshape(bs, ts, d), jnp.stack(nsa_kv_p), jnp.stack(nsa_kv_s),
            jnp.stack(nsa_win_p), jnp.stack(nsa_win_s), jnp.stack(moba_kv_p), jnp.stack(moba_kv_s),
            jnp.stack(sb_kv_p), jnp.stack(sb_kv_s))
```

```python
import functools
import math

import jax
import jax.numpy as jnp
from jax import lax
from jax.experimental import pallas as pl
from jax.experimental.pallas import tpu as pltpu

N_HEADS = 16
HEAD_DIM = 128
KV_HEADS = 4
GROUP = N_HEADS // KV_HEADS
CMP_BLOCK = 64
N_SELECT = 16
WINDOW = 512
MOBA_BLOCK = 256
MOBA_TOPK = 3
N_MIXERS = 3
ROPE_THETA = 10000.0
RMS_EPS = 1e-6
NEG_INF = -1e30
REMOVED = -3e38
SCALE = HEAD_DIM ** -0.5
LOG2_E = math.log2(math.e)

VMEM_LIMIT = 56 * 1024 * 1024
F32 = jnp.float32
BF16 = jnp.bfloat16


def _cparams(sem):
    return pltpu.CompilerParams(dimension_semantics=sem, vmem_limit_bytes=VMEM_LIMIT)


def _dot(a, b):
    return jnp.dot(a, b, preferred_element_type=F32)


def _dot_nt(a, b):
    return lax.dot_general(a, b, (((1,), (1,)), ((), ())), preferred_element_type=F32)


def _iota(shape, dim):
    return lax.broadcasted_iota(jnp.int32, shape, dim)


def _masked_softmax(s, mask):
    s = jnp.where(mask, s, NEG_INF)
    m = jnp.max(s, axis=-1, keepdims=True)
    e = jnp.exp(s - m)
    p = e * (1.0 / jnp.sum(e, axis=-1, keepdims=True))
    return jnp.where(mask, p, 0.0)


def _attend_group(q_all, k, v, mask):
    n = k.shape[0]
    m = mask.shape[-2]
    heads = q_all.shape[0] // m
    s = _dot_nt(q_all, k).reshape(heads, m, n) * (SCALE * LOG2_E)
    s = jnp.where(mask if mask.ndim == 3 else mask[None], s, NEG_INF)
    e = jnp.exp2(s - jnp.max(s, axis=-1, keepdims=True))
    inv = 1.0 / jnp.sum(e, axis=-1, keepdims=True)
    return _dot(e.reshape(heads * m, n).astype(BF16), v) * inv.reshape(heads * m, 1)


def _head_norm_rope(x, gain, cos, sin):
    y = x * lax.rsqrt(jnp.mean(x * x, axis=-1, keepdims=True) + RMS_EPS) * gain
    return y * cos + pltpu.roll(y, HEAD_DIM // 2, 1) * sin


def _rope_tables(pos):
    half = HEAD_DIM // 2
    inv_freq = ROPE_THETA ** (-jnp.arange(half, dtype=F32) / half)
    ang = pos.astype(F32)[:, None] * inv_freq[None, :]
    cos, sin = jnp.cos(ang), jnp.sin(ang)
    return jnp.concatenate([cos, cos], axis=-1), jnp.concatenate([-sin, sin], axis=-1)


def _rank_select(val, cand, k, axis):
    n = val.shape[axis]
    idx = _iota(val.shape, axis)
    v = jnp.where(cand, val, NEG_INF)
    rank = jnp.zeros(val.shape, jnp.int32)
    for j in range(n):
        vj = lax.slice_in_dim(v, j, j + 1, axis=axis)
        ahead = (vj > v) | ((vj == v) & (idx > j))
        rank = rank + jnp.where(ahead, 1, 0)
    return cand & (rank < k)


def _topk_indices(val, k):
    rows, n = val.shape
    lane = _iota((rows, n), 1).astype(F32)
    out_lane = _iota((rows, 128), 1)
    out = jnp.zeros((rows, 128), F32)
    for i in range(k):
        m = jnp.max(val, axis=-1, keepdims=True)
        pick = jnp.min(jnp.where(val == m, lane, float(n)), axis=-1, keepdims=True)
        out = jnp.where(out_lane == i, pick, out)
        val = jnp.where(lane == pick, REMOVED, val)
    return out.astype(jnp.int32)


def _double_buffer(step, n_steps, copies):
    slot = step % 2

    @pl.when(step == 0)
    def _():
        for c in copies(step, slot):
            c.start()

    @pl.when(step + 1 < n_steps)
    def _():
        for c in copies(step + 1, 1 - slot):
            c.start()

    for c in copies(step, slot):
        c.wait()
    return slot


def _proj_kernel(x_ref, g_ref, w_ref, hg_ref, cos_ref, sin_ref, *refs, tiles, n_out, cache_rows):
    out_refs, cache_refs, h_scr = refs[:n_out], refs[n_out:n_out + len(cache_rows)], refs[-1]
    j = pl.program_id(1)
    tm = x_ref.shape[0]

    @pl.when(j == 0)
    def _():
        x = x_ref[...]
        h = x * lax.rsqrt(jnp.mean(x * x, axis=-1, keepdims=True) + RMS_EPS) * g_ref[...]
        h_scr[...] = h.astype(BF16)

    acc = _dot(h_scr[...], w_ref[...])
    for jt, (mode, oi, cache_at) in enumerate(tiles):
        @pl.when(j == jt)
        def _(mode=mode, oi=oi, cache_at=cache_at):
            o_ref = out_refs[oi]
            for hh in range(acc.shape[1] // HEAD_DIM):
                sl = slice(hh * HEAD_DIM, (hh + 1) * HEAD_DIM)
                y = acc[:, sl]
                if mode == "rope":
                    y = _head_norm_rope(y, hg_ref[0], cos_ref[...], sin_ref[...])
                elif mode == "sigmoid":
                    y = 1.0 / (1.0 + jnp.exp(-y))
                o_ref[:, sl] = y.astype(o_ref.dtype)
                if cache_at is not None:
                    ci, first = cache_at
                    cache_refs[ci][pl.ds(first + hh, tm, stride=cache_rows[ci]), :] = y


def _norm_proj(x, gain, w, head_gains, cos, sin, tiles, outs, tm, tn=512, cache_rows=()):
    m, d = x.shape
    n_tiles = len(tiles)
    assert w.shape == (d, n_tiles * tn) and m % tm == 0
    starts = []
    for oi in range(len(outs)):
        starts.append(min(jt for jt, t in enumerate(tiles) if t[1] == oi))

    def out_map(oi):
        s, cnt = starts[oi], outs[oi][0]
        return lambda i, j: (i, jnp.clip(j - s, 0, cnt - 1))

    return pl.pallas_call(
        functools.partial(_proj_kernel, tiles=tuple(tiles), n_out=len(outs), cache_rows=tuple(cache_rows)),
        grid=(m // tm, n_tiles),
        in_specs=[
            pl.BlockSpec((tm, d), lambda i, j: (i, 0)),
            pl.BlockSpec((1, d), lambda i, j: (0, 0)),
            pl.BlockSpec((d, tn), lambda i, j: (0, j)),
            pl.BlockSpec((1, 1, HEAD_DIM), lambda i, j: (j, 0, 0)),
            pl.BlockSpec((tm, HEAD_DIM), lambda i, j: (i, 0)),
            pl.BlockSpec((tm, HEAD_DIM), lambda i, j: (i, 0)),
        ],
        out_specs=[pl.BlockSpec((tm, tn), out_map(oi)) for oi in range(len(outs))] + [
            pl.BlockSpec((tm * r, HEAD_DIM), lambda i, j: (i, 0), pipeline_mode=pl.Buffered(1)) for r in cache_rows],
        out_shape=[jax.ShapeDtypeStruct((m, cnt * tn), dt) for cnt, dt in outs] + [
            jax.ShapeDtypeStruct((m * r, HEAD_DIM), F32) for r in cache_rows],
        scratch_shapes=[pltpu.VMEM((tm, d), BF16)],
        compiler_params=_cparams(("parallel", "arbitrary")),
        name="norm_proj",
    )(x, gain.reshape(1, d), w, head_gains, cos, sin)


def _out_proj_kernel(o_ref, w_ref, x_ref, y_ref):
    y_ref[...] = x_ref[...] + _dot(o_ref[...], w_ref[...])


def _out_proj(o, w, x, tm, tn=512):
    m, k = o.shape
    n = w.shape[1]
    return pl.pallas_call(
        _out_proj_kernel,
        grid=(m // tm, n // tn),
        in_specs=[
            pl.BlockSpec((tm, k), lambda i, j: (i, 0)),
            pl.BlockSpec((k, tn), lambda i, j: (0, j)),
            pl.BlockSpec((tm, tn), lambda i, j: (i, j)),
        ],
        out_specs=pl.BlockSpec((tm, tn), lambda i, j: (i, j)),
        out_shape=jax.ShapeDtypeStruct((m, n), F32),
        compiler_params=_cparams(("parallel", "arbitrary")),
        name="out_proj",
    )(o, w, x)


def _ffn_kernel(x_ref, g_ref, wg_ref, wu_ref, wd_ref, y_ref, h_scr, acc_scr):
    f = pl.program_id(1)

    @pl.when(f == 0)
    def _():
        x = x_ref[...]
        h = x * lax.rsqrt(jnp.mean(x * x, axis=-1, keepdims=True) + RMS_EPS) * g_ref[...]
        h_scr[...] = h.astype(BF16)
        acc_scr[...] = jnp.zeros_like(acc_scr)

    h = h_scr[...]
    a = _dot(h, wg_ref[...])
    u = _dot(h, wu_ref[...])
    act = (a * (1.0 / (1.0 + jnp.exp(-a))) * u).astype(BF16)
    acc_scr[...] += _dot(act, wd_ref[...])

    @pl.when(f == pl.num_programs(1) - 1)
    def _():
        y_ref[...] = x_ref[...] + acc_scr[...]


def _ffn(x, gain, wg, wu, wd, layer, tm, tf=512):
    m, d = x.shape
    dff = wg.shape[2]
    assert dff % tf == 0 and m % tm == 0
    return pl.pallas_call(
        _ffn_kernel,
        grid=(m // tm, dff // tf),
        in_specs=[
            pl.BlockSpec((tm, d), lambda i, f: (i, 0)),
            pl.BlockSpec((1, d), lambda i, f: (0, 0)),
            pl.BlockSpec((None, d, tf), lambda i, f: (layer, 0, f)),
            pl.BlockSpec((None, d, tf), lambda i, f: (layer, 0, f)),
            pl.BlockSpec((None, tf, d), lambda i, f: (layer, f, 0)),
        ],
        out_specs=pl.BlockSpec((tm, d), lambda i, f: (i, 0)),
        out_shape=jax.ShapeDtypeStruct((m, d), F32),
        scratch_shapes=[pltpu.VMEM((tm, d), BF16), pltpu.VMEM((tm, d), F32)],
        compiler_params=_cparams(("parallel", "arbitrary")),
        name="ffn",
    )(x, gain.reshape(1, d), wg, wu, wd)


def _gelu_tanh(x):
    return 0.5 * x * (1.0 + jnp.tanh(math.sqrt(2.0 / math.pi) * (x + 0.044715 * (x * x * x))))


def _compress_slab(load, pos_ref, w1_ref, w2_ref, gain_ref, cos_ref, sin_ref, ok_ref, ov_ref, nblk):
    jc = 8

    for s, o_ref in ((0, ok_ref), (1, ov_ref)):
        def body(jj, acc, s=s):
            pieces = []
            for g in range(KV_HEADS):
                cols = []
                for jo in range(jc):
                    j = jj * jc + jo
                    xj = load(s * KV_HEADS + g, j)
                    cols.append((xj + pos_ref[s, pl.ds(j, 1), :]).astype(BF16))
                pieces.append(jnp.concatenate(cols, axis=1))
            lhs = jnp.concatenate(pieces, axis=0)
            k0 = pl.multiple_of(jj * (jc * HEAD_DIM), jc * HEAD_DIM)
            return acc + _dot(lhs, w1_ref[s, pl.ds(k0, jc * HEAD_DIM), :])

        hidden = lax.fori_loop(0, CMP_BLOCK // jc, body, jnp.zeros((KV_HEADS * nblk, w1_ref.shape[2]), F32))
        y = _dot(_gelu_tanh(hidden).astype(BF16), w2_ref[s])
        for g in range(KV_HEADS):
            yg = y[g * nblk:(g + 1) * nblk]
            if s == 0:
                yg = _head_norm_rope(yg, gain_ref[...], cos_ref[...], sin_ref[...])
            o_ref[g] = yg.astype(o_ref.dtype)


def _compress_prompt_kernel(*refs, nblk):
    n_col = 2 * KV_HEADS
    load = lambda c, j: refs[c][pl.ds(j, nblk, stride=CMP_BLOCK), :]
    _compress_slab(load, *refs[n_col:], nblk)


def _compress_prompt(rows, b, t, pos_emb, w1, w2, kc_gain, cos_c, sin_c):
    nc = t // CMP_BLOCK
    n_col = 2 * KV_HEADS
    out = jax.ShapeDtypeStruct((b, KV_HEADS, nc, HEAD_DIM), BF16)
    return pl.pallas_call(
        functools.partial(_compress_prompt_kernel, nblk=nc),
        grid=(b,),
        in_specs=[pl.BlockSpec((t, HEAD_DIM), lambda i, c=c: (i, c)) for c in range(n_col)] + [
            pl.BlockSpec(pos_emb.shape, lambda i: (0, 0, 0)),
            pl.BlockSpec(w1.shape, lambda i: (0, 0, 0)),
            pl.BlockSpec(w2.shape, lambda i: (0, 0, 0)),
            pl.BlockSpec((1, HEAD_DIM), lambda i: (0, 0)),
            pl.BlockSpec((nc, HEAD_DIM), lambda i: (0, 0)),
            pl.BlockSpec((nc, HEAD_DIM), lambda i: (0, 0)),
        ],
        out_specs=[pl.BlockSpec((None, KV_HEADS, nc, HEAD_DIM), lambda i: (i, 0, 0, 0))] * 2,
        out_shape=[out, out],
        compiler_params=_cparams(("arbitrary",)),
        name="nsa_compress_prompt",
    )(*([rows] * n_col), pos_emb, w1, w2, kc_gain.reshape(1, HEAD_DIM), cos_c, sin_c)


def _compress_sample_kernel(pt_ref, cache_ref, pos_ref, w1_ref, w2_ref, gain_ref, cos_ref, sin_ref,
                            ok_ref, ov_ref, slab, sems, y_scr, *, layer, pages, n_seq, n_chunks):
    step = pl.program_id(0) * n_chunks + pl.program_id(1)
    n_col = slab.shape[1]
    psz = slab.shape[0] // (2 * pages)
    nblk = pages * psz // CMP_BLOCK
    m = nblk * n_col

    def copies(st, slot):
        b, c = st // n_chunks, st % n_chunks
        return [pltpu.make_async_copy(cache_ref.at[layer, pt_ref[b, c * pages + p], :, 0],
                                      slab.at[pl.ds((slot * pages + p) * psz, psz)], sems.at[slot, p])
                for p in range(pages)]

    slot = _double_buffer(step, n_seq * n_chunks, copies)
    base = slot * (pages * psz)
    jc = 8

    hidden = [jnp.zeros((m, w1_ref.shape[2]), F32)] * 2
    for jj in range(CMP_BLOCK // jc):
        cols = []
        for j in range(jj * jc, (jj + 1) * jc):
            xj = slab[pl.ds(base + j, nblk, stride=CMP_BLOCK)] + pos_ref[j:j + 1]
            cols.append(xj.reshape(m, HEAD_DIM).astype(BF16))
        lhs = jnp.concatenate(cols, axis=1)
        rows = slice(jj * jc * HEAD_DIM, (jj + 1) * jc * HEAD_DIM)
        hidden = [hidden[s] + _dot(lhs, w1_ref[s, rows, :]) for s in range(2)]

    is_key = (_iota((m, 1), 0) & (n_col - 1)) < KV_HEADS
    act = _gelu_tanh(jnp.where(is_key, hidden[0], hidden[1])).astype(BF16)
    y_scr[...] = jnp.where(is_key, _dot(act, w2_ref[0]), _dot(act, w2_ref[1]))
    for g in range(KV_HEADS):
        yk = y_scr[pl.ds(g, nblk, stride=n_col), :]
        ok_ref[g] = _head_norm_rope(yk, gain_ref[...], cos_ref[...], sin_ref[...]).astype(ok_ref.dtype)
        ov_ref[g] = y_scr[pl.ds(KV_HEADS + g, nblk, stride=n_col), :].astype(ov_ref.dtype)


def _compress_sample(cache, layer, page_table, pos_emb, w1, w2, kc_gain, cos_c, sin_c, pages=16):
    b, n_pages = page_table.shape
    psz = cache.shape[2]
    pages = min(pages, n_pages)
    assert n_pages % pages == 0 and psz % CMP_BLOCK == 0
    nblk = pages * psz // CMP_BLOCK
    nc = n_pages * psz // CMP_BLOCK
    n_col = 2 * KV_HEADS
    pos_rows = jnp.repeat(jnp.swapaxes(pos_emb, 0, 1), KV_HEADS, axis=1)
    out = jax.ShapeDtypeStruct((b, KV_HEADS, nc, HEAD_DIM), BF16)
    grid_spec = pltpu.PrefetchScalarGridSpec(
        num_scalar_prefetch=1,
        grid=(b, n_pages // pages),
        in_specs=[
            pl.BlockSpec(memory_space=pl.ANY),
            pl.BlockSpec(pos_rows.shape, lambda i, c, pt: (0, 0, 0)),
            pl.BlockSpec(w1.shape, lambda i, c, pt: (0, 0, 0)),
            pl.BlockSpec(w2.shape, lambda i, c, pt: (0, 0, 0)),
            pl.BlockSpec((1, HEAD_DIM), lambda i, c, pt: (0, 0)),
            pl.BlockSpec((nblk, HEAD_DIM), lambda i, c, pt: (c, 0)),
            pl.BlockSpec((nblk, HEAD_DIM), lambda i, c, pt: (c, 0)),
        ],
        out_specs=[pl.BlockSpec((None, KV_HEADS, nblk, HEAD_DIM), lambda i, c, pt: (i, 0, c, 0))] * 2,
        scratch_shapes=[pltpu.VMEM((2 * pages * psz, n_col, HEAD_DIM), F32), pltpu.SemaphoreType.DMA((2, pages)),
                        pltpu.VMEM((nblk * n_col, HEAD_DIM), F32)],
    )
    return pl.pallas_call(
        functools.partial(_compress_sample_kernel, layer=layer, pages=pages, n_seq=b, n_chunks=n_pages // pages),
        grid_spec=grid_spec,
        out_shape=[out, out],
        compiler_params=_cparams(("arbitrary", "arbitrary")),
        name="nsa_compress_sample",
    )(page_table, cache, pos_rows, w1, w2, kc_gain.reshape(1, HEAD_DIM), cos_c, sin_c)


def _nsa_prompt_kernel(q_ref, ks_ref, vs_ref, kw_ref, vw_ref, kc_ref, vc_ref, gt_ref, o_ref,
                       ks_s, vs_s, kw_s, vw_s, *, t_len, tq):
    qi = pl.program_id(2)

    @pl.when(qi == 0)
    def _():
        ks_s[...] = ks_ref[...].astype(BF16)
        vs_s[...] = vs_ref[...].astype(BF16)
        kw_s[...] = kw_ref[...].astype(BF16)
        vw_s[...] = vw_ref[...].astype(BF16)

    nc = kc_ref.shape[0]
    t0 = qi * tq
    tpos = t0 + _iota((tq, 1), 0)
    qb = q_ref[...]
    gt = gt_ref[...]
    kc, vc = kc_ref[...], vc_ref[...]
    q_all = jnp.concatenate([qb[:, r * HEAD_DIM:(r + 1) * HEAD_DIM] for r in range(GROUP)], axis=0)

    cmask = ((_iota((tq, nc), 1) * CMP_BLOCK + (CMP_BLOCK - 1)) <= tpos)[None]
    s_cmp = (_dot_nt(q_all, kc) * SCALE).reshape(GROUP, tq, nc)
    o_cmp = _dot(_masked_softmax(s_cmp, cmask).reshape(GROUP * tq, nc).astype(BF16), vc)
    tpos_t = t0 + _iota((nc, tq), 1)
    cblk_t = _iota((nc, tq), 0)
    cmask_t = (cblk_t * CMP_BLOCK + (CMP_BLOCK - 1)) <= tpos_t
    s_t_all = _dot_nt(kc, q_all) * SCALE
    imp_t = jnp.zeros((nc, tq), F32)
    for r in range(GROUP):
        s_t = jnp.where(cmask_t, s_t_all[:, r * tq:(r + 1) * tq], NEG_INF)
        e_t = jnp.exp(s_t - jnp.max(s_t, axis=0, keepdims=True))
        imp_t = imp_t + jnp.where(cmask_t, e_t * (1.0 / jnp.sum(e_t, axis=0, keepdims=True)), 0.0)

    cur_t = tpos_t >> 6
    sel_t = _rank_select(imp_t, cblk_t < cur_t, N_SELECT - 1, axis=0) | (cblk_t == cur_t)
    sel_t = jnp.where(sel_t, 1.0, 0.0).astype(BF16)

    tw = 128 if tq % 128 == 0 else tq
    wn = min(WINDOW + tw, t_len)
    o_win_parts = []
    for part in range(tq // tw):
        w0 = pl.multiple_of(jnp.maximum(t0 + (part + 1) * tw - wn, 0), tw)
        wd = (t0 + part * tw + _iota((tw, 1), 0)) - (w0 + _iota((tw, wn), 1))
        q_part = jnp.concatenate([q_all[r * tq + part * tw:r * tq + (part + 1) * tw] for r in range(GROUP)], axis=0)
        o_win_parts.append(_attend_group(q_part, kw_s[pl.ds(w0, wn), :], vw_s[pl.ds(w0, wn), :],
                                         (wd >= 0) & (wd < WINDOW)))
    o_win = jnp.concatenate([o_part[r * tw:(r + 1) * tw] for r in range(GROUP) for o_part in o_win_parts], axis=0)

    n_prefix = t_len // tq
    step = t_len // n_prefix
    for pi in range(n_prefix):
        @pl.when((t0 + tq - 1) // step == pi)
        def _(lk=(pi + 1) * step):
            expand = jnp.where((_iota((nc, lk), 1) >> 6) == _iota((nc, lk), 0), 1.0, 0.0).astype(BF16)
            picked = lax.dot_general(sel_t, expand, (((0,), (0,)), ((), ())), preferred_element_type=F32)
            smask = (picked > 0.5) & (_iota((tq, lk), 1) <= tpos)
            o_slc = _attend_group(q_all, ks_s[0:lk, :], vs_s[0:lk, :], smask)
            for r in range(GROUP):
                rows = slice(r * tq, (r + 1) * tq)
                o = (gt[:, 3 * r:3 * r + 1] * o_cmp[rows] + gt[:, 3 * r + 1:3 * r + 2] * o_slc[rows]
                     + gt[:, 3 * r + 2:3 * r + 3] * o_win[rows])
                o_ref[:, r * HEAD_DIM:(r + 1) * HEAD_DIM] = o.astype(o_ref.dtype)


def _nsa_prompt_attn(q, rows, win, k_cmp, v_cmp, gates, b, t, tq=256):
    nq = t // tq
    nc = t // CMP_BLOCK
    gw = GROUP * HEAD_DIM
    kv_spec = lambda col0: pl.BlockSpec((t, HEAD_DIM), lambda i, g, qi: (i, col0 + g))
    cmp_spec = pl.BlockSpec((None, None, nc, HEAD_DIM), lambda i, g, qi: (i, g, 0, 0))
    return pl.pallas_call(
        functools.partial(_nsa_prompt_kernel, t_len=t, tq=tq),
        grid=(b, KV_HEADS, nq),
        in_specs=[
            pl.BlockSpec((tq, gw), lambda i, g, qi: (i * nq + qi, g)),
            kv_spec(2 * KV_HEADS), kv_spec(3 * KV_HEADS), kv_spec(0), kv_spec(KV_HEADS),
            cmp_spec, cmp_spec,
            pl.BlockSpec((tq, HEAD_DIM), lambda i, g, qi: (i * nq + qi, g)),
        ],
        out_specs=pl.BlockSpec((tq, gw), lambda i, g, qi: (i * nq + qi, g)),
        out_shape=jax.ShapeDtypeStruct((b * t, N_HEADS * HEAD_DIM), BF16),
        scratch_shapes=[pltpu.VMEM((t, HEAD_DIM), BF16)] * 4,
        compiler_params=_cparams(("parallel", "parallel", "arbitrary")),
        name="nsa_prompt_attn",
    )(q, rows, rows, win, win, k_cmp, v_cmp, gates)


def _nsa_sample_cmp_kernel(q_ref, kc_ref, vc_ref, o_ref, idx_ref, *, pos):
    nc = kc_ref.shape[1]
    cblk = _iota((GROUP, nc), 1)
    cmask = (cblk * CMP_BLOCK + (CMP_BLOCK - 1)) <= pos
    group_row = _iota((KV_HEADS, nc), 0)
    imp = jnp.zeros((KV_HEADS, nc), F32)
    for g in range(KV_HEADS):
        p = _masked_softmax(_dot_nt(q_ref[g], kc_ref[g]) * SCALE, cmask)
        o_ref[g] = _dot(p.astype(BF16), vc_ref[g])
        imp = jnp.where(group_row == g, jnp.sum(p, axis=0, keepdims=True), imp)
    cand = _iota((KV_HEADS, nc), 1) < (pos // CMP_BLOCK)
    idx_ref[...] = _topk_indices(jnp.where(cand, imp, NEG_INF), N_SELECT - 1)


def _nsa_sample_cmp(q4, k_cmp, v_cmp, pos):
    b = q4.shape[0]
    nc = k_cmp.shape[2]
    assert nc >= N_SELECT - 1 and pos // CMP_BLOCK >= N_SELECT - 1
    spec4 = pl.BlockSpec((None, KV_HEADS, GROUP, HEAD_DIM), lambda i: (i, 0, 0, 0))
    cspec = pl.BlockSpec((None, KV_HEADS, nc, HEAD_DIM), lambda i: (i, 0, 0, 0))
    return pl.pallas_call(
        functools.partial(_nsa_sample_cmp_kernel, pos=pos),
        grid=(b,),
        in_specs=[spec4, cspec, cspec],
        out_specs=[spec4, pl.BlockSpec((None, KV_HEADS, 128), lambda i: (i, 0, 0))],
        out_shape=[jax.ShapeDtypeStruct((b, KV_HEADS, GROUP, HEAD_DIM), F32),
                   jax.ShapeDtypeStruct((b, KV_HEADS, 128), jnp.int32)],
        compiler_params=_cparams(("parallel",)),
        name="nsa_sample_cmp",
    )(q4, k_cmp, v_cmp)


def _softmax_with_extra(s, s_new):
    m = jnp.maximum(jnp.max(s, axis=-1, keepdims=True), s_new)
    e, e_new = jnp.exp(s - m), jnp.exp(s_new - m)
    inv = 1.0 / (jnp.sum(e, axis=-1, keepdims=True) + e_new)
    return e * inv, e_new * inv


def _row_dot(q, k_row):
    return jnp.sum(q.astype(F32) * k_row.astype(BF16).astype(F32), axis=-1, keepdims=True)


def _nsa_sample_slc_kernel(pt_ref, idx_ref, q_ref, cache_ref, knew_ref, vnew_ref,
                           win_ref, kwnew_ref, vwnew_ref, ocmp_ref, gt_ref, o_ref,
                           slab, sems, *, layer, n_sel, halves, n_seq):
    g = pl.program_id(1)
    step = pl.program_id(0) * KV_HEADS + g
    per_row = 4 * KV_HEADS
    kv_rows = 2 * KV_HEADS
    rows = slab.shape[0] // 2
    blk_rows = CMP_BLOCK * per_row

    def copies(st, slot):
        b, gg = st // KV_HEADS, st % KV_HEADS
        out = []
        for n in range(n_sel):
            blk = idx_ref[b, gg * n_sel + n]
            src = cache_ref.at[layer, pt_ref[b, blk // halves] * halves + blk % halves]
            out.append(pltpu.make_async_copy(src, slab.at[pl.ds((slot * n_sel + n) * blk_rows, blk_rows)],
                                             sems.at[slot, n]))
        return out

    slot = _double_buffer(step, n_seq * KV_HEADS, copies)
    base = slot * rows
    n_keys = n_sel * CMP_BLOCK
    q = q_ref[...]
    k_sel = slab[pl.ds(base + 2 * KV_HEADS + g, n_keys, stride=per_row), :].astype(BF16)
    v_sel = slab[pl.ds(base + 3 * KV_HEADS + g, n_keys, stride=per_row), :].astype(BF16)
    p, p_new = _softmax_with_extra(_dot_nt(q, k_sel) * SCALE, _row_dot(q, knew_ref[...]) * SCALE)
    o_slc = _dot(p.astype(BF16), v_sel) + p_new.astype(BF16).astype(F32) * vnew_ref[...].astype(BF16).astype(F32)

    nw = win_ref.shape[0] // kv_rows
    kwin = win_ref[pl.ds(g, nw, stride=kv_rows), :].astype(BF16)
    vwin = win_ref[pl.ds(KV_HEADS + g, nw, stride=kv_rows), :].astype(BF16)
    sw = jnp.where(nw - _iota((GROUP, nw), 1) < WINDOW, _dot_nt(q, kwin) * SCALE, NEG_INF)
    p, p_new = _softmax_with_extra(sw, _row_dot(q, kwnew_ref[...]) * SCALE)
    o_win = _dot(p.astype(BF16), vwin) + p_new.astype(BF16).astype(F32) * vwnew_ref[...].astype(BF16).astype(F32)

    gt = gt_ref[...]
    rowi = _iota((GROUP, 1), 0)
    gcol = [sum(jnp.where(rowi == r, gt[:, 3 * r + c:3 * r + c + 1], 0.0) for r in range(GROUP)) for c in range(3)]
    o_ref[...] = gcol[0] * ocmp_ref[...] + gcol[1] * o_slc + gcol[2] * o_win


def _nsa_sample_slc(cache, layer, page_table, idx, q4, rows_s, win_buf, win_layer, win_s, o_cmp, gates_s, halves):
    b = q4.shape[0]
    n_sel = idx.shape[1] // KV_HEADS
    kv_rows = 2 * KV_HEADS
    spec4 = pl.BlockSpec((None, None, GROUP, HEAD_DIM), lambda i, g, pt, ix: (i, g, 0, 0))
    new_spec = lambda col0: pl.BlockSpec((None, None, 1, HEAD_DIM), lambda i, g, pt, ix: (i, col0 + g, 0, 0))
    grid_spec = pltpu.PrefetchScalarGridSpec(
        num_scalar_prefetch=2,
        grid=(b, KV_HEADS),
        in_specs=[
            spec4,
            pl.BlockSpec(memory_space=pl.ANY),
            new_spec(2 * KV_HEADS), new_spec(3 * KV_HEADS),
            pl.BlockSpec((None, None) + win_buf.shape[2:], lambda i, g, pt, ix: (win_layer, i, 0, 0)),
            new_spec(0), new_spec(KV_HEADS),
            spec4,
            pl.BlockSpec((None, None, 1, HEAD_DIM), lambda i, g, pt, ix: (i, g, 0, 0)),
        ],
        out_specs=spec4,
        scratch_shapes=[pltpu.VMEM((2 * n_sel * cache.shape[2], HEAD_DIM), F32),
                        pltpu.SemaphoreType.DMA((2, n_sel))],
    )
    return pl.pallas_call(
        functools.partial(_nsa_sample_slc_kernel, layer=layer, n_sel=n_sel, halves=halves, n_seq=b),
        grid_spec=grid_spec,
        out_shape=jax.ShapeDtypeStruct((b, KV_HEADS, GROUP, HEAD_DIM), F32),
        compiler_params=_cparams(("arbitrary", "arbitrary")),
        name="nsa_sample_slc",
    )(page_table, idx, q4, cache, rows_s, rows_s, win_buf, win_s, win_s, o_cmp, gates_s)


def _moba_prompt_kernel(q_ref, k_ref, v_ref, o_ref, k_s, v_s, km_s, *, t_len, tq):
    qi = pl.program_id(2)
    nb = km_s.shape[0]

    @pl.when(qi == 0)
    def _():
        k = k_ref[...]
        k_s[...] = k.astype(BF16)
        v_s[...] = v_ref[...].astype(BF16)
        km_s[...] = (jnp.sum(k.reshape(nb, MOBA_BLOCK, HEAD_DIM), axis=1) * (1.0 / MOBA_BLOCK)).astype(BF16)

    t0 = qi * tq
    tpos = t0 + _iota((tq, 1), 0)
    cur_t = (t0 + _iota((nb, tq), 1)) >> 8
    blk_t = _iota((nb, tq), 0)
    qb = q_ref[...]
    km = km_s[...]
    q_all = jnp.concatenate([qb[:, r * HEAD_DIM:(r + 1) * HEAD_DIM] for r in range(GROUP)], axis=0)
    gate_t_all = _dot_nt(km, q_all)
    sel_t = []
    for r in range(GROUP):
        sel = _rank_select(gate_t_all[:, r * tq:(r + 1) * tq], blk_t < cur_t, MOBA_TOPK, axis=0) | (blk_t == cur_t)
        sel_t.append(jnp.where(sel, 1.0, 0.0).astype(BF16))

    n_prefix = t_len // tq if tq % MOBA_BLOCK == 0 else 1
    step = t_len // n_prefix
    for pi in range(n_prefix):
        @pl.when((t0 + tq - 1) // step == pi)
        def _(lk=(pi + 1) * step):
            expand = jnp.where((_iota((nb, lk), 1) >> 8) == _iota((nb, lk), 0), 1.0, 0.0).astype(BF16)
            causal = _iota((tq, lk), 1) <= tpos
            picked = jnp.concatenate(
                [lax.dot_general(sel_t[r], expand, (((0,), (0,)), ((), ())), preferred_element_type=F32)
                 for r in range(GROUP)], axis=0).reshape(GROUP, tq, lk)
            o = _attend_group(q_all, k_s[0:lk, :], v_s[0:lk, :], (picked > 0.5) & causal[None])
            for r in range(GROUP):
                o_ref[:, r * HEAD_DIM:(r + 1) * HEAD_DIM] = o[r * tq:(r + 1) * tq].astype(o_ref.dtype)


def _moba_prompt_attn(q, kv, b, t, tq=256):
    assert t % MOBA_BLOCK == 0
    nq = t // tq
    gw = GROUP * HEAD_DIM
    kv_spec = lambda col0: pl.BlockSpec((t, HEAD_DIM), lambda i, g, qi: (i, col0 + g))
    return pl.pallas_call(
        functools.partial(_moba_prompt_kernel, t_len=t, tq=tq),
        grid=(b, KV_HEADS, nq),
        in_specs=[pl.BlockSpec((tq, gw), lambda i, g, qi: (i * nq + qi, g)), kv_spec(0), kv_spec(KV_HEADS)],
        out_specs=pl.BlockSpec((tq, gw), lambda i, g, qi: (i * nq + qi, g)),
        out_shape=jax.ShapeDtypeStruct((b * t, N_HEADS * HEAD_DIM), BF16),
        scratch_shapes=[pltpu.VMEM((t, HEAD_DIM), BF16), pltpu.VMEM((t, HEAD_DIM), BF16),
                        pltpu.VMEM((t // MOBA_BLOCK, HEAD_DIM), BF16)],
        compiler_params=_cparams(("parallel", "parallel", "arbitrary")),
        name="moba_prompt_attn",
    )(q, kv, kv)


def _moba_sample_gate_kernel(pt_ref, q_ref, cache_ref, idx_ref, slab, sems, ksum, *, layer, pages, ppb, n_seq, n_chunks):
    c = pl.program_id(1)
    step = pl.program_id(0) * n_chunks + c
    per_row = 2 * KV_HEADS
    rows = slab.shape[0] // 2
    page_rows = rows // pages

    def copies(st, slot):
        b, cc = st // n_chunks, st % n_chunks
        return [pltpu.make_async_copy(cache_ref.at[layer, pt_ref[b, cc * pages + p]],
                                      slab.at[pl.ds((slot * pages + p) * page_rows, page_rows)], sems.at[slot, p])
                for p in range(pages)]

    slot = _double_buffer(step, n_seq * n_chunks, copies)
    x = slab[pl.ds(pl.multiple_of(slot * rows, rows), rows), :]
    n_blk = pages // ppb
    sums = jnp.sum(x.reshape(n_blk, rows // (n_blk * per_row), per_row, HEAD_DIM), axis=1)
    ksum[pl.ds(pl.multiple_of(c * (n_blk * per_row), n_blk * per_row), n_blk * per_row), :] = sums.reshape(
        n_blk * per_row, HEAD_DIM)

    @pl.when(c == n_chunks - 1)
    def _():
        nb = ksum.shape[0] // per_row
        for g in range(KV_HEADS):
            km = (ksum[pl.ds(g, nb, stride=per_row), :] * (1.0 / MOBA_BLOCK)).astype(BF16)
            idx_ref[g] = _topk_indices(_dot_nt(q_ref[g], km), MOBA_TOPK)


def _moba_sample_gate(cache, layer, page_table, q4, pages=16):
    b, n_pages = page_table.shape
    per_row = 2 * KV_HEADS
    page_rows = cache.shape[2]
    psz = page_rows // per_row
    ppb = MOBA_BLOCK // psz
    nb = n_pages // ppb
    pages = min(pages, n_pages)
    assert MOBA_BLOCK % psz == 0 and n_pages % pages == 0 and pages % ppb == 0 and nb >= MOBA_TOPK
    grid_spec = pltpu.PrefetchScalarGridSpec(
        num_scalar_prefetch=1,
        grid=(b, n_pages // pages),
        in_specs=[
            pl.BlockSpec((None, KV_HEADS, GROUP, HEAD_DIM), lambda i, c, pt: (i, 0, 0, 0)),
            pl.BlockSpec(memory_space=pl.ANY),
        ],
        out_specs=pl.BlockSpec((None, KV_HEADS, GROUP, 128), lambda i, c, pt: (i, 0, 0, 0)),
        scratch_shapes=[pltpu.VMEM((2 * pages * page_rows, HEAD_DIM), F32), pltpu.SemaphoreType.DMA((2, pages)),
                        pltpu.VMEM((nb * per_row, HEAD_DIM), F32)],
    )
    return pl.pallas_call(
        functools.partial(_moba_sample_gate_kernel, layer=layer, pages=pages, ppb=ppb, n_seq=b,
                          n_chunks=n_pages // pages),
        grid_spec=grid_spec,
        out_shape=jax.ShapeDtypeStruct((b, KV_HEADS, GROUP, 128), jnp.int32),
        compiler_params=_cparams(("arbitrary", "arbitrary")),
        name="moba_sample_gate",
    )(page_table, q4, cache)


def _moba_sample_attn_kernel(pt_ref, idx_ref, q_ref, cache_ref, knew_ref, vnew_ref, o_ref, slab, sems,
                             *, layer, n_sel, ppb, n_seq):
    h = pl.program_id(1)
    step = pl.program_id(0) * N_HEADS + h
    per_row = 2 * KV_HEADS
    n_pg = n_sel * ppb
    rows = slab.shape[0] // 2
    page_rows = rows // n_pg

    def copies(st, slot):
        b, hh = st // N_HEADS, st % N_HEADS
        out = []
        for n in range(n_pg):
            page = pt_ref[b, idx_ref[b, hh * n_sel + n // ppb] * ppb + n % ppb]
            out.append(pltpu.make_async_copy(cache_ref.at[layer, page],
                                             slab.at[pl.ds((slot * n_pg + n) * page_rows, page_rows)],
                                             sems.at[slot, n]))
        return out

    slot = _double_buffer(step, n_seq * N_HEADS, copies)
    base = slot * rows
    g = h // GROUP
    n_keys = rows // per_row
    q = q_ref[...]
    k_sel = slab[pl.ds(base + g, n_keys, stride=per_row), :].astype(BF16)
    v_sel = slab[pl.ds(base + KV_HEADS + g, n_keys, stride=per_row), :].astype(BF16)
    p, p_new = _softmax_with_extra(_dot_nt(q, k_sel) * SCALE, _row_dot(q, knew_ref[...]) * SCALE)
    o = _dot(p.astype(BF16), v_sel) + p_new.astype(BF16).astype(F32) * vnew_ref[...].astype(BF16).astype(F32)
    o_ref[...] = o.astype(o_ref.dtype)


def _moba_sample_attn(cache, layer, page_table, idx, q16, kv_new):
    b = q16.shape[0]
    page_rows = cache.shape[2]
    ppb = MOBA_BLOCK // (page_rows // (2 * KV_HEADS))
    n_sel = idx.shape[1] // N_HEADS
    n_pg = n_sel * ppb
    hspec = pl.BlockSpec((None, None, 1, HEAD_DIM), lambda i, h, pt, ix: (i, h, 0, 0))
    new_spec = lambda col0: pl.BlockSpec((None, None, 1, HEAD_DIM), lambda i, h, pt, ix: (i, col0 + h // GROUP, 0, 0))
    grid_spec = pltpu.PrefetchScalarGridSpec(
        num_scalar_prefetch=2,
        grid=(b, N_HEADS),
        in_specs=[hspec, pl.BlockSpec(memory_space=pl.ANY), new_spec(0), new_spec(KV_HEADS)],
        out_specs=hspec,
        scratch_shapes=[pltpu.VMEM((2 * n_pg * page_rows, HEAD_DIM), F32), pltpu.SemaphoreType.DMA((2, n_pg))],
    )
    return pl.pallas_call(
        functools.partial(_moba_sample_attn_kernel, layer=layer, n_sel=n_sel, ppb=ppb, n_seq=b),
        grid_spec=grid_spec,
        out_shape=jax.ShapeDtypeStruct((b, N_HEADS, 1, HEAD_DIM), BF16),
        compiler_params=_cparams(("arbitrary", "arbitrary")),
        name="moba_sample_attn",
    )(page_table, idx, q16, cache, kv_new, kv_new)


def _log_sigmoid(z):
    return jnp.minimum(z, 0.0) - jnp.log(1.0 + jnp.exp(-jnp.abs(z)))


def _split2(x):
    hi = x.astype(BF16)
    return hi, (x - hi.astype(F32)).astype(BF16)


EXP_UNDERFLOW = -104.0


def _sb_prompt_kernel(q_ref, k_ref, v_ref, o_ref, k_s, v_s, acc_scr, *, tq):
    qi = pl.program_id(2)

    @pl.when(qi == 0)
    def _():
        k_s[...] = k_ref[...].astype(BF16)
        v_s[...] = v_ref[...].astype(BF16)

    hp = acc_scr.shape[0]
    lanes = lambda h: slice(h * HEAD_DIM, (h + 1) * HEAD_DIM)
    qs = [q_ref[:, lanes(h)].astype(BF16) for h in range(hp)]
    after = jnp.where(_iota((tq, tq), 0) > _iota((tq, tq), 1), 1.0, 0.0).astype(BF16)
    below = (_iota((hp * tq, tq), 0) & (tq - 1)) > _iota((hp * tq, tq), 1)
    acc_scr[...] = jnp.zeros_like(acc_scr)

    def chunk(c, run, diagonal):
        s0 = pl.multiple_of(c * tq, tq)
        z = jnp.concatenate([_dot_nt(qs[h], k_s[pl.ds(s0, tq), lanes(h)]) for h in range(hp)], axis=0) * SCALE
        log_beta = _log_sigmoid(z)
        log_keep = log_beta - z
        if diagonal:
            log_keep = jnp.where(below, log_keep, 0.0)
        both = _dot(jnp.concatenate(_split2(log_keep), axis=0), after)
        within = both[:hp * tq] + both[hp * tq:]
        a = jnp.exp(log_beta + within + run)
        if diagonal:
            a = jnp.where(below, a, 0.0)
        a = a.astype(BF16)
        for h in range(hp):
            acc_scr[h] += _dot(a[h * tq:(h + 1) * tq], v_s[pl.ds(s0, tq), lanes(h)])
        return run + within[:, 0:1] + log_keep[:, 0:1]

    run = chunk(qi, jnp.zeros((hp * tq, 1), F32), True)

    def cond(carry):
        c, run = carry
        return (c >= 0) & (jnp.max(run) >= EXP_UNDERFLOW)

    def body(carry):
        c, run = carry
        return c - 1, chunk(c, run, False)

    lax.while_loop(cond, body, (qi - 1, run))
    for h in range(hp):
        o_ref[:, lanes(h)] = acc_scr[h].astype(o_ref.dtype)


def _sb_prompt_attn(qkv, b, t, tq=256, hp=2):
    nq = t // tq
    n_hp = N_HEADS // hp
    w = hp * HEAD_DIM
    assert tq & (tq - 1) == 0
    return pl.pallas_call(
        functools.partial(_sb_prompt_kernel, tq=tq),
        grid=(b, n_hp, nq),
        in_specs=[
            pl.BlockSpec((tq, w), lambda i, h, qi: (i * nq + qi, h)),
            pl.BlockSpec((t, w), lambda i, h, qi: (i, n_hp + h)),
            pl.BlockSpec((t, w), lambda i, h, qi: (i, 2 * n_hp + h)),
        ],
        out_specs=pl.BlockSpec((tq, w), lambda i, h, qi: (i * nq + qi, h)),
        out_shape=jax.ShapeDtypeStruct((b * t, N_HEADS * HEAD_DIM), BF16),
        scratch_shapes=[pltpu.VMEM((t, w), BF16)] * 2 + [pltpu.VMEM((hp, tq, HEAD_DIM), F32)],
        compiler_params=_cparams(("parallel", "parallel", "arbitrary")),
        name="sb_prompt_attn",
    )(qkv, qkv, qkv)


def _sb_sample_kernel(pt_ref, qbd_ref, cache_ref, o_ref, buf, sems, acc_scr, *, layer, n_pages):
    b = pl.program_id(0)
    per_row = 2 * N_HEADS
    rows = buf.shape[0] // 2
    psz = rows // per_row

    def page_copy(p, slot):
        return pltpu.make_async_copy(cache_ref.at[layer, pt_ref[b, n_pages - 1 - p]],
                                     buf.at[pl.ds(slot * rows, rows)], sems.at[slot])

    acc_scr[...] = jnp.zeros_like(acc_scr)
    after = jnp.where(_iota((psz, psz), 1) > _iota((psz, psz), 0), 1.0, 0.0).astype(BF16)
    live = _iota((1, 128), 1) < N_HEADS
    page_copy(0, 0).start()

    def cond(carry):
        p, run = carry
        return (p < n_pages) & (jnp.max(jnp.where(live, run, NEG_INF)) >= EXP_UNDERFLOW)

    def body(carry):
        p, run = carry
        slot = p % 2
        page_copy(p, slot).wait()

        @pl.when(p + 1 < n_pages)
        def _():
            page_copy(p + 1, 1 - slot).start()

        base = slot * rows

        def heads_wide(first):
            return jnp.concatenate([buf[pl.ds(base + first + h, psz, stride=per_row), :].astype(BF16)
                                    for h in range(N_HEADS)], axis=1)

        z = _dot(heads_wide(0), qbd_ref[...]) * SCALE
        log_beta = _log_sigmoid(z)
        log_keep = log_beta - z
        hi, lo = _split2(log_keep)
        within = _dot(after, hi) + _dot(after, lo)
        a = jnp.exp(log_beta + within + run)
        acc_scr[...] += _dot(a.T.astype(BF16), heads_wide(N_HEADS))
        return p + 1, run + within[0:1, :] + log_keep[0:1, :]

    p_end, _ = lax.while_loop(cond, body, (0, jnp.zeros((1, 128), F32)))

    @pl.when(p_end < n_pages)
    def _():
        page_copy(p_end, p_end % 2).wait()

    acc = acc_scr[...]
    own = _iota(acc.shape, 0) == (_iota(acc.shape, 1) >> 7)
    o_ref[...] = jnp.sum(jnp.where(own, acc, 0.0), axis=0, keepdims=True)


def _sb_sample_attn(cache, layer, page_table, qbd):
    b, n_pages = page_table.shape
    aw = N_HEADS * HEAD_DIM
    grid_spec = pltpu.PrefetchScalarGridSpec(
        num_scalar_prefetch=1,
        grid=(b,),
        in_specs=[
            pl.BlockSpec((None, aw, 128), lambda i, pt: (i, 0, 0)),
            pl.BlockSpec(memory_space=pl.ANY),
        ],
        out_specs=pl.BlockSpec((None, 1, aw), lambda i, pt: (i, 0, 0)),
        scratch_shapes=[pltpu.VMEM((2 * cache.shape[2], HEAD_DIM), F32), pltpu.SemaphoreType.DMA((2,)),
                        pltpu.VMEM((128, aw), F32)],
    )
    return pl.pallas_call(
        functools.partial(_sb_sample_kernel, layer=layer, n_pages=n_pages),
        grid_spec=grid_spec,
        out_shape=jax.ShapeDtypeStruct((b, 1, aw), F32),
        compiler_params=_cparams(("arbitrary",)),
        name="sb_sample_attn",
    )(page_table, qbd, cache)


def _tile_rows(m, tile=512):
    return tile if m % tile == 0 else m


PROJ_ROWS = 1024


def _nsa_weights(w_in, q_gain, ks_gain, kw_gain):
    d = w_in.shape[0]
    main = N_HEADS * HEAD_DIM + 6 * KV_HEADS * HEAD_DIM
    wg = w_in[:, main:].reshape(d, KV_HEADS, GROUP * 3)
    wg = jnp.pad(wg, ((0, 0), (0, 0), (0, HEAD_DIM - GROUP * 3))).reshape(d, KV_HEADS * HEAD_DIM)
    w = jnp.concatenate([w_in[:, :main], wg], axis=1).astype(BF16)
    kvh = KV_HEADS
    tiles = [("rope", 0, None)] * 4 + [
        ("plain", 1, (0, 0)), ("plain", 1, (0, kvh)), ("rope", 1, (0, 2 * kvh)), ("plain", 1, (0, 3 * kvh)),
        ("rope", 2, (1, 0)), ("plain", 2, (1, kvh)), ("sigmoid", 3, None)]
    ones = jnp.ones((HEAD_DIM,), F32)
    hg = jnp.stack([q_gain] * 4 + [ones, ones, ks_gain, ones, kw_gain, ones, ones])[:, None, :]
    outs = [(4, BF16), (4, F32), (2, F32), (1, F32)]
    return w, tiles, hg, outs


def _nsa_layer(xp, xs, b, t, cache, layer, page_table, win_all, norm_gain, w_in, w_out, q_gain, kc_gain,
               ks_gain, kw_gain, cmp_pos, cmp_w1, cmp_w2):
    bs = xs.shape[0]
    n_pages, psz = page_table.shape[1], cache.shape[2]
    past = n_pages * psz
    w, tiles, hg, outs = _nsa_weights(w_in, q_gain, ks_gain, kw_gain)
    w1, w2 = cmp_w1.astype(BF16), cmp_w2.astype(BF16)
    w_out = w_out.astype(BF16)

    cos_p, sin_p = _rope_tables(jnp.arange(t, dtype=jnp.int32))
    cos_pt, sin_pt = jnp.tile(cos_p, (b, 1)), jnp.tile(sin_p, (b, 1))
    cache_rows = (4 * KV_HEADS, 2 * KV_HEADS)
    q, rows, win, gates, rows_c, win_c = _norm_proj(xp, norm_gain, w, hg, cos_pt, sin_pt, tiles, outs,
                                                    _tile_rows(b * t, PROJ_ROWS), cache_rows=cache_rows)
    nc = t // CMP_BLOCK
    cos_c, sin_c = _rope_tables(jnp.arange(nc, dtype=jnp.int32) * CMP_BLOCK + (CMP_BLOCK - 1))
    k_cmp, v_cmp = _compress_prompt(rows, b, t, cmp_pos, w1, w2, kc_gain, cos_c, sin_c)
    o = _nsa_prompt_attn(q, rows, win, k_cmp, v_cmp, gates, b, t)
    xp = _out_proj(o, w_out, xp, _tile_rows(b * t, PROJ_ROWS))

    cos_s, sin_s = _rope_tables(jnp.full((bs,), past, jnp.int32))
    qs, _, _, gates_s, rows_s, win_s = _norm_proj(xs, norm_gain, w, hg, cos_s, sin_s, tiles, outs, bs,
                                                  cache_rows=cache_rows)
    ncs = past // CMP_BLOCK
    cos_cs, sin_cs = _rope_tables(jnp.arange(ncs, dtype=jnp.int32) * CMP_BLOCK + (CMP_BLOCK - 1))
    n_layers, n_pool = cache.shape[:2]
    k_cmp_s, v_cmp_s = _compress_sample(cache.reshape(n_layers, n_pool, psz, 2, 2 * KV_HEADS, HEAD_DIM), layer,
                                        page_table, cmp_pos, w1, w2, kc_gain, cos_cs, sin_cs)
    q4 = qs.reshape(bs, KV_HEADS, GROUP, HEAD_DIM)
    o_cmp, idx = _nsa_sample_cmp(q4, k_cmp_s, v_cmp_s, past)
    idx = idx[:, :, :N_SELECT - 1].reshape(bs, KV_HEADS * (N_SELECT - 1))
    halves = psz // CMP_BLOCK
    block_view = cache.reshape(n_layers, n_pool * halves, CMP_BLOCK * 4 * KV_HEADS, HEAD_DIM)
    nw = win_all.shape[2]
    win_view = win_all.reshape(win_all.shape[0], bs, nw * 2 * KV_HEADS, HEAD_DIM)
    o_s = _nsa_sample_slc(block_view, layer, page_table, idx, q4,
                          rows_s.reshape(bs, 4 * KV_HEADS, 1, HEAD_DIM), win_view, layer,
                          win_s.reshape(bs, 2 * KV_HEADS, 1, HEAD_DIM), o_cmp,
                          gates_s.reshape(bs, KV_HEADS, 1, HEAD_DIM), halves)
    xs = _out_proj(o_s.reshape(bs, N_HEADS * HEAD_DIM).astype(BF16), w_out, xs, bs)

    kv_p = rows_c.reshape(b, t, 4, KV_HEADS, HEAD_DIM)
    kv_s = rows_s.reshape(bs, 1, 4, KV_HEADS, HEAD_DIM)
    win_p = win_c.reshape(b, t, 2, KV_HEADS, HEAD_DIM)[:, -min(WINDOW, t):]
    win_new = jnp.concatenate([win_all[layer], win_s.reshape(bs, 1, 2, KV_HEADS, HEAD_DIM)], axis=1)[:, -nw:]
    return xp, xs, kv_p, kv_s, win_p, win_new


def _moba_layer(xp, xs, b, t, cache, layer, page_table, norm_gain, w_in, w_out, q_gain, k_gain):
    bs = xs.shape[0]
    n_pages, psz = page_table.shape[1], cache.shape[2]
    past = n_pages * psz
    w = w_in.astype(BF16)
    w_out = w_out.astype(BF16)
    tiles = [("rope", 0, None)] * 4 + [("rope", 1, (0, 0)), ("plain", 1, (0, KV_HEADS))]
    cache_rows = (2 * KV_HEADS,)
    ones = jnp.ones((HEAD_DIM,), F32)
    hg = jnp.stack([q_gain] * 4 + [k_gain, ones])[:, None, :]
    outs = [(4, BF16), (2, F32)]

    cos_p, sin_p = _rope_tables(jnp.arange(t, dtype=jnp.int32))
    cos_pt, sin_pt = jnp.tile(cos_p, (b, 1)), jnp.tile(sin_p, (b, 1))
    q, kv, kv_c = _norm_proj(xp, norm_gain, w, hg, cos_pt, sin_pt, tiles, outs, _tile_rows(b * t, PROJ_ROWS),
                             cache_rows=cache_rows)
    o = _moba_prompt_attn(q, kv, b, t)
    xp = _out_proj(o, w_out, xp, _tile_rows(b * t, PROJ_ROWS))

    cos_s, sin_s = _rope_tables(jnp.full((bs,), past, jnp.int32))
    qs, _, kv_s = _norm_proj(xs, norm_gain, w, hg, cos_s, sin_s, tiles, outs, bs, cache_rows=cache_rows)
    cache2 = cache.reshape(cache.shape[0], cache.shape[1], psz * 2 * KV_HEADS, HEAD_DIM)
    idx = _moba_sample_gate(cache2, layer, page_table, qs.reshape(bs, KV_HEADS, GROUP, HEAD_DIM))
    idx = idx.reshape(bs, N_HEADS, 128)[:, :, :MOBA_TOPK].reshape(bs, N_HEADS * MOBA_TOPK)
    o_s = _moba_sample_attn(cache2, layer, page_table, idx, qs.reshape(bs, N_HEADS, 1, HEAD_DIM),
                            kv_s.reshape(bs, 2 * KV_HEADS, 1, HEAD_DIM))
    xs = _out_proj(o_s.reshape(bs, N_HEADS * HEAD_DIM), w_out, xs, bs)
    return (xp, xs, kv_c.reshape(b, t, 2, KV_HEADS, HEAD_DIM), kv_s.reshape(bs, 1, 2, KV_HEADS, HEAD_DIM))


def _sb_layer(xp, xs, b, t, cache, layer, page_table, norm_gain, w_in, w_out):
    bs = xs.shape[0]
    psz = cache.shape[2]
    aw = N_HEADS * HEAD_DIM
    w = w_in.astype(BF16)
    w_out = w_out.astype(BF16)
    n_tiles = w.shape[1] // 512
    heads_per_tile = 512 // HEAD_DIM
    q_tiles = aw // 512
    tiles = [("plain", 0, None)] * q_tiles + [("plain", 0, (0, (jt - q_tiles) * heads_per_tile))
                                              for jt in range(q_tiles, n_tiles)]
    cache_rows = (2 * N_HEADS,)
    hg = jnp.ones((n_tiles, 1, HEAD_DIM), F32)
    outs = [(n_tiles, F32)]
    dummy = jnp.zeros((b * t, HEAD_DIM), F32)
    qkv, kv_c = _norm_proj(xp, norm_gain, w, hg, dummy, dummy, tiles, outs, _tile_rows(b * t, PROJ_ROWS),
                           cache_rows=cache_rows)
    o = _sb_prompt_attn(qkv, b, t)
    xp = _out_proj(o, w_out, xp, _tile_rows(b * t, PROJ_ROWS))

    qkv_s, kv_s = _norm_proj(xs, norm_gain, w, hg, dummy[:bs], dummy[:bs], tiles, outs, bs, cache_rows=cache_rows)
    qs = qkv_s[:, :aw].astype(BF16).reshape(bs, N_HEADS, HEAD_DIM)
    eye = jnp.eye(N_HEADS, 128, dtype=BF16)
    qbd = (qs[:, :, :, None] * eye[None, :, None, :]).reshape(bs, aw, 128)
    cache2 = cache.reshape(cache.shape[0], cache.shape[1], psz * 2 * N_HEADS, HEAD_DIM)
    o_s = _sb_sample_attn(cache2, layer, page_table, qbd)
    xs = _out_proj(o_s.reshape(bs, aw).astype(BF16), w_out, xs, bs)
    return (xp, xs, kv_c.reshape(b, t, 2, N_HEADS, HEAD_DIM), kv_s.reshape(bs, 1, 2, N_HEADS, HEAD_DIM))


def kernel(x_prompt, x_sample, cache_nsa_kv, state_nsa_win, cache_moba_kv, cache_sb_kv, page_table, mix_norm, ffn_norm, nsa_w_in, nsa_w_out, nsa_q_gain, nsa_kc_gain, nsa_ks_gain, nsa_kw_gain, nsa_cmp_pos, nsa_cmp_w1, nsa_cmp_w2, moba_w_in, moba_w_out, moba_q_gain, moba_k_gain, sb_w_in, sb_w_out, ffn_w_gate, ffn_w_up, ffn_w_down):
    b, t, d = x_prompt.shape
    bs, ts, _ = x_sample.shape
    assert ts == 1
    depth = mix_norm.shape[0]
    xp = x_prompt.reshape(b * t, d)
    xs = x_sample.reshape(bs * ts, d)
    nsa_kv_p, nsa_kv_s, nsa_win_p, nsa_win_s = [], [], [], []
    moba_kv_p, moba_kv_s, sb_kv_p, sb_kv_s = [], [], [], []
    wg, wu, wd = ffn_w_gate.astype(BF16), ffn_w_up.astype(BF16), ffn_w_down.astype(BF16)
    for i in range(depth):
        j = i // N_MIXERS
        if i % N_MIXERS == 0:
            xp, xs, kvp, kvs, wp, ws = _nsa_layer(
                xp, xs, b, t, cache_nsa_kv, j, page_table, state_nsa_win, mix_norm[i], nsa_w_in[j], nsa_w_out[j],
                nsa_q_gain[j], nsa_kc_gain[j], nsa_ks_gain[j], nsa_kw_gain[j], nsa_cmp_pos[j], nsa_cmp_w1[j],
                nsa_cmp_w2[j])
            nsa_kv_p.append(kvp)
            nsa_kv_s.append(kvs)
            nsa_win_p.append(wp)
            nsa_win_s.append(ws)
        elif i % N_MIXERS == 1:
            xp, xs, kvp, kvs = _moba_layer(xp, xs, b, t, cache_moba_kv, j, page_table, mix_norm[i], moba_w_in[j],
                                           moba_w_out[j], moba_q_gain[j], moba_k_gain[j])
            moba_kv_p.append(kvp)
            moba_kv_s.append(kvs)
        else:
            xp, xs, kvp, kvs = _sb_layer(xp, xs, b, t, cache_sb_kv, j, page_table, mix_norm[i], sb_w_in[j], sb_w_out[j])
            sb_kv_p.append(kvp)
            sb_kv_s.append(kvs)
        xp = _ffn(xp, ffn_norm[i], wg, wu, wd, i, _tile_rows(b * t))
        xs = _ffn(xs, ffn_norm[i], wg, wu, wd, i, bs * ts)
    return (xp.reshape(b, t, d), xs.reshape(bs, ts, d), jnp.stack(nsa_kv_p), jnp.stack(nsa_kv_s),
            jnp.stack(nsa_win_p), jnp.stack(nsa_win_s), jnp.stack(moba_kv_p), jnp.stack(moba_kv_s),
            jnp.stack(sb_kv_p), jnp.stack(sb_kv_s))
```
